```python
import math
import jax, jax.numpy as jnp
from jax import lax
import numpy as np

D_MODEL = 1024
BATCH = 4
SEQ = 8192
DEPTH = 2

GRID_W = 64
PLE_DIM = 256
N_EVEN = (DEPTH + 1) // 2
N_ODD = DEPTH // 2
EPS = 1e-6

RWKV_HEADS = 8
RWKV_HEAD_DIM = 64
RWKV_WIDTH = RWKV_HEADS * RWKV_HEAD_DIM
DECAY_RANK = 64
ICLR_RANK = 64
GATE_RANK = 128
RWKV_SHIFT_COLS = 3 * RWKV_WIDTH + DECAY_RANK + ICLR_RANK
RWKV_COLS = RWKV_SHIFT_COLS + GATE_RANK
RWKV_GN_EPS = 64e-5

HYENA_WIDTH = D_MODEL - RWKV_WIDTH
HYENA_COLS = 3 * HYENA_WIDTH
SHORT_CONV = 3
FILTER_EMB = 17
FILTER_HIDDEN = 64
DECAY_MIN = math.log(1e-2) / 1.5
DECAY_MAX = math.log(1e-2) / 0.3

IN_COLS = RWKV_COLS + HYENA_COLS

NA_HEADS = 16
NA_HEAD_DIM = D_MODEL // NA_HEADS
NA_WIN_ROWS_MAX = 8
NA_WIN_COLS = 16

D_FF = 2816
FFN_CONV = 3

kernel_name = "hybrid_rwkv7_hyena_natten_encoder"


def rmsnorm(x, g):
    xf = x.astype(jnp.float32)
    y = xf * lax.rsqrt(jnp.mean(xf * xf, axis=-1, keepdims=True) + EPS)
    return (y * g.astype(jnp.float32)).astype(x.dtype)


def dwconv_centred(x, w, b):
    K, C = w.shape
    y = lax.conv_general_dilated(x, w.reshape(K, 1, C).astype(x.dtype), window_strides=(1,),
                                 padding=[(K // 2, K // 2)], dimension_numbers=('NWC', 'WIO', 'NWC'),
                                 feature_group_count=C)
    return y + b


def rwkv7_bidir(z, mu, w0, w_up, a0, a_up, g_up, k_k, k_a, r_k, ln_w, ln_b):
    B, L, _ = z.shape
    H, N, C = RWKV_HEADS, RWKV_HEAD_DIM, RWKV_WIDTH
    zf = z.astype(jnp.float32)
    zs = zf[..., :RWKV_SHIFT_COLS]
    prev = jnp.pad(zs, ((0, 0), (1, 0), (0, 0)))[:, :L]
    nxt = jnp.pad(zs, ((0, 0), (0, 1), (0, 0)))[:, 1:]
    zd = jnp.stack([zs + (prev - zs) * mu[0], zs + (nxt - zs) * mu[1]])
    r = zd[..., :C]
    k = zd[..., C:2 * C]
    v = zd[..., 2 * C:3 * C]
    wd = zd[..., 3 * C:3 * C + DECAY_RANK]
    ad = zd[..., 3 * C + DECAY_RANK:]
    gd = zf[..., RWKV_SHIFT_COLS:]
    log_w = -jax.nn.softplus(-(w0[:, None, None] + jnp.einsum('dblr,drc->dblc', jnp.tanh(wd), w_up))) - 0.5
    w = jnp.exp(-jnp.exp(log_w))
    a = jax.nn.sigmoid(a0[:, None, None] + jnp.einsum('dblr,drc->dblc', ad, a_up))
    g = jax.nn.sigmoid(gd) @ g_up

    def heads(t):
        return t.reshape(2, B, L, H, N)

    kk = heads(k * k_k)
    kk = kk / jnp.maximum(jnp.sqrt(jnp.sum(kk * kk, axis=-1, keepdims=True)), 1e-12)
    k = heads(k * (1.0 + (a - 1.0) * k_a))
    r, w, v, a = heads(r), heads(w), heads(v), heads(a)

    def time_major(t):
        t = jnp.stack([t[0], jnp.flip(t[1], axis=1)])
        return jnp.moveaxis(t, 2, 0)

    def step(S, inp):
        r_t, w_t, k_t, v_t, kk_t, a_t = inp
        sa = jnp.einsum('dbhvk,dbhk->dbhv', S, kk_t)
        S = (S * w_t[..., None, :] - sa[..., :, None] * (kk_t * a_t)[..., None, :]
             + v_t[..., :, None] * k_t[..., None, :])
        return S, jnp.einsum('dbhvk,dbhk->dbhv', S, r_t)

    S0 = jnp.zeros((2, B, H, N, N), jnp.float32)
    seqs = (time_major(r), time_major(w), time_major(k), time_major(v), time_major(kk), time_major(a))
    _, ys = lax.scan(step, S0, seqs)
    ys = jnp.moveaxis(ys, 0, 2)
    y = ys[0] + jnp.flip(ys[1], axis=1)
    mean = jnp.mean(y, axis=-1, keepdims=True)
    var = jnp.mean(jnp.square(y - mean), axis=-1, keepdims=True)
    y = ((y - mean) * lax.rsqrt(var + RWKV_GN_EPS)).reshape(B, L, C) * ln_w + ln_b
    bonus = jnp.sum(jnp.sum(r * k * r_k, axis=-1, keepdims=True) * v, axis=0).reshape(B, L, C)
    return ((y + bonus) * g).astype(z.dtype)


def hyena_filters(L, w1, b1, w2, b2, w3, b3, w4, freq):
    f32 = jnp.float32
    t = jnp.linspace(0.0, 1.0, L, dtype=f32)[:, None]
    bands = (FILTER_EMB - 1) // 2
    ang = (2.0 * math.pi / L) * jnp.arange(L, dtype=f32)[:, None] * jnp.linspace(1e-4, bands - 1, bands, dtype=f32)[None]
    feats = jnp.concatenate([t, jnp.cos(ang), -jnp.sin(ang)], axis=-1)
    hdn = jnp.sin(freq * (feats @ w1 + b1))
    hdn = jnp.sin(freq * (hdn @ w2 + b2))
    hdn = jnp.sin(freq * (hdn @ w3 + b3))
    h = (hdn @ w4).reshape(L, 2, HYENA_WIDTH)
    deltas = jnp.abs(jnp.linspace(DECAY_MIN, DECAY_MAX, HYENA_WIDTH, dtype=f32))
    h = h * jnp.exp(-t * deltas)[:, None, :]
    return h * lax.rsqrt(jnp.sum(h * h, axis=(0, 1), keepdims=True) + EPS)


def hyena_bidir(z, short_w, short_b, f_w1, f_b1, f_w2, f_b2, f_w3, f_b3, f_w4, f_freq, bias):
    B, L, _ = z.shape
    C = HYENA_WIDTH
    u = dwconv_centred(z, short_w, short_b)
    x0, x1, v = u[..., :C], u[..., C:2 * C], u[..., 2 * C:]
    h = hyena_filters(L, f_w1.astype(jnp.float32), f_b1.astype(jnp.float32), f_w2.astype(jnp.float32),
                      f_b2.astype(jnp.float32), f_w3.astype(jnp.float32), f_b3.astype(jnp.float32),
                      f_w4.astype(jnp.float32), f_freq.astype(jnp.float32))
    kern = jnp.concatenate([h[:, 0], jnp.zeros((1, C), jnp.float32), jnp.flip(h[1:, 1], axis=0)], axis=0)
    s = (v * x1).astype(jnp.float32)
    y = jnp.fft.irfft(jnp.fft.rfft(s, n=2 * L, axis=1) * jnp.fft.rfft(kern, n=2 * L, axis=0)[None],
                      n=2 * L, axis=1)[:, :L]
    y = y + s * bias.astype(jnp.float32)
    return (y * x0.astype(jnp.float32)).astype(z.dtype)


def neighbourhood_attention(z, q_g, k_g, rpb):
    B, L, _ = z.shape
    rows = L // GRID_W
    kh, kw = min(NA_WIN_ROWS_MAX, rows), NA_WIN_COLS
    H, dh, D = NA_HEADS, NA_HEAD_DIM, D_MODEL

    def split(t):
        return jnp.transpose(t.reshape(B, rows, GRID_W, H, dh), (0, 3, 1, 2, 4))

    q = split(rmsnorm(z[..., :D].reshape(B, L, H, dh), q_g) * (dh ** -0.5))
    k = split(rmsnorm(z[..., D:2 * D].reshape(B, L, H, dh), k_g))
    v = split(z[..., 2 * D:])
    col = np.arange(GRID_W)
    cstart = np.clip(col - kw // 2, 0, GRID_W - kw)
    col_idx = cstart[:, None] + np.arange(kw)[None, :]
    dc_idx = col_idx - col[:, None] + (NA_WIN_COLS - 1)

    def row_block(i):
        rs = jnp.clip(i - kh // 2, 0, rows - kh)
        q_i = lax.dynamic_index_in_dim(q, i, axis=2, keepdims=False)
        k_win = lax.dynamic_slice_in_dim(k, rs, kh, axis=2)[:, :, :, col_idx]
        v_win = lax.dynamic_slice_in_dim(v, rs, kh, axis=2)[:, :, :, col_idx]
        s = jnp.einsum('bhqd,bhrqwd->bhqrw', q_i, k_win).astype(jnp.float32)
        dr_idx = rs + jnp.arange(kh) - i + (NA_WIN_ROWS_MAX - 1)
        bias = rpb[:, dr_idx][:, :, dc_idx]
        s = s + jnp.transpose(bias, (0, 2, 1, 3)).astype(jnp.float32)[None]
        pr = jax.nn.softmax(s.reshape(B, H, GRID_W, kh * kw), axis=-1).reshape(B, H, GRID_W, kh, kw)
        return jnp.einsum('bhqrw,bhrqwd->bhqd', pr.astype(v.dtype), v_win)

    o = lax.map(row_block, jnp.arange(rows))
    return jnp.transpose(o, (1, 0, 3, 2, 4)).reshape(B, L, D)


def conv_ffn(h, norm_g, w_up, conv_w, conv_b, w_down):
    u = dwconv_centred(rmsnorm(h, norm_g) @ w_up, conv_w, conv_b)
    return (jax.nn.gelu(u[..., :D_FF]) * u[..., D_FF:]) @ w_down


def per_layer_embedding(h, p_i, norm_g, w_gate, w_proj):
    return jax.nn.sigmoid(rmsnorm(h, norm_g) @ w_gate) * (p_i @ w_proj)


def setup_inputs(seed: int = 0) -> dict:
    key = jax.random.key(seed)
    keys = jax.random.split(key, 64)
    counter = [0]
    f32 = jnp.float32

    def nk():
        counter[0] += 1
        return keys[counter[0] - 1]

    def nrm(shape, scale):
        return scale * jax.random.normal(nk(), shape, f32)

    def gain(shape):
        return 1.0 + 0.02 * jax.random.normal(nk(), shape, f32)

    def unif(shape, lo, hi):
        return jax.random.uniform(nk(), shape, f32, lo, hi)

    D, E, O, C, Ch = D_MODEL, N_EVEN, N_ODD, RWKV_WIDTH, HYENA_WIDTH
    return {
        "x": nrm((BATCH, SEQ, D), 1.0),
        "p": nrm((DEPTH, BATCH, SEQ, PLE_DIM), 1.0),
        "mix_norm": gain((E, D)),
        "mix_w_in": nrm((E, D, IN_COLS), D ** -0.5),
        "rwkv_mu": unif((E, 2, RWKV_SHIFT_COLS), 0.0, 1.0),
        "rwkv_w0": unif((E, 2, C), -6.0, -0.5),
        "rwkv_w_up": nrm((E, 2, DECAY_RANK, C), 0.1),
        "rwkv_a0": nrm((E, 2, C), 0.1),
        "rwkv_a_up": nrm((E, 2, ICLR_RANK, C), 0.1),
        "rwkv_g_up": nrm((E, GATE_RANK, C), GATE_RANK ** -0.5),
        "rwkv_k_k": 0.85 + 0.02 * jax.random.normal(nk(), (E, C), f32),
        "rwkv_k_a": gain((E, C)),
        "rwkv_r_k": nrm((E, RWKV_HEADS, RWKV_HEAD_DIM), 0.1),
        "rwkv_ln_w": gain((E, C)),
        "rwkv_ln_b": nrm((E, C), 0.02),
        "hy_short_w": nrm((E, SHORT_CONV, HYENA_COLS), SHORT_CONV ** -0.5),
        "hy_short_b": nrm((E, HYENA_COLS), 0.02),
        "hy_f_w1": nrm((E, FILTER_EMB, FILTER_HIDDEN), FILTER_EMB ** -0.5),
        "hy_f_b1": nrm((E, FILTER_HIDDEN), 0.1),
        "hy_f_w2": nrm((E, FILTER_HIDDEN, FILTER_HIDDEN), FILTER_HIDDEN ** -0.5),
        "hy_f_b2": nrm((E, FILTER_HIDDEN), 0.1),
        "hy_f_w3": nrm((E, FILTER_HIDDEN, FILTER_HIDDEN), FILTER_HIDDEN ** -0.5),
        "hy_f_b3": nrm((E, FILTER_HIDDEN), 0.1),
        "hy_f_w4": nrm((E, FILTER_HIDDEN, 2 * Ch), FILTER_HIDDEN ** -0.5),
        "hy_f_freq": gain((E, FILTER_HIDDEN)),
        "hy_bias": nrm((E, Ch), 1.0),
        "mix_w_out": nrm((E, D, D), D ** -0.5),
        "na_norm": gain((O, D)),
        "na_w_qkv": nrm((O, D, 3 * D), D ** -0.5),
        "na_q_g": gain((O, NA_HEAD_DIM)),
        "na_k_g": gain((O, NA_HEAD_DIM)),
        "na_rpb": nrm((O, NA_HEADS, 2 * NA_WIN_ROWS_MAX - 1, 2 * NA_WIN_COLS - 1), 0.1),
        "na_w_out": nrm((O, D, D), D ** -0.5),
        "ffn_norm": gain((DEPTH, D)),
        "ffn_w_up": nrm((DEPTH, D, 2 * D_FF), D ** -0.5),
        "ffn_conv_w": nrm((DEPTH, FFN_CONV, 2 * D_FF), FFN_CONV ** -0.5),
        "ffn_conv_b": nrm((DEPTH, 2 * D_FF), 0.02),
        "ffn_w_down": nrm((DEPTH, D_FF, D), D_FF ** -0.5),
        "ple_norm": gain((DEPTH, D)),
        "ple_w_gate": nrm((DEPTH, D, D), D ** -0.5),
        "ple_w_proj": nrm((DEPTH, PLE_DIM, D), PLE_DIM ** -0.5),
    }


def reference(x, p, mix_norm, mix_w_in, rwkv_mu, rwkv_w0, rwkv_w_up, rwkv_a0, rwkv_a_up, rwkv_g_up,
              rwkv_k_k, rwkv_k_a, rwkv_r_k, rwkv_ln_w, rwkv_ln_b, hy_short_w, hy_short_b,
              hy_f_w1, hy_f_b1, hy_f_w2, hy_f_b2, hy_f_w3, hy_f_b3, hy_f_w4, hy_f_freq, hy_bias,
              mix_w_out, na_norm, na_w_qkv, na_q_g, na_k_g, na_rpb, na_w_out,
              ffn_norm, ffn_w_up, ffn_conv_w, ffn_conv_b, ffn_w_down,
              ple_norm, ple_w_gate, ple_w_proj):
    h = x
    for i in range(DEPTH):
        j = i // 2
        if i % 2 == 0:
            z = rmsnorm(h, mix_norm[j]) @ mix_w_in[j]
            y_a = rwkv7_bidir(z[..., :RWKV_COLS], rwkv_mu[j], rwkv_w0[j], rwkv_w_up[j], rwkv_a0[j],
                              rwkv_a_up[j], rwkv_g_up[j], rwkv_k_k[j], rwkv_k_a[j], rwkv_r_k[j],
                              rwkv_ln_w[j], rwkv_ln_b[j])
            y_b = hyena_bidir(z[..., RWKV_COLS:], hy_short_w[j], hy_short_b[j], hy_f_w1[j], hy_f_b1[j],
                              hy_f_w2[j], hy_f_b2[j], hy_f_w3[j], hy_f_b3[j], hy_f_w4[j], hy_f_freq[j],
                              hy_bias[j])
            h = h + jnp.concatenate([y_a, y_b], axis=-1) @ mix_w_out[j]
        else:
            z = rmsnorm(h, na_norm[j]) @ na_w_qkv[j]
            h = h + neighbourhood_attention(z, na_q_g[j], na_k_g[j], na_rpb[j]) @ na_w_out[j]
        h = h + conv_ffn(h, ffn_norm[i], ffn_w_up[i], ffn_conv_w[i], ffn_conv_b[i], ffn_w_down[i])
        h = h + per_layer_embedding(h, p[i], ple_norm[i], ple_w_gate[i], ple_w_proj[i])
    return h
```

```python
import functools
import math

import numpy as np
import jax
import jax.numpy as jnp
from jax import lax
from jax.experimental import pallas as pl
from jax.experimental.pallas import tpu as pltpu

F32 = jnp.float32
BF16 = jnp.bfloat16
HIGHEST = lax.Precision.HIGHEST

EPS = 1e-6
GRID_W = 64
RWKV_HEADS = 8
RWKV_HEAD_DIM = 64
RWKV_WIDTH = RWKV_HEADS * RWKV_HEAD_DIM
DECAY_RANK = 64
ICLR_RANK = 64
GATE_RANK = 128
RWKV_SHIFT_COLS = 3 * RWKV_WIDTH + DECAY_RANK + ICLR_RANK
RWKV_COLS = RWKV_SHIFT_COLS + GATE_RANK
RWKV_GN_EPS = 64e-5
RWKV_CHUNK = 64
HYENA_WIDTH = 512
FILTER_EMB = 17
FILTER_EMB_PAD = 32
DECAY_MIN = math.log(1e-2) / 1.5
DECAY_MAX = math.log(1e-2) / 0.3
NA_HEADS = 16
NA_HEAD_DIM = 64
NA_WIN_ROWS = 8
NA_WIN_COLS = 16
NEG_BIG = -1e30

LANES = 128
V7X_VMEM_BYTES = 64 * 1024 * 1024
VMEM_CAP = V7X_VMEM_BYTES - 8 * 1024 * 1024


def _params(semantics, vmem_bytes):
    return pltpu.CompilerParams(dimension_semantics=semantics,
                                vmem_limit_bytes=int(min(max(vmem_bytes, 16 * 1024 * 1024), VMEM_CAP)))


def _rms_rows(x, g):
    ms = jnp.mean(x * x, axis=-1, keepdims=True)
    return x * lax.rsqrt(ms + EPS) * g


def _bdot(a, b):
    return jnp.dot(a.astype(BF16), b.astype(BF16), preferred_element_type=F32)


def _bdot_nt(a, b):
    return lax.dot_general(a.astype(BF16), b.astype(BF16), (((1,), (1,)), ((), ())),
                           preferred_element_type=F32)


def _bdot_tn(a, b):
    return lax.dot_general(a.astype(BF16), b.astype(BF16), (((0,), (0,)), ((), ())),
                           preferred_element_type=F32)


def _split2(x):
    hi = x.astype(BF16)
    lo = (x - hi.astype(F32)).astype(BF16)
    return hi, lo


def _dot3(a, b):
    ah, al = _split2(a)
    bh, bl = _split2(b)
    d = functools.partial(jnp.dot, preferred_element_type=F32)
    return d(ah, bh) + (d(ah, bl) + d(al, bh))


def _norm_matmul_body(x_ref, g_ref, w_ref, o_ref, xn_ref):
    @pl.when(pl.program_id(1) == 0)
    def _():
        xn_ref[...] = _rms_rows(x_ref[...], g_ref[...]).astype(BF16)

    o_ref[...] = jnp.dot(xn_ref[...], w_ref[...], preferred_element_type=F32).astype(o_ref.dtype)


def norm_matmul(x, g, w, tm, tn, out_dtype=F32):
    t, d = x.shape
    n = w.shape[1]
    osz = jnp.dtype(out_dtype).itemsize
    vmem = 2 * (tm * d * 4 + d * tn * 2 + tm * tn * osz) + tm * d * 2 + (4 << 20)
    return pl.pallas_call(
        _norm_matmul_body,
        grid=(t // tm, n // tn),
        in_specs=[pl.BlockSpec((tm, d), lambda i, j: (i, 0)),
                  pl.BlockSpec((1, d), lambda i, j: (0, 0)),
                  pl.BlockSpec((d, tn), lambda i, j: (0, j))],
        out_specs=pl.BlockSpec((tm, tn), lambda i, j: (i, j)),
        out_shape=jax.ShapeDtypeStruct((t, n), out_dtype),
        scratch_shapes=[pltpu.VMEM((tm, d), BF16)],
        compiler_params=_params(("parallel", "arbitrary"), vmem),
        name="norm_matmul",
    )(x, g.reshape(1, d), w)


def _matmul_residual_body(*refs, n_in):
    xs = refs[:n_in]
    ws = refs[n_in:2 * n_in]
    h_ref = refs[2 * n_in]
    o_ref = refs[2 * n_in + 1]
    acc = h_ref[...]
    for x_ref, w_ref in zip(xs, ws):
        acc = acc + jnp.dot(x_ref[...], w_ref[...], preferred_element_type=F32)
    o_ref[...] = acc


def matmul_residual(xs, ws, h, tm, tn):
    t, n = h.shape
    n_in = len(xs)
    in_specs = ([pl.BlockSpec((tm, x.shape[1]), lambda i, j: (i, 0)) for x in xs]
                + [pl.BlockSpec((w.shape[0], tn), lambda i, j: (0, j)) for w in ws]
                + [pl.BlockSpec((tm, tn), lambda i, j: (i, j))])
    ksum = sum(x.shape[1] for x in xs)
    vmem = 2 * (tm * ksum * 2 + ksum * tn * 2 + 2 * tm * tn * 4) + (4 << 20)
    return pl.pallas_call(
        functools.partial(_matmul_residual_body, n_in=n_in),
        grid=(t // tm, n // tn),
        in_specs=in_specs,
        out_specs=pl.BlockSpec((tm, tn), lambda i, j: (i, j)),
        out_shape=jax.ShapeDtypeStruct((t, n), F32),
        compiler_params=_params(("parallel", "parallel"), vmem),
        name="matmul_residual",
    )(*xs, *ws, h)


FFN_HALO = 16


def _ffn_body(xm_ref, xp_ref, xx_ref, g_ref, wa_ref, wb_ref, cwa_ref, cwb_ref, cba_ref, cbb_ref,
              wd_ref, o_ref, xn_ref, *, tm, tiles_per_seq):
    i = pl.program_id(0)
    j = pl.program_id(1)
    hl = FFN_HALO

    @pl.when(j == 0)
    def _():
        g = g_ref[...]
        first = (i % tiles_per_seq) == 0
        last = (i % tiles_per_seq) == tiles_per_seq - 1
        xm = xm_ref[...]
        xn_ref[hl:hl + tm, :] = _rms_rows(xm, g).astype(BF16)
        xn_ref[0:hl, :] = jnp.where(first, 0.0, _rms_rows(xp_ref[...], g)).astype(BF16)
        xn_ref[hl + tm:2 * hl + tm, :] = jnp.where(last, 0.0, _rms_rows(xx_ref[...], g)).astype(BF16)
        o_ref[...] = xm

    xn = xn_ref[...]
    rows = tm + 2 * hl

    def conv(z, cw_ref, cb_ref):
        cw = cw_ref[...]
        zp = pltpu.roll(z, 1, 0)
        zn = pltpu.roll(z, rows - 1, 0)
        u = zp * cw[0:1, :] + z * cw[1:2, :] + zn * cw[2:3, :] + cb_ref[...]
        return u[hl:hl + tm, :]

    ua = conv(jnp.dot(xn, wa_ref[...], preferred_element_type=F32), cwa_ref, cba_ref)
    ub = conv(jnp.dot(xn, wb_ref[...], preferred_element_type=F32), cwb_ref, cbb_ref)
    act = (jax.nn.gelu(ua) * ub).astype(BF16)
    o_ref[...] += jnp.dot(act, wd_ref[...], preferred_element_type=F32)


def conv_ffn_residual(h, norm_g, w_up, conv_w, conv_b, w_down, seq_len, tm, tf):
    t, d = h.shape
    f = w_down.shape[0]
    nf = f // tf
    hl = FFN_HALO
    tiles_per_seq = seq_len // tm
    nhb = t // hl
    body = functools.partial(_ffn_body, tm=tm, tiles_per_seq=tiles_per_seq)
    vmem = (2 * (tm * d * 4 + 2 * hl * d * 4 + 2 * d * tf * 2 + tf * d * 2 + tm * d * 4)
            + (tm + 2 * hl) * d * 2 + 10 * (tm + 2 * hl) * tf * 4 + (4 << 20))
    return pl.pallas_call(
        body,
        grid=(t // tm, nf),
        in_specs=[
            pl.BlockSpec((tm, d), lambda i, j: (i, 0)),
            pl.BlockSpec((hl, d), lambda i, j: (jnp.maximum(i * (tm // hl) - 1, 0), 0)),
            pl.BlockSpec((hl, d), lambda i, j: (jnp.minimum((i + 1) * (tm // hl), nhb - 1), 0)),
            pl.BlockSpec((1, d), lambda i, j: (0, 0)),
            pl.BlockSpec((d, tf), lambda i, j: (0, j)),
            pl.BlockSpec((d, tf), lambda i, j: (0, nf + j)),
            pl.BlockSpec((3, tf), lambda i, j: (0, j)),
            pl.BlockSpec((3, tf), lambda i, j: (0, nf + j)),
            pl.BlockSpec((1, tf), lambda i, j: (0, j)),
            pl.BlockSpec((1, tf), lambda i, j: (0, nf + j)),
            pl.BlockSpec((tf, d), lambda i, j: (j, 0)),
        ],
        out_specs=pl.BlockSpec((tm, d), lambda i, j: (i, 0)),
        out_shape=jax.ShapeDtypeStruct((t, d), F32),
        scratch_shapes=[pltpu.VMEM((tm + 2 * hl, d), BF16)],
        compiler_params=_params(("parallel", "arbitrary"), vmem),
        name="conv_ffn",
    )(h, h, h, norm_g.reshape(1, d), w_up, w_up, conv_w, conv_w,
      conv_b.reshape(1, 2 * f), conv_b.reshape(1, 2 * f), w_down)


def _ple_body(h_ref, hc_ref, g_ref, wg_ref, p_ref, wp_ref, o_ref, xn_ref):
    @pl.when(pl.program_id(1) == 0)
    def _():
        xn_ref[...] = _rms_rows(h_ref[...], g_ref[...]).astype(BF16)

    gate = jax.nn.sigmoid(jnp.dot(xn_ref[...], wg_ref[...], preferred_element_type=F32))
    proj = jnp.dot(p_ref[...].astype(BF16), wp_ref[...], preferred_element_type=F32)
    o_ref[...] = hc_ref[...] + gate * proj


def ple_residual(h, p, norm_g, w_gate, w_proj, tm, tn):
    t, d = h.shape
    pd = p.shape[1]
    vmem = 2 * (tm * d * 4 + 2 * tm * tn * 4 + d * tn * 2 + tm * pd * 4 + pd * tn * 2) + tm * d * 2 + (4 << 20)
    return pl.pallas_call(
        _ple_body,
        grid=(t // tm, d // tn),
        in_specs=[pl.BlockSpec((tm, d), lambda i, j: (i, 0)),
                  pl.BlockSpec((tm, tn), lambda i, j: (i, j)),
                  pl.BlockSpec((1, d), lambda i, j: (0, 0)),
                  pl.BlockSpec((d, tn), lambda i, j: (0, j)),
                  pl.BlockSpec((tm, pd), lambda i, j: (i, 0)),
                  pl.BlockSpec((pd, tn), lambda i, j: (0, j))],
        out_specs=pl.BlockSpec((tm, tn), lambda i, j: (i, j)),
        out_shape=jax.ShapeDtypeStruct((t, d), F32),
        scratch_shapes=[pltpu.VMEM((tm, d), BF16)],
        compiler_params=_params(("parallel", "arbitrary"), vmem),
        name="ple",
    )(h, h, norm_g.reshape(1, d), w_gate, p, w_proj)


def _qknorm_body(z_ref, sel_ref, selt_ref, qg_ref, kg_ref, q_ref, k_ref, v_ref, *, d):
    sel = sel_ref[...]
    selt = selt_ref[...]

    def head_rms(x, g):
        ss = _bdot(x * x, sel)
        inv = lax.rsqrt(ss * (1.0 / NA_HEAD_DIM) + EPS)
        invb = jnp.dot(inv, selt, preferred_element_type=F32, precision=HIGHEST)
        return x * invb * g

    q_ref[...] = (head_rms(z_ref[:, 0:d], qg_ref[...]) * (NA_HEAD_DIM ** -0.5)).astype(BF16)
    k_ref[...] = head_rms(z_ref[:, d:2 * d], kg_ref[...]).astype(BF16)
    v_ref[...] = z_ref[:, 2 * d:3 * d].astype(BF16)


def qk_norm(z, q_g, k_g, tm):
    t, d3 = z.shape
    d = d3 // 3
    heads = d // NA_HEAD_DIM
    sel_np = np.zeros((d, LANES), np.float32)
    sel_np[np.arange(d), np.arange(d) // NA_HEAD_DIM] = 1.0
    sel = jnp.asarray(sel_np, BF16)
    selt = jnp.asarray(sel_np.T, F32)
    qg = jnp.tile(q_g.reshape(1, NA_HEAD_DIM), (1, heads))
    kg = jnp.tile(k_g.reshape(1, NA_HEAD_DIM), (1, heads))
    out = jax.ShapeDtypeStruct((t, d), BF16)
    vmem = 2 * (tm * d3 * 4 + 3 * tm * d * 2) + 8 * tm * d * 4 + (4 << 20)
    return pl.pallas_call(
        functools.partial(_qknorm_body, d=d),
        grid=(t // tm,),
        in_specs=[pl.BlockSpec((tm, d3), lambda i: (i, 0)),
                  pl.BlockSpec((d, LANES), lambda i: (0, 0)),
                  pl.BlockSpec((LANES, d), lambda i: (0, 0)),
                  pl.BlockSpec((1, d), lambda i: (0, 0)),
                  pl.BlockSpec((1, d), lambda i: (0, 0))],
        out_specs=[pl.BlockSpec((tm, d), lambda i: (i, 0))] * 3,
        out_shape=[out, out, out],
        compiler_params=_params(("parallel",), vmem),
        name="qk_norm",
    )(z, sel, selt, qg, kg)


def _natten_body(q_ref, k_ref, v_ref, b_ref, o_ref, *, rows, rb):
    ib = pl.program_id(2)
    lane = lax.broadcasted_iota(jnp.int32, (GRID_W, LANES), 1)
    first = lane < NA_HEAD_DIM
    kwin = NA_WIN_ROWS * GRID_W

    def one_row(r, carry):
        i = ib * rb + r
        rs = jnp.clip(i - NA_WIN_ROWS // 2, 0, rows - NA_WIN_ROWS)
        var = i - rs
        q = q_ref[0, pl.ds(pl.multiple_of(r * GRID_W, GRID_W), GRID_W), :]
        kstart = pl.multiple_of(rs * GRID_W, GRID_W)
        k = k_ref[0, pl.ds(kstart, kwin), :]
        v = v_ref[0, pl.ds(kstart, kwin), :]
        outs = []
        for a in range(2):
            keep = first if a == 0 else jnp.logical_not(first)
            qa = jnp.where(keep, q, jnp.zeros_like(q))
            s = lax.dot_general(qa, k, (((1,), (1,)), ((), ())), preferred_element_type=F32)
            s = s + b_ref[a, var]
            m = jnp.max(s, axis=-1, keepdims=True)
            p = jnp.exp(s - m)
            l = jnp.sum(p, axis=-1, keepdims=True)
            o = jnp.dot(p.astype(BF16), v, preferred_element_type=F32)
            outs.append(o / l)
        o_ref[0, pl.ds(pl.multiple_of(r * GRID_W, GRID_W), GRID_W), :] = (
            jnp.where(first, outs[0], outs[1]).astype(o_ref.dtype))
        return carry

    lax.fori_loop(0, rb, one_row, 0)


def _natten_bias(rpb):
    heads = rpb.shape[0]
    kh, kw, w = NA_WIN_ROWS, NA_WIN_COLS, GRID_W
    var = np.arange(kh)
    jr = np.arange(kh)
    dr = jr[None, :] + (kh - 1) - var[:, None]
    col = np.arange(w)
    cstart = np.clip(col - kw // 2, 0, w - kw)
    dc = col[None, :] - col[:, None] + (kw - 1)
    valid = (col[None, :] >= cstart[:, None]) & (col[None, :] < cstart[:, None] + kw)
    dcc = np.clip(dc, 0, 2 * kw - 2)
    b = rpb[:, dr[:, :, None, None], dcc[None, None, :, :]]
    b = jnp.where(jnp.asarray(valid)[None, None, None], b, NEG_BIG)
    b = jnp.transpose(b, (0, 1, 3, 2, 4)).reshape(heads, kh, w, kh * w)
    return b.astype(F32)


def neighbourhood_attention(q, k, v, rpb, rb):
    bsz, seq, d = q.shape
    rows = seq // GRID_W
    bias = _natten_bias(rpb)
    hg = d // LANES
    kwin = NA_WIN_ROWS * GRID_W
    vmem = (2 * (2 * seq * LANES * 2 + 2 * rb * GRID_W * LANES * 2 + 2 * NA_WIN_ROWS * GRID_W * kwin * 4)
            + 16 * GRID_W * kwin * 4 + (4 << 20))
    return pl.pallas_call(
        functools.partial(_natten_body, rows=rows, rb=rb),
        grid=(bsz, hg, rows // rb),
        in_specs=[pl.BlockSpec((1, rb * GRID_W, LANES), lambda b, h, i: (b, i, h)),
                  pl.BlockSpec((1, seq, LANES), lambda b, h, i: (b, 0, h)),
                  pl.BlockSpec((1, seq, LANES), lambda b, h, i: (b, 0, h)),
                  pl.BlockSpec((2, NA_WIN_ROWS, GRID_W, kwin), lambda b, h, i: (h, 0, 0, 0))],
        out_specs=pl.BlockSpec((1, rb * GRID_W, LANES), lambda b, h, i: (b, i, h)),
        out_shape=jax.ShapeDtypeStruct((bsz, seq, d), BF16),
        compiler_params=_params(("parallel", "parallel", "arbitrary"), vmem),
        name="natten",
    )(q, k, v, bias)


def _hy_filter_body(f_ref, w1_ref, b1_ref, w2_ref, b2_ref, w3_ref, b3_ref, w4_ref, fr_ref, dl_ref,
                    h_ref, ss_ref):
    d = functools.partial(jnp.dot, preferred_element_type=F32, precision=HIGHEST)
    f = f_ref[...]
    fr = fr_ref[...]
    x = jnp.sin(fr * (d(f, w1_ref[...]) + b1_ref[...]))
    x = jnp.sin(fr * (d(x, w2_ref[...]) + b2_ref[...]))
    x = jnp.sin(fr * (d(x, w3_ref[...]) + b3_ref[...]))
    h = d(x, w4_ref[...]) * jnp.exp(-f[:, 0:1] * dl_ref[...])
    h_ref[...] = h

    @pl.when(pl.program_id(0) == 0)
    def _():
        ss_ref[...] = jnp.zeros_like(ss_ref)

    ss_ref[...] += jnp.sum(h * h, axis=0, keepdims=True)


def hyena_filter(seq, w1, b1, w2, b2, w3, b3, w4, freq, tl):
    c2 = w4.shape[1]
    hid = w1.shape[1]
    t = np.linspace(0.0, 1.0, seq, dtype=np.float32)[:, None]
    bands = (FILTER_EMB - 1) // 2
    ang = (np.float32(2.0 * math.pi / seq) * np.arange(seq, dtype=np.float32)[:, None]
           * np.linspace(1e-4, bands - 1, bands, dtype=np.float32)[None])
    feats = np.concatenate([t, np.cos(ang), -np.sin(ang),
                            np.zeros((seq, FILTER_EMB_PAD - FILTER_EMB), np.float32)], axis=-1)
    w1p = jnp.concatenate([w1, jnp.zeros((FILTER_EMB_PAD - FILTER_EMB, hid), F32)], axis=0)
    deltas = np.abs(np.linspace(DECAY_MIN, DECAY_MAX, c2 // 2, dtype=np.float32))
    deltas = np.tile(deltas, 2)[None]
    full = lambda a: pl.BlockSpec(a.shape, lambda i: (0,) * a.ndim)
    args = [jnp.asarray(feats), w1p, b1.reshape(1, hid), w2, b2.reshape(1, hid), w3, b3.reshape(1, hid),
            w4, freq.reshape(1, hid), jnp.asarray(deltas)]
    in_specs = [pl.BlockSpec((tl, FILTER_EMB_PAD), lambda i: (i, 0))] + [full(a) for a in args[1:]]
    return pl.pallas_call(
        _hy_filter_body,
        grid=(seq // tl,),
        in_specs=in_specs,
        out_specs=[pl.BlockSpec((tl, c2), lambda i: (i, 0)), pl.BlockSpec((1, c2), lambda i: (0, 0))],
        out_shape=[jax.ShapeDtypeStruct((seq, c2), F32), jax.ShapeDtypeStruct((1, c2), F32)],
        compiler_params=_params(("arbitrary",), 32 << 20),
        name="hyena_filter",
    )(*args)


def _hy_gate_body(z0_ref, z1_ref, zv_ref, w0_ref, w1_ref, wv_ref, b0_ref, b1_ref, bv_ref, s_ref, x0_ref,
                  *, seq):
    row = lax.broadcasted_iota(jnp.int32, (seq, LANES), 0)
    top = row == 0
    bot = row == seq - 1

    def conv(z_ref, w_ref, b_ref):
        z = z_ref[0]
        w = w_ref[...]
        zp = jnp.where(top, 0.0, pltpu.roll(z, 1, 0))
        zn = jnp.where(bot, 0.0, pltpu.roll(z, seq - 1, 0))
        return zp * w[0:1, :] + z * w[1:2, :] + zn * w[2:3, :] + b_ref[...]

    x0_ref[0] = conv(z0_ref, w0_ref, b0_ref)
    s_ref[0] = conv(zv_ref, wv_ref, bv_ref) * conv(z1_ref, w1_ref, b1_ref)


def hyena_gate(z, short_w, short_b, col0):
    bsz, seq, _ = z.shape
    c = HYENA_WIDTH
    nb = c // LANES
    o0 = col0 // LANES
    sb = short_b.reshape(1, 3 * c)
    zspec = lambda off: pl.BlockSpec((1, seq, LANES), lambda b, j: (b, 0, off + j))
    wspec = lambda off: pl.BlockSpec((3, LANES), lambda b, j: (0, off + j))
    bspec = lambda off: pl.BlockSpec((1, LANES), lambda b, j: (0, off + j))
    out = jax.ShapeDtypeStruct((bsz, seq, c), F32)
    return pl.pallas_call(
        functools.partial(_hy_gate_body, seq=seq),
        grid=(bsz, nb),
        in_specs=[zspec(o0), zspec(o0 + nb), zspec(o0 + 2 * nb),
                  wspec(0), wspec(nb), wspec(2 * nb), bspec(0), bspec(nb), bspec(2 * nb)],
        out_specs=[pl.BlockSpec((1, seq, LANES), lambda b, j: (b, 0, j))] * 2,
        out_shape=[out, out],
        compiler_params=_params(("parallel", "parallel"), 2 * 5 * seq * LANES * 4 + 8 * seq * LANES * 4 + (4 << 20)),
        name="hyena_gate",
    )(z, z, z, short_w, short_w, short_w, sb, sb, sb)


def _dft_consts(n, n_in):
    k = np.arange(n)[:, None].astype(np.float64)
    t = np.arange(n_in)[None, :].astype(np.float64)
    th = 2.0 * np.pi * k * t / n
    w_first = np.concatenate([np.cos(th), -np.sin(th)], axis=0)
    w_last = np.concatenate([np.cos(th).T, -np.sin(th).T], axis=1)
    tt = np.arange(n)[None, :].astype(np.float64)
    th2 = 2.0 * np.pi * k * tt / n
    cm, sm = np.cos(th2), np.sin(th2)
    m_mid = np.block([[cm, sm], [-sm, cm]])
    ph = 2.0 * np.pi * k * tt / (n * n)
    tw = np.stack([np.cos(ph), np.sin(ph)], axis=1)[..., None]
    return w_first, w_last, m_mid, tw


def _hy_stage1_body(w_ref, x_ref, o_ref, *, tb):
    w = w_ref[...]
    for j in range(tb):
        o_ref[0, j] = jnp.dot(w, x_ref[0, j], preferred_element_type=F32).astype(o_ref.dtype)


def hyena_stage(w, x, tb, out_dtype):
    bsz, n2, kk, c = x.shape
    m = w.shape[0]
    osz = jnp.dtype(out_dtype).itemsize
    vmem = 2 * (tb * kk * c * 2 + tb * m * c * osz + m * kk * 2) + 2 * m * c * 4 + (4 << 20)
    return pl.pallas_call(
        functools.partial(_hy_stage1_body, tb=tb),
        grid=(bsz, n2 // tb),
        in_specs=[pl.BlockSpec((m, kk), lambda b, j: (0, 0)),
                  pl.BlockSpec((1, tb, kk, c), lambda b, j: (b, j, 0, 0))],
        out_specs=pl.BlockSpec((1, tb, m, c), lambda b, j: (b, j, 0, 0)),
        out_shape=jax.ShapeDtypeStruct((bsz, n2, m, c), out_dtype),
        compiler_params=_params(("parallel", "parallel"), vmem),
        name="hyena_dft_outer",
    )(w, x)


def _twiddle(xr, xi, tc, ts, sign):
    return xr * tc + sign * (xi * ts), xi * tc - sign * (xr * ts)


def _hy_spec_body(m_ref, tw_ref, a_ref, o_ref, *, n):
    a = a_ref[0, 0].astype(F32)
    tc = tw_ref[0, 0]
    ts = tw_ref[0, 1]
    xr, xi = _twiddle(a[0:n], a[n:2 * n], tc, ts, 1.0)
    x = jnp.concatenate([xr, xi], axis=0).astype(BF16)
    o_ref[0, 0] = jnp.dot(m_ref[...], x, preferred_element_type=F32)


def _hy_mid_body(m_ref, mt_ref, tw_ref, hs_ref, a_ref, o_ref, *, n):
    a = a_ref[0, 0].astype(F32)
    tc = tw_ref[0, 0]
    ts = tw_ref[0, 1]
    xr, xi = _twiddle(a[0:n], a[n:2 * n], tc, ts, 1.0)
    x = jnp.concatenate([xr, xi], axis=0).astype(BF16)
    s = jnp.dot(m_ref[...], x, preferred_element_type=F32)
    hs = hs_ref[0, 0]
    sr, si = s[0:n], s[n:2 * n]
    hr, hi = hs[0:n], hs[n:2 * n]
    p = jnp.concatenate([sr * hr - si * hi, sr * hi + si * hr], axis=0).astype(BF16)
    y = jnp.dot(mt_ref[...], p, preferred_element_type=F32)
    yr, yi = _twiddle(y[0:n], y[n:2 * n], tc, ts, -1.0)
    o_ref[0, 0] = jnp.concatenate([yr, yi], axis=0).astype(o_ref.dtype)


def hyena_spectrum(a, m_mid, tw):
    _, n, n2x, c = a.shape
    return pl.pallas_call(
        functools.partial(_hy_spec_body, n=n),
        grid=(n, 1),
        in_specs=[pl.BlockSpec((n2x, n2x), lambda k, b: (0, 0)),
                  pl.BlockSpec((1, 2, n, 1), lambda k, b: (k, 0, 0, 0)),
                  pl.BlockSpec((1, 1, n2x, c), lambda k, b: (b, k, 0, 0))],
        out_specs=pl.BlockSpec((1, 1, n2x, c), lambda k, b: (b, k, 0, 0)),
        out_shape=jax.ShapeDtypeStruct((1, n, n2x, c), F32),
        compiler_params=_params(("parallel", "parallel"), 32 << 20),
        name="hyena_filter_spectrum",
    )(m_mid, tw, a)


def hyena_mid(a, hspec, m_mid, m_mid_t, tw):
    bsz, n, n2x, c = a.shape
    return pl.pallas_call(
        functools.partial(_hy_mid_body, n=n),
        grid=(n, bsz),
        in_specs=[pl.BlockSpec((n2x, n2x), lambda k, b: (0, 0)),
                  pl.BlockSpec((n2x, n2x), lambda k, b: (0, 0)),
                  pl.BlockSpec((1, 2, n, 1), lambda k, b: (k, 0, 0, 0)),
                  pl.BlockSpec((1, 1, n2x, c), lambda k, b: (0, k, 0, 0)),
                  pl.BlockSpec((1, 1, n2x, c), lambda k, b: (b, k, 0, 0))],
        out_specs=pl.BlockSpec((1, 1, n2x, c), lambda k, b: (b, k, 0, 0)),
        out_shape=jax.ShapeDtypeStruct((bsz, n, n2x, c), BF16),
        compiler_params=_params(("parallel", "parallel"), 32 << 20),
        name="hyena_dft_mid",
    )(m_mid, m_mid_t, tw, hspec, a)


def _hy_combine_body(y_ref, s_ref, x0_ref, sc_ref, bi_ref, o_ref):
    o_ref[...] = ((y_ref[...] * sc_ref[...] + s_ref[...] * bi_ref[...]) * x0_ref[...]).astype(o_ref.dtype)


def hyena_combine(y, s, x0, scale, bias, tm):
    t, c = y.shape
    row = lambda: pl.BlockSpec((tm, c), lambda i: (i, 0))
    vec = lambda: pl.BlockSpec((1, c), lambda i: (0, 0))
    return pl.pallas_call(
        _hy_combine_body,
        grid=(t // tm,),
        in_specs=[row(), row(), row(), vec(), vec()],
        out_specs=row(),
        out_shape=jax.ShapeDtypeStruct((t, c), BF16),
        compiler_params=_params(("parallel",), 2 * 4 * tm * c * 4 + (4 << 20)),
        name="hyena_combine",
    )(y, s, x0, scale.reshape(1, c), bias.reshape(1, c))


def hyena_bidir(z, col0, short_w, short_b, f_w1, f_b1, f_w2, f_b2, f_w3, f_b3, f_w4, f_freq, bias):
    bsz, seq, _ = z.shape
    c = HYENA_WIDTH
    n = int(round(math.sqrt(2 * seq)))
    assert n * n == 2 * seq and n % 16 == 0
    nh = n // 2
    tb = 8

    w_first, w_last, m_mid, tw = _dft_consts(n, n)
    w_first_x = jnp.asarray(w_first[:, :nh], BF16)
    w_first_h = jnp.asarray(w_first, BF16)
    w_last_y = jnp.asarray(w_last[:nh], BF16)
    m_mid_j = jnp.asarray(m_mid, BF16)
    m_mid_t = jnp.asarray(m_mid.T, BF16)
    tw_j = jnp.asarray(tw, F32)

    h_raw, ss = hyena_filter(seq, f_w1, f_b1, f_w2, f_b2, f_w3, f_b3, f_w4, f_freq, tl=min(seq, 1024))
    scale = lax.rsqrt(ss[0, :c] + ss[0, c:] + EPS) * (1.0 / (2 * seq))
    kern = jnp.concatenate([h_raw[:, :c], jnp.zeros((1, c), F32), jnp.flip(h_raw[1:, c:], axis=0)], axis=0)
    kern_t = jnp.transpose(kern.reshape(1, n, n, c), (0, 2, 1, 3)).astype(BF16)
    ah = hyena_stage(w_first_h, kern_t, tb, BF16)
    ah = jnp.transpose(ah.reshape(1, n, 2, n, c), (0, 3, 2, 1, 4)).reshape(1, n, 2 * n, c)
    hspec = hyena_spectrum(ah, m_mid_j, tw_j)

    s, x0 = hyena_gate(z, short_w, short_b, col0)
    s_t = jnp.transpose(s.reshape(bsz, nh, n, c), (0, 2, 1, 3)).astype(BF16)
    a = hyena_stage(w_first_x, s_t, tb, BF16)
    a = jnp.transpose(a.reshape(bsz, n, 2, n, c), (0, 3, 2, 1, 4)).reshape(bsz, n, 2 * n, c)
    bm = hyena_mid(a, hspec, m_mid_j, m_mid_t, tw_j)
    bm = jnp.transpose(bm.reshape(bsz, n, 2, n, c), (0, 3, 2, 1, 4)).reshape(bsz, n, 2 * n, c)
    y = hyena_stage(w_last_y, bm, tb, F32)
    y = jnp.transpose(y, (0, 2, 1, 3)).reshape(bsz * seq, c)
    return hyena_combine(y, s.reshape(bsz * seq, c), x0.reshape(bsz * seq, c), scale, bias,
                         tm=min(bsz * seq, 2048))


def _tri_inverse(nm):
    c = nm.shape[0]
    eye = (lax.broadcasted_iota(jnp.int32, (c, c), 0) == lax.broadcasted_iota(jnp.int32, (c, c), 1)).astype(F32)
    t = eye - nm
    p = nm
    steps = int(math.log2(c)) - 1
    for _ in range(steps):
        p = _dot3(p, p)
        t = _dot3(t, eye + p)
    return t


def _rwkv_body(*refs, reverse, final):
    if final:
        (z_ref, yin_ref, bin_ref, mu_ref, w0_ref, wup_ref, a0_ref, aup_ref, gup_ref, kk_ref, ka_ref,
         rk_ref, lnw_ref, lnb_ref, sel_ref, o1_ref, carry_ref, st_ref) = refs
    else:
        (z_ref, mu_ref, w0_ref, wup_ref, a0_ref, aup_ref, kk_ref, ka_ref,
         rk_ref, sel_ref, o1_ref, o2_ref, carry_ref, st_ref) = refs
    ch = RWKV_CHUNK
    cw = RWKV_WIDTH
    c_idx = pl.program_id(1)

    @pl.when(c_idx == 0)
    def _():
        carry_ref[...] = jnp.zeros_like(carry_ref)
        st_ref[...] = jnp.zeros_like(st_ref)

    z = z_ref[0]
    zs = z[:, :RWKV_SHIFT_COLS]
    gd = z[:, RWKV_SHIFT_COLS:RWKV_COLS]
    row = lax.broadcasted_iota(jnp.int32, (ch, 1), 0)
    if reverse:
        nb = jnp.broadcast_to(carry_ref[0:1, :], zs.shape)
        shifted = jnp.where(row == ch - 1, nb, pltpu.roll(zs, ch - 1, 0))
        carry_ref[...] = zs[0:8, :]
    else:
        nb = jnp.broadcast_to(carry_ref[7:8, :], zs.shape)
        shifted = jnp.where(row == 0, nb, pltpu.roll(zs, 1, 0))
        carry_ref[...] = zs[ch - 8:ch, :]
    zd = zs + (shifted - zs) * mu_ref[...]

    r = zd[:, 0:cw]
    k = zd[:, cw:2 * cw]
    v = zd[:, 2 * cw:3 * cw]
    lora = zd[:, 3 * cw:3 * cw + DECAY_RANK + ICLR_RANK]
    x = w0_ref[...] + _bdot(jnp.tanh(lora), wup_ref[...])
    log_w = jnp.minimum(x, 0.0) - jnp.log(1.0 + jnp.exp(-jnp.abs(x))) - 0.5
    lnw = -jnp.exp(log_w)
    a = jax.nn.sigmoid(a0_ref[...] + _bdot(lora, aup_ref[...]))
    sel = sel_ref[...]
    kkr = k * kk_ref[...]
    kkn = jnp.sqrt(_bdot(kkr * kkr, sel))
    kk = kkr / jnp.maximum(kkn, 1e-12)
    k2 = k * (1.0 + (a - 1.0) * ka_ref[...])
    ah = kk * a
    bonus = _bdot(r * k2 * rk_ref[...], sel) * v

    ri = lax.broadcasted_iota(jnp.int32, (ch, ch), 0)
    ci = lax.broadcasted_iota(jnp.int32, (ch, ch), 1)
    if reverse:
        incl = ci >= ri
        strict = ci > ri
    else:
        incl = ci <= ri
        strict = ci < ri
    cs = jnp.dot(incl.astype(F32), lnw, preferred_element_type=F32, precision=HIGHEST)
    total = jnp.sum(lnw, axis=0, keepdims=True)
    e_in = jnp.exp(cs)
    e_ex = jnp.exp(cs - lnw)
    e_inv = jnp.exp(-cs)
    e_end = jnp.exp(total - cs)
    rt = r * e_in
    kkt = kk * e_ex
    kh = k2 * e_inv
    ahh = ah * e_inv
    kb = k2 * e_end
    ab = ah * e_end
    g_end = jnp.exp(total)

    lane = lax.broadcasted_iota(jnp.int32, (1, LANES), 1)
    first = lane < RWKV_HEAD_DIM
    blk = ((lax.broadcasted_iota(jnp.int32, (LANES, LANES), 0) < RWKV_HEAD_DIM)
           == (lax.broadcasted_iota(jnp.int32, (LANES, LANES), 1) < RWKV_HEAD_DIM))

    ys = []
    for p in range(cw // LANES):
        sl = slice(p * LANES, (p + 1) * LANES)
        rt_p, kkt_p, kh_p, ahh_p, kb_p, ab_p, v_p = rt[:, sl], kkt[:, sl], kh[:, sl], ahh[:, sl], kb[:, sl], ab[:, sl], v[:, sl]
        lst = jnp.concatenate([kkt_p, rt_p], axis=0)
        wt_h, ut_h, ark_h, ara_h = [], [], [], []
        for hh in range(2):
            keep = first if hh == 0 else jnp.logical_not(first)
            lm = jnp.where(keep, lst, 0.0)
            sk = _bdot_nt(lm, kh_p)
            sa = _bdot_nt(lm, ahh_p)
            a_ak = jnp.where(strict, sk[0:ch], 0.0)
            a_aa = jnp.where(strict, sa[0:ch], 0.0)
            ark_h.append(jnp.where(incl, sk[ch:2 * ch], 0.0))
            ara_h.append(jnp.where(incl, sa[ch:2 * ch], 0.0))
            tinv = _tri_inverse(a_aa)
            akv = _bdot(a_ak, v_p)
            wu = _dot3(tinv, jnp.concatenate([kkt_p, akv], axis=1))
            wt_h.append(wu[:, 0:LANES])
            ut_h.append(wu[:, LANES:2 * LANES])
        wt = jnp.where(first, wt_h[0], wt_h[1])
        ut = jnp.where(first, ut_h[0], ut_h[1])
        st = st_ref[p]
        lm0 = _bdot_nt(jnp.concatenate([wt, rt_p], axis=0), st)
        u = lm0[0:ch] + ut
        y0 = lm0[ch:2 * ch]
        vu = jnp.concatenate([v_p, u], axis=0)
        yh = [_bdot(jnp.concatenate([ark_h[hh], -ara_h[hh]], axis=1), vu) for hh in range(2)]
        ys.append(y0 + jnp.where(first, yh[0], yh[1]))
        upd = _bdot_tn(jnp.concatenate([v_p, -u], axis=0), jnp.concatenate([kb_p, ab_p], axis=0))
        st_ref[p] = st * g_end[:, sl] + jnp.where(blk, upd, 0.0)
    y = jnp.concatenate(ys, axis=1)

    if not final:
        o1_ref[0] = y
        o2_ref[0] = bonus
    else:
        yt = y + yin_ref[0]
        mean = _bdot(yt, sel) * (1.0 / RWKV_HEAD_DIM)
        dv = yt - mean
        var = _bdot(dv * dv, sel) * (1.0 / RWKV_HEAD_DIM)
        yn = dv * lax.rsqrt(var + RWKV_GN_EPS) * lnw_ref[...] + lnb_ref[...]
        g = _bdot(jax.nn.sigmoid(gd), gup_ref[...])
        o1_ref[0] = ((yn + bonus + bin_ref[0]) * g).astype(o1_ref.dtype)


def rwkv7_direction(z, y_in, bonus_in, mu, w0, w_up, a0, a_up, g_up, k_k, k_a, r_k, ln_w, ln_b, reverse, final):
    bsz, seq, _ = z.shape
    ch = RWKV_CHUNK
    cw = RWKV_WIDTH
    nc = seq // ch
    sel_np = (np.arange(cw)[:, None] // RWKV_HEAD_DIM == np.arange(cw)[None, :] // RWKV_HEAD_DIM)
    sel = jnp.asarray(sel_np, BF16)
    zero_pad = jnp.zeros((DECAY_RANK, cw), F32)
    wup = jnp.concatenate([w_up, zero_pad], axis=0).astype(BF16)
    aup = jnp.concatenate([zero_pad, a_up], axis=0).astype(BF16)
    vec = lambda a: a.reshape(1, -1)
    cidx = (lambda c: nc - 1 - c) if reverse else (lambda c: c)
    full = lambda a: pl.BlockSpec(a.shape, lambda b, c: (0,) * a.ndim)
    tok = pl.BlockSpec((1, ch, cw), lambda b, c: (b, cidx(c), 0))
    zspec = pl.BlockSpec((1, ch, RWKV_COLS), lambda b, c: (b, cidx(c), 0))
    if final:
        params = [vec(mu), vec(w0), wup, vec(a0), aup, g_up.astype(BF16), vec(k_k), vec(k_a), vec(r_k),
                  vec(ln_w), vec(ln_b), sel]
        args = [z, y_in, bonus_in] + params
        in_specs = [zspec, tok, tok] + [full(a) for a in params]
        out_shape = [jax.ShapeDtypeStruct((bsz, seq, cw), BF16)]
        out_specs = [tok]
    else:
        params = [vec(mu), vec(w0), wup, vec(a0), aup, vec(k_k), vec(k_a), vec(r_k), sel]
        args = [z] + params
        in_specs = [zspec] + [full(a) for a in params]
        out_shape = [jax.ShapeDtypeStruct((bsz, seq, cw), F32)] * 2
        out_specs = [tok, tok]
    return pl.pallas_call(
        functools.partial(_rwkv_body, reverse=reverse, final=final),
        grid=(bsz, nc),
        in_specs=in_specs,
        out_specs=out_specs,
        out_shape=out_shape,
        scratch_shapes=[pltpu.VMEM((8, RWKV_SHIFT_COLS), F32),
                        pltpu.VMEM((cw // LANES, LANES, LANES), F32)],
        compiler_params=_params(("parallel", "arbitrary"), 40 << 20),
        name="rwkv7_bwd" if reverse else "rwkv7_fwd",
    )(*args)


def rwkv7_bidir(z, mu, w0, w_up, a0, a_up, g_up, k_k, k_a, r_k, ln_w, ln_b):
    bsz, seq, _ = z.shape
    shared = (g_up, k_k, k_a, r_k.reshape(-1), ln_w, ln_b)
    y_f, b_f = rwkv7_direction(z, None, None, mu[0], w0[0], w_up[0], a0[0], a_up[0], *shared,
                               reverse=False, final=False)
    (out,) = rwkv7_direction(z, y_f, b_f, mu[1], w0[1], w_up[1], a0[1], a_up[1], *shared,
                             reverse=True, final=True)
    return out


def kernel(x, p, mix_norm, mix_w_in, rwkv_mu, rwkv_w0, rwkv_w_up, rwkv_a0, rwkv_a_up, rwkv_g_up, rwkv_k_k, rwkv_k_a, rwkv_r_k, rwkv_ln_w, rwkv_ln_b, hy_short_w, hy_short_b, hy_f_w1, hy_f_b1, hy_f_w2, hy_f_b2, hy_f_w3, hy_f_b3, hy_f_w4, hy_f_freq, hy_bias, mix_w_out, na_norm, na_w_qkv, na_q_g, na_k_g, na_rpb, na_w_out, ffn_norm, ffn_w_up, ffn_conv_w, ffn_conv_b, ffn_w_down, ple_norm, ple_w_gate, ple_w_proj):
    bsz, seq, d = x.shape
    depth = p.shape[0]
    t = bsz * seq
    tm = min(1024, seq)
    h = x.reshape(t, d)
    for i in range(depth):
        j = i // 2
        if i % 2 == 0:
            w_in = mix_w_in[j].astype(BF16)
            z = norm_matmul(h, mix_norm[j], w_in, tm=tm, tn=w_in.shape[1] // 2).reshape(bsz, seq, -1)
            y_a = rwkv7_bidir(z, rwkv_mu[j], rwkv_w0[j], rwkv_w_up[j], rwkv_a0[j], rwkv_a_up[j], rwkv_g_up[j],
                              rwkv_k_k[j], rwkv_k_a[j], rwkv_r_k[j], rwkv_ln_w[j], rwkv_ln_b[j])
            y_b = hyena_bidir(z, RWKV_COLS, hy_short_w[j], hy_short_b[j], hy_f_w1[j], hy_f_b1[j], hy_f_w2[j],
                              hy_f_b2[j], hy_f_w3[j], hy_f_b3[j], hy_f_w4[j], hy_f_freq[j], hy_bias[j])
            w_out = mix_w_out[j].astype(BF16)
            h = matmul_residual([y_a.reshape(t, -1), y_b], [w_out[:RWKV_WIDTH], w_out[RWKV_WIDTH:]], h,
                                tm=tm, tn=512)
        else:
            z = norm_matmul(h, na_norm[j], na_w_qkv[j].astype(BF16), tm=tm, tn=1024)
            q, k, v = qk_norm(z, na_q_g[j], na_k_g[j], tm=min(512, seq))
            o = neighbourhood_attention(q.reshape(bsz, seq, d), k.reshape(bsz, seq, d), v.reshape(bsz, seq, d),
                                        na_rpb[j], rb=8)
            h = matmul_residual([o.reshape(t, d)], [na_w_out[j].astype(BF16)], h, tm=tm, tn=512)
        h = conv_ffn_residual(h, ffn_norm[i], ffn_w_up[i].astype(BF16), ffn_conv_w[i], ffn_conv_b[i],
                              ffn_w_down[i].astype(BF16), seq_len=seq, tm=tm, tf=256)
        h = ple_residual(h, p[i].reshape(t, -1), ple_norm[i], ple_w_gate[i].astype(BF16),
                         ple_w_proj[i].astype(BF16), tm=tm, tn=512)
    return h.reshape(bsz, seq, d)
```

```python
import functools
import math

import numpy as np
import jax
import jax.numpy as jnp
from jax import lax
from jax.experimental import pallas as pl
from jax.experimental.pallas import tpu as pltpu

F32 = jnp.float32
BF16 = jnp.bfloat16
HIGHEST = lax.Precision.HIGHEST

EPS = 1e-6
GRID_W = 64
RWKV_HEADS = 8
RWKV_HEAD_DIM = 64
RWKV_WIDTH = RWKV_HEADS * RWKV_HEAD_DIM
DECAY_RANK = 64
ICLR_RANK = 64
GATE_RANK = 128
RWKV_SHIFT_COLS = 3 * RWKV_WIDTH + DECAY_RANK + ICLR_RANK
RWKV_COLS = RWKV_SHIFT_COLS + GATE_RANK
RWKV_GN_EPS = 64e-5
RWKV_CHUNK = 64
HYENA_WIDTH = 512
FILTER_EMB = 17
FILTER_EMB_PAD = 32
DECAY_MIN = math.log(1e-2) / 1.5
DECAY_MAX = math.log(1e-2) / 0.3
NA_HEADS = 16
NA_HEAD_DIM = 64
NA_WIN_ROWS = 8
NA_WIN_COLS = 16
NEG_BIG = -1e30

LANES = 128
SUBLANES = 8
V7X_VMEM_BYTES = 64 * 1024 * 1024
VMEM_CAP = V7X_VMEM_BYTES - 8 * 1024 * 1024


def _params(semantics, vmem_bytes):
    return pltpu.CompilerParams(dimension_semantics=semantics,
                                vmem_limit_bytes=int(min(max(vmem_bytes, 16 * 1024 * 1024), VMEM_CAP)))


def _rms_rows(x, g):
    ms = jnp.mean(x * x, axis=-1, keepdims=True)
    return x * lax.rsqrt(ms + EPS) * g


def _bdot(a, b):
    return jnp.dot(a.astype(BF16), b.astype(BF16), preferred_element_type=F32)


def _bdot_nt(a, b):
    return lax.dot_general(a.astype(BF16), b.astype(BF16), (((1,), (1,)), ((), ())),
                           preferred_element_type=F32)


def _bdot_tn(a, b):
    return lax.dot_general(a.astype(BF16), b.astype(BF16), (((0,), (0,)), ((), ())),
                           preferred_element_type=F32)


def _norm_matmul_body(x_ref, g_ref, w_ref, o_ref, xn_ref):
    @pl.when(pl.program_id(1) == 0)
    def _():
        xn_ref[...] = _rms_rows(x_ref[...], g_ref[...]).astype(BF16)

    o_ref[...] = jnp.dot(xn_ref[...], w_ref[...], preferred_element_type=F32).astype(o_ref.dtype)


def norm_matmul(x, g, w, tm, tn, out_dtype=F32):
    t, d = x.shape
    n = w.shape[1]
    osz = jnp.dtype(out_dtype).itemsize
    vmem = 2 * (tm * d * 4 + d * tn * 2 + tm * tn * osz) + tm * d * 2 + (4 << 20)
    return pl.pallas_call(
        _norm_matmul_body,
        grid=(t // tm, n // tn),
        in_specs=[pl.BlockSpec((tm, d), lambda i, j: (i, 0)),
                  pl.BlockSpec((1, d), lambda i, j: (0, 0)),
                  pl.BlockSpec((d, tn), lambda i, j: (0, j))],
        out_specs=pl.BlockSpec((tm, tn), lambda i, j: (i, j)),
        out_shape=jax.ShapeDtypeStruct((t, n), out_dtype),
        scratch_shapes=[pltpu.VMEM((tm, d), BF16)],
        compiler_params=_params(("parallel", "arbitrary"), vmem),
        name="norm_matmul",
    )(x, g.reshape(1, d), w)


def _matmul_residual_body(*refs, n_in):
    xs = refs[:n_in]
    ws = refs[n_in:2 * n_in]
    h_ref = refs[2 * n_in]
    o_ref = refs[2 * n_in + 1]
    acc = h_ref[...]
    for x_ref, w_ref in zip(xs, ws):
        acc = acc + jnp.dot(x_ref[...], w_ref[...], preferred_element_type=F32)
    o_ref[...] = acc


def matmul_residual(xs, ws, h, tm, tn):
    t, n = h.shape
    n_in = len(xs)
    in_specs = ([pl.BlockSpec((tm, x.shape[1]), lambda i, j: (i, 0)) for x in xs]
                + [pl.BlockSpec((w.shape[0], tn), lambda i, j: (0, j)) for w in ws]
                + [pl.BlockSpec((tm, tn), lambda i, j: (i, j))])
    ksum = sum(x.shape[1] for x in xs)
    vmem = 2 * (tm * ksum * 2 + ksum * tn * 2 + 2 * tm * tn * 4) + (4 << 20)
    return pl.pallas_call(
        functools.partial(_matmul_residual_body, n_in=n_in),
        grid=(t // tm, n // tn),
        in_specs=in_specs,
        out_specs=pl.BlockSpec((tm, tn), lambda i, j: (i, j)),
        out_shape=jax.ShapeDtypeStruct((t, n), F32),
        compiler_params=_params(("parallel", "parallel"), vmem),
        name="matmul_residual",
    )(*xs, *ws, h)


FFN_HALO = 16


def _ffn_body(xm_ref, xp_ref, xx_ref, g_ref, wa_ref, wb_ref, cwa_ref, cwb_ref, cba_ref, cbb_ref,
              wd_ref, o_ref, xn_ref, *, tm, tiles_per_seq):
    i = pl.program_id(0)
    j = pl.program_id(1)
    hl = FFN_HALO

    @pl.when(j == 0)
    def _():
        g = g_ref[...]
        first = (i % tiles_per_seq) == 0
        last = (i % tiles_per_seq) == tiles_per_seq - 1
        xm = xm_ref[...]
        xn_ref[hl:hl + tm, :] = _rms_rows(xm, g).astype(BF16)
        xn_ref[0:hl, :] = jnp.where(first, 0.0, _rms_rows(xp_ref[...], g)).astype(BF16)
        xn_ref[hl + tm:2 * hl + tm, :] = jnp.where(last, 0.0, _rms_rows(xx_ref[...], g)).astype(BF16)
        o_ref[...] = xm

    xn = xn_ref[...]
    rows = tm + 2 * hl

    def conv(z, cw_ref, cb_ref):
        cw = cw_ref[...]
        zp = pltpu.roll(z, 1, 0)
        zn = pltpu.roll(z, rows - 1, 0)
        u = zp * cw[0:1, :] + z * cw[1:2, :] + zn * cw[2:3, :] + cb_ref[...]
        return u[hl:hl + tm, :]

    ua = conv(jnp.dot(xn, wa_ref[...], preferred_element_type=F32), cwa_ref, cba_ref)
    ub = conv(jnp.dot(xn, wb_ref[...], preferred_element_type=F32), cwb_ref, cbb_ref)
    act = (jax.nn.gelu(ua) * ub).astype(BF16)
    o_ref[...] += jnp.dot(act, wd_ref[...], preferred_element_type=F32)


def conv_ffn_residual(h, norm_g, w_up, conv_w, conv_b, w_down, seq_len, tm, tf):
    t, d = h.shape
    f = w_down.shape[0]
    nf = f // tf
    hl = FFN_HALO
    tiles_per_seq = seq_len // tm
    nhb = t // hl
    body = functools.partial(_ffn_body, tm=tm, tiles_per_seq=tiles_per_seq)
    vmem = (2 * (tm * d * 4 + 2 * hl * d * 4 + 2 * d * tf * 2 + tf * d * 2 + tm * d * 4)
            + (tm + 2 * hl) * d * 2 + 10 * (tm + 2 * hl) * tf * 4 + (4 << 20))
    return pl.pallas_call(
        body,
        grid=(t // tm, nf),
        in_specs=[
            pl.BlockSpec((tm, d), lambda i, j: (i, 0)),
            pl.BlockSpec((hl, d), lambda i, j: (jnp.maximum(i * (tm // hl) - 1, 0), 0)),
            pl.BlockSpec((hl, d), lambda i, j: (jnp.minimum((i + 1) * (tm // hl), nhb - 1), 0)),
            pl.BlockSpec((1, d), lambda i, j: (0, 0)),
            pl.BlockSpec((d, tf), lambda i, j: (0, j)),
            pl.BlockSpec((d, tf), lambda i, j: (0, nf + j)),
            pl.BlockSpec((3, tf), lambda i, j: (0, j)),
            pl.BlockSpec((3, tf), lambda i, j: (0, nf + j)),
            pl.BlockSpec((1, tf), lambda i, j: (0, j)),
            pl.BlockSpec((1, tf), lambda i, j: (0, nf + j)),
            pl.BlockSpec((tf, d), lambda i, j: (j, 0)),
        ],
        out_specs=pl.BlockSpec((tm, d), lambda i, j: (i, 0)),
        out_shape=jax.ShapeDtypeStruct((t, d), F32),
        scratch_shapes=[pltpu.VMEM((tm + 2 * hl, d), BF16)],
        compiler_params=_params(("parallel", "arbitrary"), vmem),
        name="conv_ffn",
    )(h, h, h, norm_g.reshape(1, d), w_up, w_up, conv_w, conv_w,
      conv_b.reshape(1, 2 * f), conv_b.reshape(1, 2 * f), w_down)


def _ple_body(h_ref, hc_ref, g_ref, wg_ref, p_ref, wp_ref, o_ref, xn_ref):
    @pl.when(pl.program_id(1) == 0)
    def _():
        xn_ref[...] = _rms_rows(h_ref[...], g_ref[...]).astype(BF16)

    gate = jax.nn.sigmoid(jnp.dot(xn_ref[...], wg_ref[...], preferred_element_type=F32))
    proj = jnp.dot(p_ref[...].astype(BF16), wp_ref[...], preferred_element_type=F32)
    o_ref[...] = hc_ref[...] + gate * proj


def ple_residual(h, p, norm_g, w_gate, w_proj, tm, tn):
    t, d = h.shape
    pd = p.shape[1]
    vmem = 2 * (tm * d * 4 + 2 * tm * tn * 4 + d * tn * 2 + tm * pd * 4 + pd * tn * 2) + tm * d * 2 + (4 << 20)
    return pl.pallas_call(
        _ple_body,
        grid=(t // tm, d // tn),
        in_specs=[pl.BlockSpec((tm, d), lambda i, j: (i, 0)),
                  pl.BlockSpec((tm, tn), lambda i, j: (i, j)),
                  pl.BlockSpec((1, d), lambda i, j: (0, 0)),
                  pl.BlockSpec((d, tn), lambda i, j: (0, j)),
                  pl.BlockSpec((tm, pd), lambda i, j: (i, 0)),
                  pl.BlockSpec((pd, tn), lambda i, j: (0, j))],
        out_specs=pl.BlockSpec((tm, tn), lambda i, j: (i, j)),
        out_shape=jax.ShapeDtypeStruct((t, d), F32),
        scratch_shapes=[pltpu.VMEM((tm, d), BF16)],
        compiler_params=_params(("parallel", "arbitrary"), vmem),
        name="ple",
    )(h, h, norm_g.reshape(1, d), w_gate, p, w_proj)


def _qknorm_body(z_ref, sel_ref, selt_ref, qg_ref, kg_ref, q_ref, k_ref, v_ref, *, d):
    sel = sel_ref[...]
    selt = selt_ref[...]

    def head_rms(x, g):
        ss = _bdot(x * x, sel)
        inv = lax.rsqrt(ss * (1.0 / NA_HEAD_DIM) + EPS)
        invb = jnp.dot(inv, selt, preferred_element_type=F32, precision=HIGHEST)
        return x * invb * g

    q_ref[...] = (head_rms(z_ref[:, 0:d], qg_ref[...]) * (NA_HEAD_DIM ** -0.5)).astype(BF16)
    k_ref[...] = head_rms(z_ref[:, d:2 * d], kg_ref[...]).astype(BF16)
    v_ref[...] = z_ref[:, 2 * d:3 * d].astype(BF16)


def qk_norm(z, q_g, k_g, tm):
    t, d3 = z.shape
    d = d3 // 3
    heads = d // NA_HEAD_DIM
    sel_np = np.zeros((d, LANES), np.float32)
    sel_np[np.arange(d), np.arange(d) // NA_HEAD_DIM] = 1.0
    sel = jnp.asarray(sel_np, BF16)
    selt = jnp.asarray(sel_np.T, F32)
    qg = jnp.tile(q_g.reshape(1, NA_HEAD_DIM), (1, heads))
    kg = jnp.tile(k_g.reshape(1, NA_HEAD_DIM), (1, heads))
    out = jax.ShapeDtypeStruct((t, d), BF16)
    vmem = 2 * (tm * d3 * 4 + 3 * tm * d * 2) + 8 * tm * d * 4 + (4 << 20)
    return pl.pallas_call(
        functools.partial(_qknorm_body, d=d),
        grid=(t // tm,),
        in_specs=[pl.BlockSpec((tm, d3), lambda i: (i, 0)),
                  pl.BlockSpec((d, LANES), lambda i: (0, 0)),
                  pl.BlockSpec((LANES, d), lambda i: (0, 0)),
                  pl.BlockSpec((1, d), lambda i: (0, 0)),
                  pl.BlockSpec((1, d), lambda i: (0, 0))],
        out_specs=[pl.BlockSpec((tm, d), lambda i: (i, 0))] * 3,
        out_shape=[out, out, out],
        compiler_params=_params(("parallel",), vmem),
        name="qk_norm",
    )(z, sel, selt, qg, kg)


def _natten_body(q_ref, k_ref, v_ref, b_ref, o_ref, *, rows, rb):
    ib = pl.program_id(2)
    lane = lax.broadcasted_iota(jnp.int32, (GRID_W, LANES), 1)
    first = lane < NA_HEAD_DIM
    kwin = NA_WIN_ROWS * GRID_W

    def one_row(r, carry):
        i = ib * rb + r
        rs = jnp.clip(i - NA_WIN_ROWS // 2, 0, rows - NA_WIN_ROWS)
        var = i - rs
        q = q_ref[0, pl.ds(pl.multiple_of(r * GRID_W, GRID_W), GRID_W), :]
        kstart = pl.multiple_of(rs * GRID_W, GRID_W)
        k = k_ref[0, pl.ds(kstart, kwin), :]
        v = v_ref[0, pl.ds(kstart, kwin), :]
        outs = []
        for a in range(2):
            keep = first if a == 0 else jnp.logical_not(first)
            qa = jnp.where(keep, q, jnp.zeros_like(q))
            s = lax.dot_general(qa, k, (((1,), (1,)), ((), ())), preferred_element_type=F32)
            s = s + b_ref[a, var]
            m = jnp.max(s, axis=-1, keepdims=True)
            p = jnp.exp(s - m)
            l = jnp.sum(p, axis=-1, keepdims=True)
            o = jnp.dot(p.astype(BF16), v, preferred_element_type=F32)
            outs.append(o / l)
        o_ref[0, pl.ds(pl.multiple_of(r * GRID_W, GRID_W), GRID_W), :] = (
            jnp.where(first, outs[0], outs[1]).astype(o_ref.dtype))
        return carry

    lax.fori_loop(0, rb, one_row, 0)


def _natten_bias(rpb):
    kh, kw, w = NA_WIN_ROWS, NA_WIN_COLS, GRID_W
    var = np.arange(kh)
    jr = np.arange(kh)
    dr = jr[None, :] + (kh - 1) - var[:, None]
    sel_r = (dr[:, :, None] == np.arange(2 * kh - 1)[None, None, :]).astype(np.float32)
    col = np.arange(w)
    cstart = np.clip(col - kw // 2, 0, w - kw)
    dc = col[None, :] - col[:, None] + (kw - 1)
    valid = (col[None, :] >= cstart[:, None]) & (col[None, :] < cstart[:, None] + kw)
    sel_c = ((dc[:, :, None] == np.arange(2 * kw - 1)[None, None, :]) & valid[:, :, None]).astype(np.float32)
    b = jnp.einsum('hrd,vjr,cgd->hvcjg', rpb, jnp.asarray(sel_r), jnp.asarray(sel_c), precision=HIGHEST)
    mask = np.where(valid, 0.0, NEG_BIG).astype(np.float32)[None, None, :, None, :]
    b = b + jnp.asarray(mask)
    return b.reshape(rpb.shape[0], kh, w, kh * w)


def neighbourhood_attention(q, k, v, rpb, rb):
    bsz, seq, d = q.shape
    rows = seq // GRID_W
    bias = _natten_bias(rpb)
    hg = d // LANES
    kwin = NA_WIN_ROWS * GRID_W
    vmem = (2 * (2 * seq * LANES * 2 + 2 * rb * GRID_W * LANES * 2 + 2 * NA_WIN_ROWS * GRID_W * kwin * 4)
            + 16 * GRID_W * kwin * 4 + (4 << 20))
    return pl.pallas_call(
        functools.partial(_natten_body, rows=rows, rb=rb),
        grid=(bsz, hg, rows // rb),
        in_specs=[pl.BlockSpec((1, rb * GRID_W, LANES), lambda b, h, i: (b, i, h)),
                  pl.BlockSpec((1, seq, LANES), lambda b, h, i: (b, 0, h)),
                  pl.BlockSpec((1, seq, LANES), lambda b, h, i: (b, 0, h)),
                  pl.BlockSpec((2, NA_WIN_ROWS, GRID_W, kwin), lambda b, h, i: (h, 0, 0, 0))],
        out_specs=pl.BlockSpec((1, rb * GRID_W, LANES), lambda b, h, i: (b, i, h)),
        out_shape=jax.ShapeDtypeStruct((bsz, seq, d), BF16),
        compiler_params=_params(("parallel", "parallel", "arbitrary"), vmem),
        name="natten",
    )(q, k, v, bias)


def _hy_filter_body(f_ref, w1_ref, b1_ref, w2_ref, b2_ref, w3_ref, b3_ref, w4_ref, fr_ref, dl_ref,
                    h_ref, ss_ref):
    d = functools.partial(jnp.dot, preferred_element_type=F32, precision=HIGHEST)
    f = f_ref[...]
    fr = fr_ref[...]
    x = jnp.sin(fr * (d(f, w1_ref[...]) + b1_ref[...]))
    x = jnp.sin(fr * (d(x, w2_ref[...]) + b2_ref[...]))
    x = jnp.sin(fr * (d(x, w3_ref[...]) + b3_ref[...]))
    h = d(x, w4_ref[...]) * jnp.exp(-f[:, 0:1] * dl_ref[...])
    h_ref[...] = h

    @pl.when(pl.program_id(0) == 0)
    def _():
        ss_ref[...] = jnp.zeros_like(ss_ref)

    ss_ref[...] += jnp.sum(h * h, axis=0, keepdims=True)


def hyena_filter(seq, w1, b1, w2, b2, w3, b3, w4, freq, tl):
    c2 = w4.shape[1]
    hid = w1.shape[1]
    t = np.linspace(0.0, 1.0, seq, dtype=np.float32)[:, None]
    bands = (FILTER_EMB - 1) // 2
    ang = (np.float32(2.0 * math.pi / seq) * np.arange(seq, dtype=np.float32)[:, None]
           * np.linspace(1e-4, bands - 1, bands, dtype=np.float32)[None])
    feats = np.concatenate([t, np.cos(ang), -np.sin(ang),
                            np.zeros((seq, FILTER_EMB_PAD - FILTER_EMB), np.float32)], axis=-1)
    w1p = jnp.concatenate([w1, jnp.zeros((FILTER_EMB_PAD - FILTER_EMB, hid), F32)], axis=0)
    deltas = np.abs(np.linspace(DECAY_MIN, DECAY_MAX, c2 // 2, dtype=np.float32))
    deltas = np.tile(deltas, 2)[None]
    full = lambda a: pl.BlockSpec(a.shape, lambda i: (0,) * a.ndim)
    args = [jnp.asarray(feats), w1p, b1.reshape(1, hid), w2, b2.reshape(1, hid), w3, b3.reshape(1, hid),
            w4, freq.reshape(1, hid), jnp.asarray(deltas)]
    in_specs = [pl.BlockSpec((tl, FILTER_EMB_PAD), lambda i: (i, 0))] + [full(a) for a in args[1:]]
    return pl.pallas_call(
        _hy_filter_body,
        grid=(seq // tl,),
        in_specs=in_specs,
        out_specs=[pl.BlockSpec((tl, c2), lambda i: (i, 0)), pl.BlockSpec((1, c2), lambda i: (0, 0))],
        out_shape=[jax.ShapeDtypeStruct((seq, c2), F32), jax.ShapeDtypeStruct((1, c2), F32)],
        compiler_params=_params(("arbitrary",), 32 << 20),
        name="hyena_filter",
    )(*args)


def _hy_gate_body(z0_ref, z1_ref, zv_ref, w0_ref, w1_ref, wv_ref, b0_ref, b1_ref, bv_ref, s_ref, x0_ref,
                  *, seq):
    row = lax.broadcasted_iota(jnp.int32, (seq, LANES), 0)
    top = row == 0
    bot = row == seq - 1

    def conv(z_ref, w_ref, b_ref):
        z = z_ref[0]
        w = w_ref[...]
        zp = jnp.where(top, 0.0, pltpu.roll(z, 1, 0))
        zn = jnp.where(bot, 0.0, pltpu.roll(z, seq - 1, 0))
        return zp * w[0:1, :] + z * w[1:2, :] + zn * w[2:3, :] + b_ref[...]

    x0_ref[0] = conv(z0_ref, w0_ref, b0_ref)
    s_ref[0] = conv(zv_ref, wv_ref, bv_ref) * conv(z1_ref, w1_ref, b1_ref)


def hyena_gate(z, short_w, short_b, col0):
    bsz, seq, _ = z.shape
    c = HYENA_WIDTH
    nb = c // LANES
    o0 = col0 // LANES
    sb = short_b.reshape(1, 3 * c)
    zspec = lambda off: pl.BlockSpec((1, seq, LANES), lambda b, j: (b, 0, off + j))
    wspec = lambda off: pl.BlockSpec((3, LANES), lambda b, j: (0, off + j))
    bspec = lambda off: pl.BlockSpec((1, LANES), lambda b, j: (0, off + j))
    out = jax.ShapeDtypeStruct((bsz, seq, c), F32)
    return pl.pallas_call(
        functools.partial(_hy_gate_body, seq=seq),
        grid=(bsz, nb),
        in_specs=[zspec(o0), zspec(o0 + nb), zspec(o0 + 2 * nb),
                  wspec(0), wspec(nb), wspec(2 * nb), bspec(0), bspec(nb), bspec(2 * nb)],
        out_specs=[pl.BlockSpec((1, seq, LANES), lambda b, j: (b, 0, j))] * 2,
        out_shape=[out, out],
        compiler_params=_params(("parallel", "parallel"), 2 * 5 * seq * LANES * 4 + 8 * seq * LANES * 4 + (4 << 20)),
        name="hyena_gate",
    )(z, z, z, short_w, short_w, short_w, sb, sb, sb)


def _dft_consts(n, n_in):
    k = np.arange(n)[:, None].astype(np.float64)
    t = np.arange(n_in)[None, :].astype(np.float64)
    th = 2.0 * np.pi * k * t / n
    w_first = np.concatenate([np.cos(th), -np.sin(th)], axis=0)
    w_last = np.concatenate([np.cos(th).T, -np.sin(th).T], axis=1)
    tt = np.arange(n)[None, :].astype(np.float64)
    th2 = 2.0 * np.pi * k * tt / n
    cm, sm = np.cos(th2), np.sin(th2)
    m_mid = np.block([[cm, sm], [-sm, cm]])
    ph = 2.0 * np.pi * k * tt / (n * n)
    tw = np.stack([np.cos(ph), np.sin(ph)], axis=1)[..., None]
    return w_first, w_last, m_mid, tw


def _hy_stage1_body(w_ref, x_ref, o_ref, *, tb):
    w = w_ref[...]
    for j in range(tb):
        o_ref[0, j] = jnp.dot(w, x_ref[0, j], preferred_element_type=F32).astype(o_ref.dtype)


def hyena_stage(w, x, tb, out_dtype):
    bsz, n2, kk, c = x.shape
    m = w.shape[0]
    osz = jnp.dtype(out_dtype).itemsize
    vmem = 2 * (tb * kk * c * 2 + tb * m * c * osz + m * kk * 2) + 2 * m * c * 4 + (4 << 20)
    return pl.pallas_call(
        functools.partial(_hy_stage1_body, tb=tb),
        grid=(bsz, n2 // tb),
        in_specs=[pl.BlockSpec((m, kk), lambda b, j: (0, 0)),
                  pl.BlockSpec((1, tb, kk, c), lambda b, j: (b, j, 0, 0))],
        out_specs=pl.BlockSpec((1, tb, m, c), lambda b, j: (b, j, 0, 0)),
        out_shape=jax.ShapeDtypeStruct((bsz, n2, m, c), out_dtype),
        compiler_params=_params(("parallel", "parallel"), vmem),
        name="hyena_dft_outer",
    )(w, x)


def _twiddle(xr, xi, tc, ts, sign):
    return xr * tc + sign * (xi * ts), xi * tc - sign * (xr * ts)


def _hy_spec_body(m_ref, tw_ref, a_ref, o_ref, *, n):
    a = a_ref[0, 0].astype(F32)
    tc = tw_ref[0, 0]
    ts = tw_ref[0, 1]
    xr, xi = _twiddle(a[0:n], a[n:2 * n], tc, ts, 1.0)
    x = jnp.concatenate([xr, xi], axis=0).astype(BF16)
    o_ref[0, 0] = jnp.dot(m_ref[...], x, preferred_element_type=F32)


def _hy_mid_body(m_ref, mt_ref, tw_ref, hs_ref, a_ref, o_ref, *, n):
    a = a_ref[0, 0].astype(F32)
    tc = tw_ref[0, 0]
    ts = tw_ref[0, 1]
    xr, xi = _twiddle(a[0:n], a[n:2 * n], tc, ts, 1.0)
    x = jnp.concatenate([xr, xi], axis=0).astype(BF16)
    s = jnp.dot(m_ref[...], x, preferred_element_type=F32)
    hs = hs_ref[0, 0]
    sr, si = s[0:n], s[n:2 * n]
    hr, hi = hs[0:n], hs[n:2 * n]
    p = jnp.concatenate([sr * hr - si * hi, sr * hi + si * hr], axis=0).astype(BF16)
    y = jnp.dot(mt_ref[...], p, preferred_element_type=F32)
    yr, yi = _twiddle(y[0:n], y[n:2 * n], tc, ts, -1.0)
    o_ref[0, 0] = jnp.concatenate([yr, yi], axis=0).astype(o_ref.dtype)


def hyena_spectrum(a, m_mid, tw):
    _, n, n2x, c = a.shape
    return pl.pallas_call(
        functools.partial(_hy_spec_body, n=n),
        grid=(n, 1),
        in_specs=[pl.BlockSpec((n2x, n2x), lambda k, b: (0, 0)),
                  pl.BlockSpec((1, 2, n, 1), lambda k, b: (k, 0, 0, 0)),
                  pl.BlockSpec((1, 1, n2x, c), lambda k, b: (b, k, 0, 0))],
        out_specs=pl.BlockSpec((1, 1, n2x, c), lambda k, b: (b, k, 0, 0)),
        out_shape=jax.ShapeDtypeStruct((1, n, n2x, c), F32),
        compiler_params=_params(("parallel", "parallel"), 32 << 20),
        name="hyena_filter_spectrum",
    )(m_mid, tw, a)


def hyena_mid(a, hspec, m_mid, m_mid_t, tw):
    bsz, n, n2x, c = a.shape
    return pl.pallas_call(
        functools.partial(_hy_mid_body, n=n),
        grid=(n, bsz),
        in_specs=[pl.BlockSpec((n2x, n2x), lambda k, b: (0, 0)),
                  pl.BlockSpec((n2x, n2x), lambda k, b: (0, 0)),
                  pl.BlockSpec((1, 2, n, 1), lambda k, b: (k, 0, 0, 0)),
                  pl.BlockSpec((1, 1, n2x, c), lambda k, b: (0, k, 0, 0)),
                  pl.BlockSpec((1, 1, n2x, c), lambda k, b: (b, k, 0, 0))],
        out_specs=pl.BlockSpec((1, 1, n2x, c), lambda k, b: (b, k, 0, 0)),
        out_shape=jax.ShapeDtypeStruct((bsz, n, n2x, c), BF16),
        compiler_params=_params(("parallel", "parallel"), 32 << 20),
        name="hyena_dft_mid",
    )(m_mid, m_mid_t, tw, hspec, a)


def _hy_combine_body(y_ref, s_ref, x0_ref, sc_ref, bi_ref, o_ref):
    o_ref[...] = ((y_ref[...] * sc_ref[...] + s_ref[...] * bi_ref[...]) * x0_ref[...]).astype(o_ref.dtype)


def hyena_combine(y, s, x0, scale, bias, tm):
    t, c = y.shape
    row = lambda: pl.BlockSpec((tm, c), lambda i: (i, 0))
    vec = lambda: pl.BlockSpec((1, c), lambda i: (0, 0))
    return pl.pallas_call(
        _hy_combine_body,
        grid=(t // tm,),
        in_specs=[row(), row(), row(), vec(), vec()],
        out_specs=row(),
        out_shape=jax.ShapeDtypeStruct((t, c), BF16),
        compiler_params=_params(("parallel",), 2 * 4 * tm * c * 4 + (4 << 20)),
        name="hyena_combine",
    )(y, s, x0, scale.reshape(1, c), bias.reshape(1, c))


def hyena_bidir(z, col0, short_w, short_b, f_w1, f_b1, f_w2, f_b2, f_w3, f_b3, f_w4, f_freq, bias):
    bsz, seq, _ = z.shape
    c = HYENA_WIDTH
    n = int(round(math.sqrt(2 * seq)))
    assert n * n == 2 * seq and n % 16 == 0
    nh = n // 2
    tb = 8

    w_first, w_last, m_mid, tw = _dft_consts(n, n)
    w_first_x = jnp.asarray(w_first[:, :nh], BF16)
    w_first_h = jnp.asarray(w_first, BF16)
    w_last_y = jnp.asarray(w_last[:nh], BF16)
    m_mid_j = jnp.asarray(m_mid, BF16)
    m_mid_t = jnp.asarray(m_mid.T, BF16)
    tw_j = jnp.asarray(tw, F32)

    h_raw, ss = hyena_filter(seq, f_w1, f_b1, f_w2, f_b2, f_w3, f_b3, f_w4, f_freq, tl=min(seq, 1024))
    scale = lax.rsqrt(ss[0, :c] + ss[0, c:] + EPS) * (1.0 / (2 * seq))
    kern = jnp.concatenate([h_raw[:, :c], jnp.zeros((1, c), F32), jnp.flip(h_raw[1:, c:], axis=0)], axis=0)
    kern_t = jnp.transpose(kern.reshape(1, n, n, c), (0, 2, 1, 3)).astype(BF16)
    ah = hyena_stage(w_first_h, kern_t, tb, BF16)
    ah = jnp.transpose(ah.reshape(1, n, 2, n, c), (0, 3, 2, 1, 4)).reshape(1, n, 2 * n, c)
    hspec = hyena_spectrum(ah, m_mid_j, tw_j)

    s, x0 = hyena_gate(z, short_w, short_b, col0)
    s_t = jnp.transpose(s.reshape(bsz, nh, n, c), (0, 2, 1, 3)).astype(BF16)
    a = hyena_stage(w_first_x, s_t, tb, BF16)
    a = jnp.transpose(a.reshape(bsz, n, 2, n, c), (0, 3, 2, 1, 4)).reshape(bsz, n, 2 * n, c)
    bm = hyena_mid(a, hspec, m_mid_j, m_mid_t, tw_j)
    bm = jnp.transpose(bm.reshape(bsz, n, 2, n, c), (0, 3, 2, 1, 4)).reshape(bsz, n, 2 * n, c)
    y = hyena_stage(w_last_y, bm, tb, F32)
    y = jnp.transpose(y, (0, 2, 1, 3)).reshape(bsz * seq, c)
    return hyena_combine(y, s.reshape(bsz * seq, c), x0.reshape(bsz * seq, c), scale, bias,
                         tm=min(bsz * seq, 2048))


RWKV_PAIRS = RWKV_WIDTH // LANES


def _pair_rows(x, first):
    return jnp.concatenate([jnp.where(first, x, 0.0), jnp.where(first, 0.0, x)], axis=0)


def _fold_rows(x2, ch):
    return x2[0:ch] + x2[ch:2 * ch]


def _rwkv_prep_body(z_ref, halo_ref, mu_ref, w0_ref, wup_ref, a0_ref, aup_ref, kk_ref, ka_ref, rk_ref, sel_ref,
                    wt_ref, rt_ref, kb_ref, ab_ref, v_ref, ut_ref, yk_ref, ara_ref, ge_ref, bonus_ref,
                    *, reverse, nc):
    ch = RWKV_CHUNK
    cw = RWKV_WIDTH
    c_idx = pl.program_id(1)

    zs = z_ref[0]
    row = lax.broadcasted_iota(jnp.int32, (ch, 1), 0)
    if reverse:
        nbr = jnp.where(c_idx == nc - 1, 0.0, halo_ref[0, 0:1, :])
        shifted = jnp.where(row == ch - 1, jnp.broadcast_to(nbr, zs.shape), pltpu.roll(zs, ch - 1, 0))
    else:
        nbr = jnp.where(c_idx == 0, 0.0, halo_ref[0, SUBLANES - 1:SUBLANES, :])
        shifted = jnp.where(row == 0, jnp.broadcast_to(nbr, zs.shape), pltpu.roll(zs, 1, 0))
    zd = zs + (shifted - zs) * mu_ref[...]

    r = zd[:, 0:cw]
    k = zd[:, cw:2 * cw]
    v = zd[:, 2 * cw:3 * cw]
    lora = zd[:, 3 * cw:3 * cw + DECAY_RANK + ICLR_RANK]
    x = w0_ref[...] + _bdot(jnp.tanh(lora), wup_ref[...])
    log_w = jnp.minimum(x, 0.0) - jnp.log(1.0 + jnp.exp(-jnp.abs(x))) - 0.5
    lnw = -jnp.exp(log_w)
    a = jax.nn.sigmoid(a0_ref[...] + _bdot(lora, aup_ref[...]))
    sel = sel_ref[...]
    kkr = k * kk_ref[...]
    kkn = jnp.sqrt(_bdot(kkr * kkr, sel))
    kk = kkr / jnp.maximum(kkn, 1e-12)
    k2 = k * (1.0 + (a - 1.0) * ka_ref[...])
    ah = kk * a
    bonus_ref[0] = _bdot(r * k2 * rk_ref[...], sel) * v

    ri = lax.broadcasted_iota(jnp.int32, (ch, ch), 0)
    ci = lax.broadcasted_iota(jnp.int32, (ch, ch), 1)
    tri = (ci >= ri) if reverse else (ci <= ri)
    cs = jnp.dot(tri.astype(F32), lnw, preferred_element_type=F32, precision=HIGHEST)
    total = jnp.sum(lnw, axis=0, keepdims=True)
    e_inv = jnp.exp(-cs)
    e_end = jnp.exp(total - cs)
    rt = r * jnp.exp(cs)
    kkt = kk * jnp.exp(cs - lnw)
    kh = k2 * e_inv
    ahh = ah * e_inv
    rt_ref[0] = rt.astype(rt_ref.dtype)
    kb_ref[0] = (k2 * e_end).astype(kb_ref.dtype)
    ab_ref[0] = (ah * e_end).astype(ab_ref.dtype)
    v_ref[0] = v.astype(v_ref.dtype)
    ge_ref[0, 0] = jnp.exp(total)

    r2 = lax.broadcasted_iota(jnp.int32, (2 * ch, 2 * ch), 0)
    c2 = lax.broadcasted_iota(jnp.int32, (2 * ch, 2 * ch), 1)
    same_head = (r2 < ch) == (c2 < ch)
    si = r2 & (ch - 1)
    sj = c2 & (ch - 1)
    before = (sj > si) if reverse else (sj < si)
    strict = same_head & before
    incl = same_head & jnp.logical_or(before, si == sj)
    eye = (r2 == c2).astype(F32)
    first = lax.broadcasted_iota(jnp.int32, (1, LANES), 1) < RWKV_HEAD_DIM

    pairs = range(RWKV_PAIRS)
    sl = [slice(p * LANES, (p + 1) * LANES) for p in pairs]
    l2 = [_pair_rows(kkt[:, sl[p]], first) for p in pairs]
    l4 = [jnp.concatenate([l2[p], _pair_rows(rt[:, sl[p]], first)], axis=0) for p in pairs]
    v2 = [_pair_rows(v[:, sl[p]], first) for p in pairs]
    sk = [_bdot_nt(l4[p], jnp.concatenate([kh[:, sl[p]]] * 2, axis=0)) for p in pairs]
    sa = [_bdot_nt(l4[p], jnp.concatenate([ahh[:, sl[p]]] * 2, axis=0)) for p in pairs]
    nm = [jnp.where(strict, sa[p][0:2 * ch], 0.0) for p in pairs]
    for p in pairs:
        ara_ref[0, 0, p] = jnp.where(incl, sa[p][2 * ch:4 * ch], 0.0).astype(ara_ref.dtype)
    akr = [jnp.concatenate([jnp.where(strict, sk[p][0:2 * ch], 0.0),
                            jnp.where(incl, sk[p][2 * ch:4 * ch], 0.0)], axis=0) for p in pairs]
    av = [_bdot(akr[p], v2[p]) for p in pairs]

    lvl = (si >> 1) == (sj >> 1)
    t = [eye - jnp.where(lvl, nm[p], 0.0) for p in pairs]
    s = 2
    while s < ch:
        sh = s.bit_length() - 1
        lvl = ((si >> (sh + 1)) == (sj >> (sh + 1))) & ((si >> sh) != (sj >> sh))
        x = [_bdot(jnp.where(lvl, nm[p], 0.0), t[p]) for p in pairs]
        t = [t[p] - _bdot(t[p], x[p]) for p in pairs]
        s *= 2

    wu = [_bdot(t[p], jnp.concatenate([l2[p], av[p][0:2 * ch]], axis=1)) for p in pairs]
    wt_ref[0] = jnp.concatenate([_fold_rows(wu[p][:, 0:LANES], ch) for p in pairs], axis=1).astype(wt_ref.dtype)
    ut_ref[0] = jnp.concatenate([_fold_rows(wu[p][:, LANES:2 * LANES], ch) for p in pairs], axis=1)
    yk_ref[0] = jnp.concatenate([_fold_rows(av[p][2 * ch:4 * ch], ch) for p in pairs], axis=1)


def rwkv7_prep(z, mu, w0, w_up, a0, a_up, k_k, k_a, r_k, reverse):
    bsz, seq, _ = z.shape
    ch = RWKV_CHUNK
    cw = RWKV_WIDTH
    nc = seq // ch
    cpb = ch // SUBLANES
    nhb = seq // SUBLANES
    sel_np = (np.arange(cw)[:, None] // RWKV_HEAD_DIM == np.arange(cw)[None, :] // RWKV_HEAD_DIM)
    sel = jnp.asarray(sel_np, BF16)
    zero_pad = jnp.zeros((DECAY_RANK, cw), F32)
    wup = jnp.concatenate([w_up, zero_pad], axis=0).astype(BF16)
    aup = jnp.concatenate([zero_pad, a_up], axis=0).astype(BF16)
    vec = lambda a: a.reshape(1, -1)
    params = [vec(mu), vec(w0), wup, vec(a0), aup, vec(k_k), vec(k_a), vec(r_k), sel]
    full = lambda a: pl.BlockSpec(a.shape, lambda b, c: (0,) * a.ndim)
    if reverse:
        halo = pl.BlockSpec((1, SUBLANES, RWKV_SHIFT_COLS), lambda b, c: (b, jnp.minimum((c + 1) * cpb, nhb - 1), 0))
    else:
        halo = pl.BlockSpec((1, SUBLANES, RWKV_SHIFT_COLS), lambda b, c: (b, jnp.maximum(c * cpb - 1, 0), 0))
    tok = pl.BlockSpec((1, ch, cw), lambda b, c: (b, c, 0))
    tok_bf = jax.ShapeDtypeStruct((bsz, seq, cw), BF16)
    tok_f = jax.ShapeDtypeStruct((bsz, seq, cw), F32)
    return pl.pallas_call(
        functools.partial(_rwkv_prep_body, reverse=reverse, nc=nc),
        grid=(bsz, nc),
        in_specs=[pl.BlockSpec((1, ch, RWKV_SHIFT_COLS), lambda b, c: (b, c, 0)), halo] + [full(a) for a in params],
        out_specs=[tok] * 7
                  + [pl.BlockSpec((1, 1, RWKV_PAIRS, 2 * ch, 2 * ch), lambda b, c: (b, c, 0, 0, 0)),
                     pl.BlockSpec((1, 1, 1, cw), lambda b, c: (b, c, 0, 0)),
                     tok],
        out_shape=[tok_bf, tok_bf, tok_bf, tok_bf, tok_bf, tok_f, tok_f,
                   jax.ShapeDtypeStruct((bsz, nc, RWKV_PAIRS, 2 * ch, 2 * ch), BF16),
                   jax.ShapeDtypeStruct((bsz, nc, 1, cw), F32),
                   tok_f],
        compiler_params=_params(("parallel", "parallel"), 40 << 20),
        name="rwkv7_prep_bwd" if reverse else "rwkv7_prep_fwd",
    )(z, z, *params)


def _rwkv_scan_body(*refs, bsz, final):
    if final:
        (wt_ref, rt_ref, kb_ref, ab_ref, v_ref, ut_ref, yk_ref, ara_ref, ge_ref,
         yin_ref, b1_ref, b2_ref, gd_ref, gup_ref, lnw_ref, lnb_ref, sel_ref, o_ref, st_ref) = refs
    else:
        (wt_ref, rt_ref, kb_ref, ab_ref, v_ref, ut_ref, yk_ref, ara_ref, ge_ref, o_ref, st_ref) = refs
    ch = RWKV_CHUNK

    @pl.when(pl.program_id(0) == 0)
    def _():
        st_ref[...] = jnp.zeros_like(st_ref)

    first = lax.broadcasted_iota(jnp.int32, (1, LANES), 1) < RWKV_HEAD_DIM
    blk = ((lax.broadcasted_iota(jnp.int32, (LANES, LANES), 0) < RWKV_HEAD_DIM)
           == (lax.broadcasted_iota(jnp.int32, (LANES, LANES), 1) < RWKV_HEAD_DIM))
    idx = [(b, p) for b in range(bsz) for p in range(RWKV_PAIRS)]
    sl = [slice(p * LANES, (p + 1) * LANES) for p in range(RWKV_PAIRS)]
    st = [st_ref[b * RWKV_PAIRS + p] for b, p in idx]
    lm0 = [_bdot_nt(jnp.concatenate([wt_ref[b, :, sl[p]], rt_ref[b, :, sl[p]]], axis=0), st[n])
           for n, (b, p) in enumerate(idx)]
    u = [lm0[n][0:ch] + ut_ref[b, :, sl[p]] for n, (b, p) in enumerate(idx)]
    yr = [_bdot(ara_ref[b, 0, p], _pair_rows(u[n], first)) for n, (b, p) in enumerate(idx)]
    upd = [_bdot_tn(jnp.concatenate([v_ref[b, :, sl[p]].astype(F32), -u[n]], axis=0),
                    jnp.concatenate([kb_ref[b, :, sl[p]], ab_ref[b, :, sl[p]]], axis=0))
           for n, (b, p) in enumerate(idx)]
    for n, (b, p) in enumerate(idx):
        st_ref[b * RWKV_PAIRS + p] = st[n] * ge_ref[b, 0, :, sl[p]] + jnp.where(blk, upd[n], 0.0)
    for b in range(bsz):
        y = jnp.concatenate(
            [lm0[b * RWKV_PAIRS + p][ch:2 * ch] + yk_ref[b, :, sl[p]] - _fold_rows(yr[b * RWKV_PAIRS + p], ch)
             for p in range(RWKV_PAIRS)], axis=1)
        if not final:
            o_ref[b] = y
        else:
            sel = sel_ref[...]
            yt = y + yin_ref[b]
            mean = _bdot(yt, sel) * (1.0 / RWKV_HEAD_DIM)
            dv = yt - mean
            var = _bdot(dv * dv, sel) * (1.0 / RWKV_HEAD_DIM)
            yn = dv * lax.rsqrt(var + RWKV_GN_EPS) * lnw_ref[...] + lnb_ref[...]
            g = _bdot(jax.nn.sigmoid(gd_ref[b]), gup_ref[...])
            o_ref[b] = ((yn + b1_ref[b] + b2_ref[b]) * g).astype(o_ref.dtype)


def rwkv7_scan(prep, reverse, final_args=None):
    wt, rt, kb, ab, v, ut, yk, ara, ge, _ = prep
    bsz, seq, cw = wt.shape
    ch = RWKV_CHUNK
    nc = seq // ch
    cidx = (lambda c: nc - 1 - c) if reverse else (lambda c: c)
    tok = pl.BlockSpec((bsz, ch, cw), lambda c: (0, cidx(c), 0))
    in_specs = [tok] * 7 + [pl.BlockSpec((bsz, 1, RWKV_PAIRS, 2 * ch, 2 * ch), lambda c: (0, cidx(c), 0, 0, 0)),
                            pl.BlockSpec((bsz, 1, 1, cw), lambda c: (0, cidx(c), 0, 0))]
    args = [wt, rt, kb, ab, v, ut, yk, ara, ge]
    final = final_args is not None
    if final:
        z, y_in, bonus_a, bonus_b, g_up, ln_w, ln_b = final_args
        sel_np = (np.arange(cw)[:, None] // RWKV_HEAD_DIM == np.arange(cw)[None, :] // RWKV_HEAD_DIM)
        full = lambda a: pl.BlockSpec(a.shape, lambda c: (0,) * a.ndim)
        extra = [g_up.astype(BF16), ln_w.reshape(1, cw), ln_b.reshape(1, cw), jnp.asarray(sel_np, BF16)]
        in_specs += [tok, tok, tok,
                     pl.BlockSpec((bsz, ch, GATE_RANK), lambda c: (0, cidx(c), RWKV_SHIFT_COLS // GATE_RANK))]
        in_specs += [full(a) for a in extra]
        args += [y_in, bonus_a, bonus_b, z] + extra
    out_dtype = BF16 if final else F32
    return pl.pallas_call(
        functools.partial(_rwkv_scan_body, bsz=bsz, final=final),
        grid=(nc,),
        in_specs=in_specs,
        out_specs=tok,
        out_shape=jax.ShapeDtypeStruct((bsz, seq, cw), out_dtype),
        scratch_shapes=[pltpu.VMEM((bsz * RWKV_PAIRS, LANES, LANES), F32)],
        compiler_params=_params(("arbitrary",), 40 << 20),
        name="rwkv7_scan_bwd" if reverse else "rwkv7_scan_fwd",
    )(*args)


def rwkv7_bidir(z, mu, w0, w_up, a0, a_up, g_up, k_k, k_a, r_k, ln_w, ln_b):
    rk = r_k.reshape(-1)
    prep_f = rwkv7_prep(z, mu[0], w0[0], w_up[0], a0[0], a_up[0], k_k, k_a, rk, reverse=False)
    prep_b = rwkv7_prep(z, mu[1], w0[1], w_up[1], a0[1], a_up[1], k_k, k_a, rk, reverse=True)
    y_f = rwkv7_scan(prep_f, reverse=False)
    return rwkv7_scan(prep_b, reverse=True, final_args=(z, y_f, prep_f[9], prep_b[9], g_up, ln_w, ln_b))


def kernel(x, p, mix_norm, mix_w_in, rwkv_mu, rwkv_w0, rwkv_w_up, rwkv_a0, rwkv_a_up, rwkv_g_up, rwkv_k_k, rwkv_k_a, rwkv_r_k, rwkv_ln_w, rwkv_ln_b, hy_short_w, hy_short_b, hy_f_w1, hy_f_b1, hy_f_w2, hy_f_b2, hy_f_w3, hy_f_b3, hy_f_w4, hy_f_freq, hy_bias, mix_w_out, na_norm, na_w_qkv, na_q_g, na_k_g, na_rpb, na_w_out, ffn_norm, ffn_w_up, ffn_conv_w, ffn_conv_b, ffn_w_down, ple_norm, ple_w_gate, ple_w_proj):
    bsz, seq, d = x.shape
    depth = p.shape[0]
    t = bsz * seq
    tm = min(1024, seq)
    h = x.reshape(t, d)
    for i in range(depth):
        j = i // 2
        if i % 2 == 0:
            w_in = mix_w_in[j].astype(BF16)
            z = norm_matmul(h, mix_norm[j], w_in, tm=tm, tn=w_in.shape[1] // 2).reshape(bsz, seq, -1)
            y_a = rwkv7_bidir(z, rwkv_mu[j], rwkv_w0[j], rwkv_w_up[j], rwkv_a0[j], rwkv_a_up[j], rwkv_g_up[j],
                              rwkv_k_k[j], rwkv_k_a[j], rwkv_r_k[j], rwkv_ln_w[j], rwkv_ln_b[j])
            y_b = hyena_bidir(z, RWKV_COLS, hy_short_w[j], hy_short_b[j], hy_f_w1[j], hy_f_b1[j], hy_f_w2[j],
                              hy_f_b2[j], hy_f_w3[j], hy_f_b3[j], hy_f_w4[j], hy_f_freq[j], hy_bias[j])
            w_out = mix_w_out[j].astype(BF16)
            h = matmul_residual([y_a.reshape(t, -1), y_b], [w_out[:RWKV_WIDTH], w_out[RWKV_WIDTH:]], h,
                                tm=tm, tn=512)
        else:
            z = norm_matmul(h, na_norm[j], na_w_qkv[j].astype(BF16), tm=tm, tn=1024)
            q, k, v = qk_norm(z, na_q_g[j], na_k_g[j], tm=min(512, seq))
            o = neighbourhood_attention(q.reshape(bsz, seq, d), k.reshape(bsz, seq, d), v.reshape(bsz, seq, d),
                                        na_rpb[j], rb=8)
            h = matmul_residual([o.reshape(t, d)], [na_w_out[j].astype(BF16)], h, tm=tm, tn=512)
        h = conv_ffn_residual(h, ffn_norm[i], ffn_w_up[i].astype(BF16), ffn_conv_w[i], ffn_conv_b[i],
                              ffn_w_down[i].astype(BF16), seq_len=seq, tm=tm, tf=256)
        h = ple_residual(h, p[i].reshape(t, -1), ple_norm[i], ple_w_gate[i].astype(BF16),
                         ple_w_proj[i].astype(BF16), tm=tm, tn=512)
    return h.reshape(bsz, seq, d)
```

```python
import functools
import math

import numpy as np
import jax
import jax.numpy as jnp
from jax import lax
from jax.experimental import pallas as pl
from jax.experimental.pallas import tpu as pltpu

F32 = jnp.float32
BF16 = jnp.bfloat16
HIGHEST = lax.Precision.HIGHEST

EPS = 1e-6
GRID_W = 64
RWKV_HEADS = 8
RWKV_HEAD_DIM = 64
RWKV_WIDTH = RWKV_HEADS * RWKV_HEAD_DIM
DECAY_RANK = 64
ICLR_RANK = 64
GATE_RANK = 128
RWKV_SHIFT_COLS = 3 * RWKV_WIDTH + DECAY_RANK + ICLR_RANK
RWKV_COLS = RWKV_SHIFT_COLS + GATE_RANK
RWKV_GN_EPS = 64e-5
RWKV_CHUNK = 64
HYENA_WIDTH = 512
FILTER_EMB = 17
FILTER_EMB_PAD = 32
DECAY_MIN = math.log(1e-2) / 1.5
DECAY_MAX = math.log(1e-2) / 0.3
NA_HEADS = 16
NA_HEAD_DIM = 64
NA_WIN_ROWS = 8
NA_WIN_COLS = 16
NEG_BIG = -1e30

LANES = 128
SUBLANES = 8
V7X_VMEM_BYTES = 64 * 1024 * 1024
VMEM_CAP = V7X_VMEM_BYTES - 8 * 1024 * 1024


def _params(semantics, vmem_bytes):
    return pltpu.CompilerParams(dimension_semantics=semantics,
                                vmem_limit_bytes=int(min(max(vmem_bytes, 16 * 1024 * 1024), VMEM_CAP)))


def _rms_rows(x, g):
    ms = jnp.mean(x * x, axis=-1, keepdims=True)
    return x * lax.rsqrt(ms + EPS) * g


def _bdot(a, b):
    return jnp.dot(a.astype(BF16), b.astype(BF16), preferred_element_type=F32)


def _bdot_nt(a, b):
    return lax.dot_general(a.astype(BF16), b.astype(BF16), (((1,), (1,)), ((), ())),
                           preferred_element_type=F32)


def _bdot_tn(a, b):
    return lax.dot_general(a.astype(BF16), b.astype(BF16), (((0,), (0,)), ((), ())),
                           preferred_element_type=F32)


def _dot_exact_lhs(a_bf16, b):
    d = functools.partial(jnp.dot, preferred_element_type=F32)
    b1 = b.astype(BF16)
    r1 = b - b1.astype(F32)
    b2 = r1.astype(BF16)
    b3 = (r1 - b2.astype(F32)).astype(BF16)
    return d(a_bf16, b1) + (d(a_bf16, b2) + d(a_bf16, b3))


def _norm_matmul_body(x_ref, g_ref, w_ref, o_ref, xn_ref):
    @pl.when(pl.program_id(1) == 0)
    def _():
        xn_ref[...] = _rms_rows(x_ref[...], g_ref[...]).astype(BF16)

    o_ref[...] = jnp.dot(xn_ref[...], w_ref[...], preferred_element_type=F32).astype(o_ref.dtype)


def norm_matmul(x, g, w, tm, tn, out_dtype=F32):
    t, d = x.shape
    n = w.shape[1]
    osz = jnp.dtype(out_dtype).itemsize
    vmem = 2 * (tm * d * 4 + d * tn * 2 + tm * tn * osz) + tm * d * 2 + (4 << 20)
    return pl.pallas_call(
        _norm_matmul_body,
        grid=(t // tm, n // tn),
        in_specs=[pl.BlockSpec((tm, d), lambda i, j: (i, 0)),
                  pl.BlockSpec((1, d), lambda i, j: (0, 0)),
                  pl.BlockSpec((d, tn), lambda i, j: (0, j))],
        out_specs=pl.BlockSpec((tm, tn), lambda i, j: (i, j)),
        out_shape=jax.ShapeDtypeStruct((t, n), out_dtype),
        scratch_shapes=[pltpu.VMEM((tm, d), BF16)],
        compiler_params=_params(("parallel", "arbitrary"), vmem),
        name="norm_matmul",
    )(x, g.reshape(1, d), w)


def _matmul_residual_body(*refs, n_in):
    xs = refs[:n_in]
    ws = refs[n_in:2 * n_in]
    h_ref = refs[2 * n_in]
    o_ref = refs[2 * n_in + 1]
    acc = h_ref[...]
    for x_ref, w_ref in zip(xs, ws):
        acc = acc + jnp.dot(x_ref[...], w_ref[...], preferred_element_type=F32)
    o_ref[...] = acc


def matmul_residual(xs, ws, h, tm, tn):
    t, n = h.shape
    n_in = len(xs)
    in_specs = ([pl.BlockSpec((tm, x.shape[1]), lambda i, j: (i, 0)) for x in xs]
                + [pl.BlockSpec((w.shape[0], tn), lambda i, j: (0, j)) for w in ws]
                + [pl.BlockSpec((tm, tn), lambda i, j: (i, j))])
    ksum = sum(x.shape[1] for x in xs)
    vmem = 2 * (tm * ksum * 2 + ksum * tn * 2 + 2 * tm * tn * 4) + (4 << 20)
    return pl.pallas_call(
        functools.partial(_matmul_residual_body, n_in=n_in),
        grid=(t // tm, n // tn),
        in_specs=in_specs,
        out_specs=pl.BlockSpec((tm, tn), lambda i, j: (i, j)),
        out_shape=jax.ShapeDtypeStruct((t, n), F32),
        compiler_params=_params(("parallel", "parallel"), vmem),
        name="matmul_residual",
    )(*xs, *ws, h)


FFN_HALO = 16


def _ffn_body(xm_ref, xp_ref, xx_ref, g_ref, wa_ref, wb_ref, cwa_ref, cwb_ref, cba_ref, cbb_ref,
              wd_ref, o_ref, xn_ref, *, tm, tiles_per_seq, nsub):
    i = pl.program_id(0)
    j = pl.program_id(1)
    hl = FFN_HALO

    @pl.when(j == 0)
    def _():
        g = g_ref[...]
        first = (i % tiles_per_seq) == 0
        last = (i % tiles_per_seq) == tiles_per_seq - 1
        xm = xm_ref[...]
        xn_ref[hl:hl + tm, :] = _rms_rows(xm, g).astype(BF16)
        xn_ref[0:hl, :] = jnp.where(first, 0.0, _rms_rows(xp_ref[...], g)).astype(BF16)
        xn_ref[hl + tm:2 * hl + tm, :] = jnp.where(last, 0.0, _rms_rows(xx_ref[...], g)).astype(BF16)
        o_ref[...] = xm

    ts = tm // nsub
    rows = ts + 2 * hl

    def conv(z, cw_ref, cb_ref):
        cw = cw_ref[...]
        zp = pltpu.roll(z, 1, 0)
        zn = pltpu.roll(z, rows - 1, 0)
        u = zp * cw[0:1, :] + z * cw[1:2, :] + zn * cw[2:3, :] + cb_ref[...]
        return u[hl:hl + ts, :]

    xs = [xn_ref[s * ts:s * ts + rows, :] for s in range(nsub)]
    za = [jnp.dot(xs[s], wa_ref[...], preferred_element_type=F32) for s in range(nsub)]
    zb = [jnp.dot(xs[s], wb_ref[...], preferred_element_type=F32) for s in range(nsub)]
    act = [(jax.nn.gelu(conv(za[s], cwa_ref, cba_ref)) * conv(zb[s], cwb_ref, cbb_ref)).astype(BF16)
           for s in range(nsub)]
    for s in range(nsub):
        o_ref[s * ts:(s + 1) * ts, :] += jnp.dot(act[s], wd_ref[...], preferred_element_type=F32)


def conv_ffn_residual(h, norm_g, w_up, conv_w, conv_b, w_down, seq_len, tm, tf, nsub):
    t, d = h.shape
    f = w_down.shape[0]
    nf = f // tf
    hl = FFN_HALO
    tiles_per_seq = seq_len // tm
    nhb = t // hl
    body = functools.partial(_ffn_body, tm=tm, tiles_per_seq=tiles_per_seq, nsub=nsub)
    vmem = (2 * (tm * d * 4 + 2 * hl * d * 4 + 2 * d * tf * 2 + tf * d * 2 + tm * d * 4)
            + (tm + 2 * hl) * d * 2 + 10 * (tm + 2 * hl) * tf * 4 + (4 << 20))
    return pl.pallas_call(
        body,
        grid=(t // tm, nf),
        in_specs=[
            pl.BlockSpec((tm, d), lambda i, j: (i, 0)),
            pl.BlockSpec((hl, d), lambda i, j: (jnp.maximum(i * (tm // hl) - 1, 0), 0)),
            pl.BlockSpec((hl, d), lambda i, j: (jnp.minimum((i + 1) * (tm // hl), nhb - 1), 0)),
            pl.BlockSpec((1, d), lambda i, j: (0, 0)),
            pl.BlockSpec((d, tf), lambda i, j: (0, j)),
            pl.BlockSpec((d, tf), lambda i, j: (0, nf + j)),
            pl.BlockSpec((3, tf), lambda i, j: (0, j)),
            pl.BlockSpec((3, tf), lambda i, j: (0, nf + j)),
            pl.BlockSpec((1, tf), lambda i, j: (0, j)),
            pl.BlockSpec((1, tf), lambda i, j: (0, nf + j)),
            pl.BlockSpec((tf, d), lambda i, j: (j, 0)),
        ],
        out_specs=pl.BlockSpec((tm, d), lambda i, j: (i, 0)),
        out_shape=jax.ShapeDtypeStruct((t, d), F32),
        scratch_shapes=[pltpu.VMEM((tm + 2 * hl, d), BF16)],
        compiler_params=_params(("parallel", "arbitrary"), vmem),
        name="conv_ffn",
    )(h, h, h, norm_g.reshape(1, d), w_up, w_up, conv_w, conv_w,
      conv_b.reshape(1, 2 * f), conv_b.reshape(1, 2 * f), w_down)


def _ple_body(h_ref, hc_ref, g_ref, wg_ref, p_ref, wp_ref, o_ref, xn_ref):
    @pl.when(pl.program_id(1) == 0)
    def _():
        xn_ref[...] = _rms_rows(h_ref[...], g_ref[...]).astype(BF16)

    gate = jax.nn.sigmoid(jnp.dot(xn_ref[...], wg_ref[...], preferred_element_type=F32))
    proj = jnp.dot(p_ref[...].astype(BF16), wp_ref[...], preferred_element_type=F32)
    o_ref[...] = hc_ref[...] + gate * proj


def ple_residual(h, p, norm_g, w_gate, w_proj, tm, tn):
    t, d = h.shape
    pd = p.shape[1]
    vmem = 2 * (tm * d * 4 + 2 * tm * tn * 4 + d * tn * 2 + tm * pd * 4 + pd * tn * 2) + tm * d * 2 + (4 << 20)
    return pl.pallas_call(
        _ple_body,
        grid=(t // tm, d // tn),
        in_specs=[pl.BlockSpec((tm, d), lambda i, j: (i, 0)),
                  pl.BlockSpec((tm, tn), lambda i, j: (i, j)),
                  pl.BlockSpec((1, d), lambda i, j: (0, 0)),
                  pl.BlockSpec((d, tn), lambda i, j: (0, j)),
                  pl.BlockSpec((tm, pd), lambda i, j: (i, 0)),
                  pl.BlockSpec((pd, tn), lambda i, j: (0, j))],
        out_specs=pl.BlockSpec((tm, tn), lambda i, j: (i, j)),
        out_shape=jax.ShapeDtypeStruct((t, d), F32),
        scratch_shapes=[pltpu.VMEM((tm, d), BF16)],
        compiler_params=_params(("parallel", "arbitrary"), vmem),
        name="ple",
    )(h, h, norm_g.reshape(1, d), w_gate, p, w_proj)


NA_KNORM_ROWS = 512


def _pair_head_rms(x, g, first):
    xx = x * x
    ss_a = jnp.sum(jnp.where(first, xx, 0.0), axis=-1, keepdims=True)
    ss_b = jnp.sum(jnp.where(first, 0.0, xx), axis=-1, keepdims=True)
    inv = lax.rsqrt(jnp.where(first, ss_a, ss_b) * (1.0 / NA_HEAD_DIM) + EPS)
    return x * inv * g


def _natten_body(q_ref, k_ref, v_ref, qg_ref, kg_ref, b_ref, o_ref, kn_ref, *, rows, rb):
    ib = pl.program_id(2)
    first = lax.broadcasted_iota(jnp.int32, (1, LANES), 1) < NA_HEAD_DIM
    kwin = NA_WIN_ROWS * GRID_W
    seq = rows * GRID_W

    @pl.when(ib == 0)
    def _():
        def norm_rows(c, carry):
            sl = pl.ds(pl.multiple_of(c * NA_KNORM_ROWS, NA_KNORM_ROWS), NA_KNORM_ROWS)
            kn_ref[sl, :] = _pair_head_rms(k_ref[0, sl, :].astype(F32), kg_ref[...], first).astype(BF16)
            return carry
        lax.fori_loop(0, seq // NA_KNORM_ROWS, norm_rows, 0)

    qn = _pair_head_rms(q_ref[0].astype(F32), qg_ref[...], first)

    combos = [(r, a) for r in range(rb) for a in range(2)]
    kstart, var = [], []
    for r in range(rb):
        i = ib * rb + r
        rs = jnp.clip(i - NA_WIN_ROWS // 2, 0, rows - NA_WIN_ROWS)
        var.append(i - rs)
        kstart.append(pl.multiple_of(rs * GRID_W, GRID_W))
    s = []
    for r, a in combos:
        q = qn[r * GRID_W:(r + 1) * GRID_W, :]
        keep = first if a == 0 else jnp.logical_not(first)
        qa = jnp.where(keep, q, 0.0).astype(BF16)
        k = kn_ref[pl.ds(kstart[r], kwin), :]
        s.append(lax.dot_general(qa, k, (((1,), (1,)), ((), ())), preferred_element_type=F32) + b_ref[a, var[r]])
    p, l = [], []
    for n in range(len(combos)):
        e = jnp.exp(s[n] - jnp.max(s[n], axis=-1, keepdims=True))
        l.append(jnp.sum(e, axis=-1, keepdims=True))
        p.append(e.astype(BF16))
    o = [jnp.dot(p[n], v_ref[0, pl.ds(kstart[r], kwin), :], preferred_element_type=F32) / l[n]
         for n, (r, a) in enumerate(combos)]
    for r in range(rb):
        o_ref[0, r * GRID_W:(r + 1) * GRID_W, :] = jnp.where(first, o[2 * r], o[2 * r + 1]).astype(o_ref.dtype)


def _natten_bias(rpb):
    kh, kw, w = NA_WIN_ROWS, NA_WIN_COLS, GRID_W
    var = np.arange(kh)
    jr = np.arange(kh)
    dr = jr[None, :] + (kh - 1) - var[:, None]
    sel_r = (dr[:, :, None] == np.arange(2 * kh - 1)[None, None, :]).astype(np.float32)
    col = np.arange(w)
    cstart = np.clip(col - kw // 2, 0, w - kw)
    dc = col[None, :] - col[:, None] + (kw - 1)
    valid = (col[None, :] >= cstart[:, None]) & (col[None, :] < cstart[:, None] + kw)
    sel_c = ((dc[:, :, None] == np.arange(2 * kw - 1)[None, None, :]) & valid[:, :, None]).astype(np.float32)
    b = jnp.einsum('hrd,vjr,cgd->hvcjg', rpb, jnp.asarray(sel_r), jnp.asarray(sel_c), precision=HIGHEST)
    mask = np.where(valid, 0.0, NEG_BIG).astype(np.float32)[None, None, :, None, :]
    b = b + jnp.asarray(mask)
    return b.reshape(rpb.shape[0], kh, w, kh * w)


def neighbourhood_attention(z, q_g, k_g, rpb, rb):
    bsz, seq, d3 = z.shape
    d = d3 // 3
    rows = seq // GRID_W
    bias = _natten_bias(rpb)
    hg = d // LANES
    kwin = NA_WIN_ROWS * GRID_W
    qg = jnp.tile(q_g.reshape(1, NA_HEAD_DIM), (1, 2)) * (NA_HEAD_DIM ** -0.5)
    kg = jnp.tile(k_g.reshape(1, NA_HEAD_DIM), (1, 2))
    vmem = (2 * (2 * seq * LANES * 2 + 2 * rb * GRID_W * LANES * 2 + 2 * NA_WIN_ROWS * GRID_W * kwin * 4)
            + seq * LANES * 2 + 6 * rb * GRID_W * kwin * 4 + (4 << 20))
    return pl.pallas_call(
        functools.partial(_natten_body, rows=rows, rb=rb),
        grid=(bsz, hg, rows // rb),
        in_specs=[pl.BlockSpec((1, rb * GRID_W, LANES), lambda b, h, i: (b, i, h)),
                  pl.BlockSpec((1, seq, LANES), lambda b, h, i: (b, 0, hg + h)),
                  pl.BlockSpec((1, seq, LANES), lambda b, h, i: (b, 0, 2 * hg + h)),
                  pl.BlockSpec((1, LANES), lambda b, h, i: (0, 0)),
                  pl.BlockSpec((1, LANES), lambda b, h, i: (0, 0)),
                  pl.BlockSpec((2, NA_WIN_ROWS, GRID_W, kwin), lambda b, h, i: (h, 0, 0, 0))],
        out_specs=pl.BlockSpec((1, rb * GRID_W, LANES), lambda b, h, i: (b, i, h)),
        out_shape=jax.ShapeDtypeStruct((bsz, seq, d), BF16),
        scratch_shapes=[pltpu.VMEM((seq, LANES), BF16)],
        compiler_params=_params(("parallel", "parallel", "arbitrary"), vmem),
        name="natten",
    )(z, z, z, qg, kg, bias)


def _hy_filter_body(f_ref, w1_ref, b1_ref, w2_ref, b2_ref, w3_ref, b3_ref, w4_ref, fr_ref, dl_ref,
                    h_ref, ss_ref):
    d = functools.partial(jnp.dot, preferred_element_type=F32, precision=HIGHEST)
    f = f_ref[...]
    fr = fr_ref[...]
    x = jnp.sin(fr * (d(f, w1_ref[...]) + b1_ref[...]))
    x = jnp.sin(fr * (d(x, w2_ref[...]) + b2_ref[...]))
    x = jnp.sin(fr * (d(x, w3_ref[...]) + b3_ref[...]))
    h = d(x, w4_ref[0]) * jnp.exp(-f[:, 0:1] * dl_ref[...])
    h_ref[...] = h * f[:, FILTER_EMB:FILTER_EMB + 1]

    @pl.when(pl.program_id(0) == 0)
    def _():
        ss_ref[...] = jnp.zeros_like(ss_ref)

    ss_ref[...] += jnp.sum(h * h, axis=0, keepdims=True)


def hyena_filter(seq, w1, b1, w2, b2, w3, b3, w4, freq, tl):
    c = w4.shape[1] // 2
    hid = w1.shape[1]
    t = np.linspace(0.0, 1.0, seq, dtype=np.float32)[:, None]
    bands = (FILTER_EMB - 1) // 2
    ang = (np.float32(2.0 * math.pi / seq) * np.arange(seq, dtype=np.float32)[:, None]
           * np.linspace(1e-4, bands - 1, bands, dtype=np.float32)[None])
    feats = np.concatenate([t, np.cos(ang), -np.sin(ang), np.ones((seq, 1), np.float32),
                            np.zeros((seq, FILTER_EMB_PAD - FILTER_EMB - 1), np.float32)], axis=-1)
    feats_b = np.concatenate([feats[0:1], feats[:0:-1]], axis=0)
    feats_b[0, FILTER_EMB] = 0.0
    feats2 = np.concatenate([feats, feats_b], axis=0)
    w1p = jnp.concatenate([w1, jnp.zeros((FILTER_EMB_PAD - FILTER_EMB, hid), F32)], axis=0)
    w4s = jnp.stack([w4[:, :c], w4[:, c:]])
    deltas = np.abs(np.linspace(DECAY_MIN, DECAY_MAX, c, dtype=np.float32))[None]
    full = lambda a: pl.BlockSpec(a.shape, lambda i: (0,) * a.ndim)
    nblk = seq // tl
    args = [jnp.asarray(feats2), w1p, b1.reshape(1, hid), w2, b2.reshape(1, hid), w3, b3.reshape(1, hid),
            w4s, freq.reshape(1, hid), jnp.asarray(deltas)]
    in_specs = ([pl.BlockSpec((tl, FILTER_EMB_PAD), lambda i: (i, 0))] + [full(a) for a in args[1:7]]
                + [pl.BlockSpec((1, hid, c), lambda i: (i // nblk, 0, 0)), full(args[8]), full(args[9])])
    return pl.pallas_call(
        _hy_filter_body,
        grid=(2 * nblk,),
        in_specs=in_specs,
        out_specs=[pl.BlockSpec((tl, c), lambda i: (i, 0)), pl.BlockSpec((1, c), lambda i: (0, 0))],
        out_shape=[jax.ShapeDtypeStruct((2 * seq, c), F32), jax.ShapeDtypeStruct((1, c), F32)],
        compiler_params=_params(("arbitrary",), 32 << 20),
        name="hyena_filter",
    )(*args)


def _hy_gate_body(z0_ref, z1_ref, zv_ref, w0_ref, w1_ref, wv_ref, b0_ref, b1_ref, bv_ref, s_ref, x0_ref,
                  *, seq):
    row = lax.broadcasted_iota(jnp.int32, (seq, LANES), 0)
    top = row == 0
    bot = row == seq - 1

    def conv(z_ref, w_ref, b_ref):
        z = z_ref[0]
        w = w_ref[...]
        zp = jnp.where(top, 0.0, pltpu.roll(z, 1, 0))
        zn = jnp.where(bot, 0.0, pltpu.roll(z, seq - 1, 0))
        return zp * w[0:1, :] + z * w[1:2, :] + zn * w[2:3, :] + b_ref[...]

    x0_ref[0] = conv(z0_ref, w0_ref, b0_ref)
    s_ref[0] = conv(zv_ref, wv_ref, bv_ref) * conv(z1_ref, w1_ref, b1_ref)


def hyena_gate(z, short_w, short_b, col0):
    bsz, seq, _ = z.shape
    c = HYENA_WIDTH
    nb = c // LANES
    o0 = col0 // LANES
    sb = short_b.reshape(1, 3 * c)
    zspec = lambda off: pl.BlockSpec((1, seq, LANES), lambda b, j: (b, 0, off + j))
    wspec = lambda off: pl.BlockSpec((3, LANES), lambda b, j: (0, off + j))
    bspec = lambda off: pl.BlockSpec((1, LANES), lambda b, j: (0, off + j))
    out = jax.ShapeDtypeStruct((bsz, seq, c), F32)
    return pl.pallas_call(
        functools.partial(_hy_gate_body, seq=seq),
        grid=(bsz, nb),
        in_specs=[zspec(o0), zspec(o0 + nb), zspec(o0 + 2 * nb),
                  wspec(0), wspec(nb), wspec(2 * nb), bspec(0), bspec(nb), bspec(2 * nb)],
        out_specs=[pl.BlockSpec((1, seq, LANES), lambda b, j: (b, 0, j))] * 2,
        out_shape=[out, out],
        compiler_params=_params(("parallel", "parallel"), 2 * 5 * seq * LANES * 4 + 8 * seq * LANES * 4 + (4 << 20)),
        name="hyena_gate",
    )(z, z, z, short_w, short_w, short_w, sb, sb, sb)


def _dft_consts(n, n_in):
    k = np.arange(n)[:, None].astype(np.float64)
    t = np.arange(n_in)[None, :].astype(np.float64)
    th = 2.0 * np.pi * k * t / n
    w_first = np.concatenate([np.cos(th), -np.sin(th)], axis=0)
    w_last = np.concatenate([np.cos(th).T, -np.sin(th).T], axis=1)
    tt = np.arange(n)[None, :].astype(np.float64)
    th2 = 2.0 * np.pi * k * tt / n
    cm, sm = np.cos(th2), np.sin(th2)
    m_mid = np.block([[cm, sm], [-sm, cm]])
    ph = 2.0 * np.pi * k * tt / (n * n)
    tw = np.stack([np.cos(ph), np.sin(ph)], axis=1)[..., None]
    return w_first, w_last, m_mid, tw


def _hy_stage1_body(w_ref, x_ref, o_ref, *, tb):
    w = w_ref[...]
    for j in range(tb):
        o_ref[0, j] = jnp.dot(w, x_ref[0, j], preferred_element_type=F32).astype(o_ref.dtype)


def hyena_stage(w, x, tb, out_dtype):
    bsz, n2, kk, c = x.shape
    m = w.shape[0]
    osz = jnp.dtype(out_dtype).itemsize
    vmem = 2 * (tb * kk * c * 2 + tb * m * c * osz + m * kk * 2) + 2 * m * c * 4 + (4 << 20)
    return pl.pallas_call(
        functools.partial(_hy_stage1_body, tb=tb),
        grid=(bsz, n2 // tb),
        in_specs=[pl.BlockSpec((m, kk), lambda b, j: (0, 0)),
                  pl.BlockSpec((1, tb, kk, c), lambda b, j: (b, j, 0, 0))],
        out_specs=pl.BlockSpec((1, tb, m, c), lambda b, j: (b, j, 0, 0)),
        out_shape=jax.ShapeDtypeStruct((bsz, n2, m, c), out_dtype),
        compiler_params=_params(("parallel", "parallel"), vmem),
        name="hyena_dft_outer",
    )(w, x)


def _twiddle(xr, xi, tc, ts, sign):
    return xr * tc + sign * (xi * ts), xi * tc - sign * (xr * ts)


HYENA_K1_PER_STEP = 4


def _hy_twiddled_input(a_ref, tw_ref, j, n):
    a = a_ref[0, j].astype(F32)
    xr, xi = _twiddle(a[0:n], a[n:2 * n], tw_ref[j, 0], tw_ref[j, 1], 1.0)
    return jnp.concatenate([xr, xi], axis=0).astype(BF16)


def _hy_spec_body(m_ref, tw_ref, a_ref, o_ref, *, n, kb):
    x = [_hy_twiddled_input(a_ref, tw_ref, j, n) for j in range(kb)]
    for j in range(kb):
        o_ref[0, j] = jnp.dot(m_ref[...], x[j], preferred_element_type=F32)


def _hy_mid_body(m_ref, mt_ref, tw_ref, hs_ref, a_ref, o_ref, *, n, kb):
    x = [_hy_twiddled_input(a_ref, tw_ref, j, n) for j in range(kb)]
    s = [jnp.dot(m_ref[...], x[j], preferred_element_type=F32) for j in range(kb)]
    p = []
    for j in range(kb):
        hs = hs_ref[0, j]
        sr, si = s[j][0:n], s[j][n:2 * n]
        hr, hi = hs[0:n], hs[n:2 * n]
        p.append(jnp.concatenate([sr * hr - si * hi, sr * hi + si * hr], axis=0).astype(BF16))
    y = [jnp.dot(mt_ref[...], p[j], preferred_element_type=F32) for j in range(kb)]
    for j in range(kb):
        yr, yi = _twiddle(y[j][0:n], y[j][n:2 * n], tw_ref[j, 0], tw_ref[j, 1], -1.0)
        o_ref[0, j] = jnp.concatenate([yr, yi], axis=0).astype(o_ref.dtype)


def hyena_spectrum(a, m_mid, tw):
    _, n, n2x, c = a.shape
    kb = min(HYENA_K1_PER_STEP, n)
    return pl.pallas_call(
        functools.partial(_hy_spec_body, n=n, kb=kb),
        grid=(n // kb, 1),
        in_specs=[pl.BlockSpec((n2x, n2x), lambda k, b: (0, 0)),
                  pl.BlockSpec((kb, 2, n, 1), lambda k, b: (k, 0, 0, 0)),
                  pl.BlockSpec((1, kb, n2x, c), lambda k, b: (b, k, 0, 0))],
        out_specs=pl.BlockSpec((1, kb, n2x, c), lambda k, b: (b, k, 0, 0)),
        out_shape=jax.ShapeDtypeStruct((1, n, n2x, c), F32),
        compiler_params=_params(("parallel", "parallel"), 40 << 20),
        name="hyena_filter_spectrum",
    )(m_mid, tw, a)


def hyena_mid(a, hspec, m_mid, m_mid_t, tw):
    bsz, n, n2x, c = a.shape
    kb = min(HYENA_K1_PER_STEP, n)
    return pl.pallas_call(
        functools.partial(_hy_mid_body, n=n, kb=kb),
        grid=(n // kb, bsz),
        in_specs=[pl.BlockSpec((n2x, n2x), lambda k, b: (0, 0)),
                  pl.BlockSpec((n2x, n2x), lambda k, b: (0, 0)),
                  pl.BlockSpec((kb, 2, n, 1), lambda k, b: (k, 0, 0, 0)),
                  pl.BlockSpec((1, kb, n2x, c), lambda k, b: (0, k, 0, 0)),
                  pl.BlockSpec((1, kb, n2x, c), lambda k, b: (b, k, 0, 0))],
        out_specs=pl.BlockSpec((1, kb, n2x, c), lambda k, b: (b, k, 0, 0)),
        out_shape=jax.ShapeDtypeStruct((bsz, n, n2x, c), BF16),
        compiler_params=_params(("parallel", "parallel"), 40 << 20),
        name="hyena_dft_mid",
    )(m_mid, m_mid_t, tw, hspec, a)


def _hy_combine_body(y_ref, s_ref, x0_ref, sc_ref, bi_ref, o_ref):
    o_ref[...] = ((y_ref[...] * sc_ref[...] + s_ref[...] * bi_ref[...]) * x0_ref[...]).astype(o_ref.dtype)


def hyena_combine(y, s, x0, scale, bias, tm):
    t, c = y.shape
    row = lambda: pl.BlockSpec((tm, c), lambda i: (i, 0))
    vec = lambda: pl.BlockSpec((1, c), lambda i: (0, 0))
    return pl.pallas_call(
        _hy_combine_body,
        grid=(t // tm,),
        in_specs=[row(), row(), row(), vec(), vec()],
        out_specs=row(),
        out_shape=jax.ShapeDtypeStruct((t, c), BF16),
        compiler_params=_params(("parallel",), 2 * 4 * tm * c * 4 + (4 << 20)),
        name="hyena_combine",
    )(y, s, x0, scale.reshape(1, c), bias.reshape(1, c))


def hyena_bidir(z, col0, short_w, short_b, f_w1, f_b1, f_w2, f_b2, f_w3, f_b3, f_w4, f_freq, bias):
    bsz, seq, _ = z.shape
    c = HYENA_WIDTH
    n = int(round(math.sqrt(2 * seq)))
    assert n * n == 2 * seq and n % 16 == 0
    nh = n // 2
    tb = 8

    w_first, w_last, m_mid, tw = _dft_consts(n, n)
    w_first_x = jnp.asarray(w_first[:, :nh], BF16)
    w_first_h = jnp.asarray(w_first, BF16)
    w_last_y = jnp.asarray(w_last[:nh], BF16)
    m_mid_j = jnp.asarray(m_mid, BF16)
    m_mid_t = jnp.asarray(m_mid.T, BF16)
    tw_j = jnp.asarray(tw, F32)

    kern, ss = hyena_filter(seq, f_w1, f_b1, f_w2, f_b2, f_w3, f_b3, f_w4, f_freq, tl=min(seq, 1024))
    scale = lax.rsqrt(ss[0] + EPS) * (1.0 / (2 * seq))
    kern_t = jnp.transpose(kern.reshape(1, n, n, c), (0, 2, 1, 3)).astype(BF16)
    ah = hyena_stage(w_first_h, kern_t, tb, BF16)
    ah = jnp.transpose(ah.reshape(1, n, 2, n, c), (0, 3, 2, 1, 4)).reshape(1, n, 2 * n, c)
    hspec = hyena_spectrum(ah, m_mid_j, tw_j)

    s, x0 = hyena_gate(z, short_w, short_b, col0)
    s_t = jnp.transpose(s.reshape(bsz, nh, n, c), (0, 2, 1, 3)).astype(BF16)
    a = hyena_stage(w_first_x, s_t, tb, BF16)
    a = jnp.transpose(a.reshape(bsz, n, 2, n, c), (0, 3, 2, 1, 4)).reshape(bsz, n, 2 * n, c)
    bm = hyena_mid(a, hspec, m_mid_j, m_mid_t, tw_j)
    bm = jnp.transpose(bm.reshape(bsz, n, 2, n, c), (0, 3, 2, 1, 4)).reshape(bsz, n, 2 * n, c)
    y = hyena_stage(w_last_y, bm, tb, F32)
    y = jnp.transpose(y, (0, 2, 1, 3)).reshape(bsz * seq, c)
    return hyena_combine(y, s.reshape(bsz * seq, c), x0.reshape(bsz * seq, c), scale, bias,
                         tm=min(bsz * seq, 2048))


RWKV_PAIRS = RWKV_WIDTH // LANES
RWKV_PREP_CHUNKS = 4


def _pair_rows(x, first):
    return jnp.concatenate([jnp.where(first, x, 0.0), jnp.where(first, 0.0, x)], axis=0)


def _fold_rows(x2, ch):
    return x2[0:ch] + x2[ch:2 * ch]


def _rwkv_prep_body(z_ref, halo_ref, mu_ref, w0_ref, wup_ref, a0_ref, aup_ref, kk_ref, ka_ref, rk_ref, sel_ref,
                    wt_ref, rt_ref, kb_ref, ab_ref, v_ref, ut_ref, yk_ref, ara_ref, ge_ref, bonus_ref,
                    *, reverse, nsteps, cps):
    ch = RWKV_CHUNK
    cw = RWKV_WIDTH
    rows = cps * ch
    c_idx = pl.program_id(1)

    zs = z_ref[0]
    row = lax.broadcasted_iota(jnp.int32, (rows, 1), 0)
    if reverse:
        nbr = jnp.where(c_idx == nsteps - 1, 0.0, halo_ref[0, 0:1, :])
        shifted = jnp.where(row == rows - 1, jnp.broadcast_to(nbr, zs.shape), pltpu.roll(zs, rows - 1, 0))
    else:
        nbr = jnp.where(c_idx == 0, 0.0, halo_ref[0, SUBLANES - 1:SUBLANES, :])
        shifted = jnp.where(row == 0, jnp.broadcast_to(nbr, zs.shape), pltpu.roll(zs, 1, 0))
    zd = zs + (shifted - zs) * mu_ref[...]

    r = zd[:, 0:cw]
    k = zd[:, cw:2 * cw]
    v = zd[:, 2 * cw:3 * cw]
    lora = zd[:, 3 * cw:3 * cw + DECAY_RANK + ICLR_RANK]
    x = w0_ref[...] + _bdot(jnp.tanh(lora), wup_ref[...])
    log_w = jnp.minimum(x, 0.0) - jnp.log(1.0 + jnp.exp(-jnp.abs(x))) - 0.5
    lnw = -jnp.exp(log_w)
    a = jax.nn.sigmoid(a0_ref[...] + _bdot(lora, aup_ref[...]))
    sel = sel_ref[...]
    kkr = k * kk_ref[...]
    kkn = jnp.sqrt(_bdot(kkr * kkr, sel))
    kk = kkr / jnp.maximum(kkn, 1e-12)
    k2 = k * (1.0 + (a - 1.0) * ka_ref[...])
    ah = kk * a
    bonus_ref[0] = _bdot(r * k2 * rk_ref[...], sel) * v

    ri = lax.broadcasted_iota(jnp.int32, (rows, rows), 0)
    ci = lax.broadcasted_iota(jnp.int32, (rows, rows), 1)
    csh = ch.bit_length() - 1
    tri = ((ci >= ri) if reverse else (ci <= ri)) & ((ri >> csh) == (ci >> csh))
    cs = _dot_exact_lhs(tri.astype(BF16), lnw)
    total = jnp.concatenate(
        [jnp.broadcast_to(jnp.sum(lnw[g * ch:(g + 1) * ch], axis=0, keepdims=True), (ch, cw)) for g in range(cps)],
        axis=0)
    e_inv = jnp.exp(-cs)
    e_end = jnp.exp(total - cs)
    rt = r * jnp.exp(cs)
    kkt = kk * jnp.exp(cs - lnw)
    kh = k2 * e_inv
    ahh = ah * e_inv
    rt_ref[0] = rt.astype(rt_ref.dtype)
    kb_ref[0] = (k2 * e_end).astype(kb_ref.dtype)
    ab_ref[0] = (ah * e_end).astype(ab_ref.dtype)
    v_ref[0] = v.astype(v_ref.dtype)
    for g in range(cps):
        ge_ref[0, g] = jnp.exp(total[g * ch:g * ch + 1])

    r2 = lax.broadcasted_iota(jnp.int32, (2 * ch, 2 * ch), 0)
    c2 = lax.broadcasted_iota(jnp.int32, (2 * ch, 2 * ch), 1)
    same_head = (r2 < ch) == (c2 < ch)
    si = r2 & (ch - 1)
    sj = c2 & (ch - 1)
    before = (sj > si) if reverse else (sj < si)
    strict = same_head & before
    incl = same_head & jnp.logical_or(before, si == sj)
    eye = (r2 == c2).astype(F32)
    first = lax.broadcasted_iota(jnp.int32, (1, LANES), 1) < RWKV_HEAD_DIM

    units = [(g, p) for g in range(cps) for p in range(RWKV_PAIRS)]
    un = range(len(units))
    tile = lambda arr, g, p: arr[g * ch:(g + 1) * ch, p * LANES:(p + 1) * LANES]
    l2 = [_pair_rows(tile(kkt, g, p), first) for g, p in units]
    l4 = [jnp.concatenate([l2[n], _pair_rows(tile(rt, g, p), first)], axis=0) for n, (g, p) in enumerate(units)]
    v2 = [_pair_rows(tile(v, g, p), first) for g, p in units]
    sk = [_bdot_nt(l4[n], jnp.concatenate([tile(kh, g, p)] * 2, axis=0)) for n, (g, p) in enumerate(units)]
    sa = [_bdot_nt(l4[n], jnp.concatenate([tile(ahh, g, p)] * 2, axis=0)) for n, (g, p) in enumerate(units)]
    nm = [jnp.where(strict, sa[n][0:2 * ch], 0.0) for n in un]
    for n, (g, p) in enumerate(units):
        ara_ref[0, g, p] = jnp.where(incl, sa[n][2 * ch:4 * ch], 0.0).astype(ara_ref.dtype)
    akr = [jnp.concatenate([jnp.where(strict, sk[n][0:2 * ch], 0.0),
                            jnp.where(incl, sk[n][2 * ch:4 * ch], 0.0)], axis=0) for n in un]
    av = [_bdot(akr[n], v2[n]) for n in un]

    lvl = (si >> 1) == (sj >> 1)
    t = [eye - jnp.where(lvl, nm[n], 0.0) for n in un]
    s = 2
    while s < ch:
        sh = s.bit_length() - 1
        lvl = ((si >> (sh + 1)) == (sj >> (sh + 1))) & ((si >> sh) != (sj >> sh))
        x = [_bdot(jnp.where(lvl, nm[n], 0.0), t[n]) for n in un]
        t = [t[n] - _bdot(t[n], x[n]) for n in un]
        s *= 2

    wu = [_bdot(t[n], jnp.concatenate([l2[n], av[n][0:2 * ch]], axis=1)) for n in un]
    gather = lambda f: jnp.concatenate(
        [jnp.concatenate([f(g * RWKV_PAIRS + p) for p in range(RWKV_PAIRS)], axis=1) for g in range(cps)], axis=0)
    wt_ref[0] = gather(lambda n: _fold_rows(wu[n][:, 0:LANES], ch)).astype(wt_ref.dtype)
    ut_ref[0] = gather(lambda n: _fold_rows(wu[n][:, LANES:2 * LANES], ch))
    yk_ref[0] = gather(lambda n: _fold_rows(av[n][2 * ch:4 * ch], ch))


def rwkv7_prep(z, mu, w0, w_up, a0, a_up, k_k, k_a, r_k, reverse):
    bsz, seq, _ = z.shape
    ch = RWKV_CHUNK
    cw = RWKV_WIDTH
    nc = seq // ch
    cps = min(RWKV_PREP_CHUNKS, nc)
    nsteps = nc // cps
    cpb = cps * ch // SUBLANES
    nhb = seq // SUBLANES
    sel_np = (np.arange(cw)[:, None] // RWKV_HEAD_DIM == np.arange(cw)[None, :] // RWKV_HEAD_DIM)
    sel = jnp.asarray(sel_np, BF16)
    zero_pad = jnp.zeros((DECAY_RANK, cw), F32)
    wup = jnp.concatenate([w_up, zero_pad], axis=0).astype(BF16)
    aup = jnp.concatenate([zero_pad, a_up], axis=0).astype(BF16)
    vec = lambda a: a.reshape(1, -1)
    params = [vec(mu), vec(w0), wup, vec(a0), aup, vec(k_k), vec(k_a), vec(r_k), sel]
    full = lambda a: pl.BlockSpec(a.shape, lambda b, c: (0,) * a.ndim)
    if reverse:
        halo = pl.BlockSpec((1, SUBLANES, RWKV_SHIFT_COLS), lambda b, c: (b, jnp.minimum((c + 1) * cpb, nhb - 1), 0))
    else:
        halo = pl.BlockSpec((1, SUBLANES, RWKV_SHIFT_COLS), lambda b, c: (b, jnp.maximum(c * cpb - 1, 0), 0))
    tok = pl.BlockSpec((1, cps * ch, cw), lambda b, c: (b, c, 0))
    tok_bf = jax.ShapeDtypeStruct((bsz, seq, cw), BF16)
    tok_f = jax.ShapeDtypeStruct((bsz, seq, cw), F32)
    return pl.pallas_call(
        functools.partial(_rwkv_prep_body, reverse=reverse, nsteps=nsteps, cps=cps),
        grid=(bsz, nsteps),
        in_specs=[pl.BlockSpec((1, cps * ch, RWKV_SHIFT_COLS), lambda b, c: (b, c, 0)), halo]
                 + [full(a) for a in params],
        out_specs=[tok] * 7
                  + [pl.BlockSpec((1, cps, RWKV_PAIRS, 2 * ch, 2 * ch), lambda b, c: (b, c, 0, 0, 0)),
                     pl.BlockSpec((1, cps, 1, cw), lambda b, c: (b, c, 0, 0)),
                     tok],
        out_shape=[tok_bf, tok_bf, tok_bf, tok_bf, tok_bf, tok_f, tok_f,
                   jax.ShapeDtypeStruct((bsz, nc, RWKV_PAIRS, 2 * ch, 2 * ch), BF16),
                   jax.ShapeDtypeStruct((bsz, nc, 1, cw), F32),
                   tok_f],
        compiler_params=_params(("parallel", "parallel"), 40 << 20),
        name="rwkv7_prep_bwd" if reverse else "rwkv7_prep_fwd",
    )(z, z, *params)


def _rwkv_scan_body(*refs, bsz, final):
    if final:
        (wt_ref, rt_ref, kb_ref, ab_ref, v_ref, ut_ref, yk_ref, ara_ref, ge_ref,
         yin_ref, b1_ref, b2_ref, gd_ref, gup_ref, lnw_ref, lnb_ref, sel_ref, o_ref, st_ref) = refs
    else:
        (wt_ref, rt_ref, kb_ref, ab_ref, v_ref, ut_ref, yk_ref, ara_ref, ge_ref, o_ref, st_ref) = refs
    ch = RWKV_CHUNK

    @pl.when(pl.program_id(0) == 0)
    def _():
        st_ref[...] = jnp.zeros_like(st_ref)

    first = lax.broadcasted_iota(jnp.int32, (1, LANES), 1) < RWKV_HEAD_DIM
    blk = ((lax.broadcasted_iota(jnp.int32, (LANES, LANES), 0) < RWKV_HEAD_DIM)
           == (lax.broadcasted_iota(jnp.int32, (LANES, LANES), 1) < RWKV_HEAD_DIM))
    idx = [(b, p) for b in range(bsz) for p in range(RWKV_PAIRS)]
    sl = [slice(p * LANES, (p + 1) * LANES) for p in range(RWKV_PAIRS)]
    st = [st_ref[b * RWKV_PAIRS + p] for b, p in idx]
    lm0 = [_bdot_nt(jnp.concatenate([wt_ref[b, :, sl[p]], rt_ref[b, :, sl[p]]], axis=0), st[n])
           for n, (b, p) in enumerate(idx)]
    u = [lm0[n][0:ch] + ut_ref[b, :, sl[p]] for n, (b, p) in enumerate(idx)]
    yr = [_bdot(ara_ref[b, 0, p], _pair_rows(u[n], first)) for n, (b, p) in enumerate(idx)]
    upd = [_bdot_tn(jnp.concatenate([v_ref[b, :, sl[p]].astype(F32), -u[n]], axis=0),
                    jnp.concatenate([kb_ref[b, :, sl[p]], ab_ref[b, :, sl[p]]], axis=0))
           for n, (b, p) in enumerate(idx)]
    for n, (b, p) in enumerate(idx):
        st_ref[b * RWKV_PAIRS + p] = st[n] * ge_ref[b, 0, :, sl[p]] + jnp.where(blk, upd[n], 0.0)
    for b in range(bsz):
        y = jnp.concatenate(
            [lm0[b * RWKV_PAIRS + p][ch:2 * ch] + yk_ref[b, :, sl[p]] - _fold_rows(yr[b * RWKV_PAIRS + p], ch)
             for p in range(RWKV_PAIRS)], axis=1)
        if not final:
            o_ref[b] = y
        else:
            sel = sel_ref[...]
            yt = y + yin_ref[b]
            mean = _bdot(yt, sel) * (1.0 / RWKV_HEAD_DIM)
            dv = yt - mean
            var = _bdot(dv * dv, sel) * (1.0 / RWKV_HEAD_DIM)
            yn = dv * lax.rsqrt(var + RWKV_GN_EPS) * lnw_ref[...] + lnb_ref[...]
            g = _bdot(jax.nn.sigmoid(gd_ref[b]), gup_ref[...])
            o_ref[b] = ((yn + b1_ref[b] + b2_ref[b]) * g).astype(o_ref.dtype)


def rwkv7_scan(prep, reverse, final_args=None):
    wt, rt, kb, ab, v, ut, yk, ara, ge, _ = prep
    bsz, seq, cw = wt.shape
    ch = RWKV_CHUNK
    nc = seq // ch
    cidx = (lambda c: nc - 1 - c) if reverse else (lambda c: c)
    tok = pl.BlockSpec((bsz, ch, cw), lambda c: (0, cidx(c), 0))
    in_specs = [tok] * 7 + [pl.BlockSpec((bsz, 1, RWKV_PAIRS, 2 * ch, 2 * ch), lambda c: (0, cidx(c), 0, 0, 0)),
                            pl.BlockSpec((bsz, 1, 1, cw), lambda c: (0, cidx(c), 0, 0))]
    args = [wt, rt, kb, ab, v, ut, yk, ara, ge]
    final = final_args is not None
    if final:
        z, y_in, bonus_a, bonus_b, g_up, ln_w, ln_b = final_args
        sel_np = (np.arange(cw)[:, None] // RWKV_HEAD_DIM == np.arange(cw)[None, :] // RWKV_HEAD_DIM)
        full = lambda a: pl.BlockSpec(a.shape, lambda c: (0,) * a.ndim)
        extra = [g_up.astype(BF16), ln_w.reshape(1, cw), ln_b.reshape(1, cw), jnp.asarray(sel_np, BF16)]
        in_specs += [tok, tok, tok,
                     pl.BlockSpec((bsz, ch, GATE_RANK), lambda c: (0, cidx(c), RWKV_SHIFT_COLS // GATE_RANK))]
        in_specs += [full(a) for a in extra]
        args += [y_in, bonus_a, bonus_b, z] + extra
    out_dtype = BF16 if final else F32
    return pl.pallas_call(
        functools.partial(_rwkv_scan_body, bsz=bsz, final=final),
        grid=(nc,),
        in_specs=in_specs,
        out_specs=tok,
        out_shape=jax.ShapeDtypeStruct((bsz, seq, cw), out_dtype),
        scratch_shapes=[pltpu.VMEM((bsz * RWKV_PAIRS, LANES, LANES), F32)],
        compiler_params=_params(("arbitrary",), 40 << 20),
        name="rwkv7_scan_bwd" if reverse else "rwkv7_scan_fwd",
    )(*args)


def rwkv7_bidir(z, mu, w0, w_up, a0, a_up, g_up, k_k, k_a, r_k, ln_w, ln_b):
    rk = r_k.reshape(-1)
    prep_f = rwkv7_prep(z, mu[0], w0[0], w_up[0], a0[0], a_up[0], k_k, k_a, rk, reverse=False)
    prep_b = rwkv7_prep(z, mu[1], w0[1], w_up[1], a0[1], a_up[1], k_k, k_a, rk, reverse=True)
    y_f = rwkv7_scan(prep_f, reverse=False)
    return rwkv7_scan(prep_b, reverse=True, final_args=(z, y_f, prep_f[9], prep_b[9], g_up, ln_w, ln_b))


def kernel(x, p, mix_norm, mix_w_in, rwkv_mu, rwkv_w0, rwkv_w_up, rwkv_a0, rwkv_a_up, rwkv_g_up, rwkv_k_k, rwkv_k_a, rwkv_r_k, rwkv_ln_w, rwkv_ln_b, hy_short_w, hy_short_b, hy_f_w1, hy_f_b1, hy_f_w2, hy_f_b2, hy_f_w3, hy_f_b3, hy_f_w4, hy_f_freq, hy_bias, mix_w_out, na_norm, na_w_qkv, na_q_g, na_k_g, na_rpb, na_w_out, ffn_norm, ffn_w_up, ffn_conv_w, ffn_conv_b, ffn_w_down, ple_norm, ple_w_gate, ple_w_proj):
    bsz, seq, d = x.shape
    depth = p.shape[0]
    t = bsz * seq
    tm = min(1024, seq)
    h = x.reshape(t, d)
    for i in range(depth):
        j = i // 2
        if i % 2 == 0:
            w_in = mix_w_in[j].astype(BF16)
            z = norm_matmul(h, mix_norm[j], w_in, tm=tm, tn=w_in.shape[1] // 2).reshape(bsz, seq, -1)
            y_a = rwkv7_bidir(z, rwkv_mu[j], rwkv_w0[j], rwkv_w_up[j], rwkv_a0[j], rwkv_a_up[j], rwkv_g_up[j],
                              rwkv_k_k[j], rwkv_k_a[j], rwkv_r_k[j], rwkv_ln_w[j], rwkv_ln_b[j])
            y_b = hyena_bidir(z, RWKV_COLS, hy_short_w[j], hy_short_b[j], hy_f_w1[j], hy_f_b1[j], hy_f_w2[j],
                              hy_f_b2[j], hy_f_w3[j], hy_f_b3[j], hy_f_w4[j], hy_f_freq[j], hy_bias[j])
            w_out = mix_w_out[j].astype(BF16)
            h = matmul_residual([y_a.reshape(t, -1), y_b], [w_out[:RWKV_WIDTH], w_out[RWKV_WIDTH:]], h,
                                tm=tm, tn=512)
        else:
            z = norm_matmul(h, na_norm[j], na_w_qkv[j].astype(BF16), tm=tm, tn=1024, out_dtype=BF16)
            o = neighbourhood_attention(z.reshape(bsz, seq, 3 * d), na_q_g[j], na_k_g[j], na_rpb[j], rb=8)
            h = matmul_residual([o.reshape(t, d)], [na_w_out[j].astype(BF16)], h, tm=tm, tn=512)
        h = conv_ffn_residual(h, ffn_norm[i], ffn_w_up[i].astype(BF16), ffn_conv_w[i], ffn_conv_b[i],
                              ffn_w_down[i].astype(BF16), seq_len=seq, tm=tm, tf=256, nsub=2)
        h = ple_residual(h, p[i].reshape(t, -1), ple_norm[i], ple_w_gate[i].astype(BF16),
                         ple_w_proj[i].astype(BF16), tm=tm, tn=512)
    return h.reshape(bsz, seq, d)
```

```python
import functools
import math

import numpy as np
import jax
import jax.numpy as jnp
from jax import lax
from jax.experimental import pallas as pl
from jax.experimental.pallas import tpu as pltpu

F32 = jnp.float32
BF16 = jnp.bfloat16
HIGHEST = lax.Precision.HIGHEST

EPS = 1e-6
GRID_W = 64
RWKV_HEADS = 8
RWKV_HEAD_DIM = 64
RWKV_WIDTH = RWKV_HEADS * RWKV_HEAD_DIM
DECAY_RANK = 64
ICLR_RANK = 64
GATE_RANK = 128
RWKV_SHIFT_COLS = 3 * RWKV_WIDTH + DECAY_RANK + ICLR_RANK
RWKV_COLS = RWKV_SHIFT_COLS + GATE_RANK
RWKV_GN_EPS = 64e-5
RWKV_CHUNK = 64
HYENA_WIDTH = 512
FILTER_EMB = 17
FILTER_EMB_PAD = 32
DECAY_MIN = math.log(1e-2) / 1.5
DECAY_MAX = math.log(1e-2) / 0.3
NA_HEADS = 16
NA_HEAD_DIM = 64
NA_WIN_ROWS = 8
NA_WIN_COLS = 16
NEG_BIG = -1e30

LANES = 128
SUBLANES = 8
V7X_VMEM_BYTES = 64 * 1024 * 1024
VMEM_CAP = V7X_VMEM_BYTES - 8 * 1024 * 1024


def _params(semantics, vmem_bytes):
    return pltpu.CompilerParams(dimension_semantics=semantics,
                                vmem_limit_bytes=int(min(max(vmem_bytes, 16 * 1024 * 1024), VMEM_CAP)))


def _rms_rows(x, g):
    ms = jnp.mean(x * x, axis=-1, keepdims=True)
    return x * lax.rsqrt(ms + EPS) * g


def _bdot(a, b):
    return jnp.dot(a.astype(BF16), b.astype(BF16), preferred_element_type=F32)


def _bdot_nt(a, b):
    return lax.dot_general(a.astype(BF16), b.astype(BF16), (((1,), (1,)), ((), ())),
                           preferred_element_type=F32)


def _bdot_tn(a, b):
    return lax.dot_general(a.astype(BF16), b.astype(BF16), (((0,), (0,)), ((), ())),
                           preferred_element_type=F32)


def _dot_exact_lhs(a_bf16, b):
    d = functools.partial(jnp.dot, preferred_element_type=F32)
    b1 = b.astype(BF16)
    r1 = b - b1.astype(F32)
    b2 = r1.astype(BF16)
    b3 = (r1 - b2.astype(F32)).astype(BF16)
    return d(a_bf16, b1) + (d(a_bf16, b2) + d(a_bf16, b3))


def _norm_matmul_body(x_ref, g_ref, w_ref, o_ref, xn_ref):
    @pl.when(pl.program_id(1) == 0)
    def _():
        xn_ref[...] = _rms_rows(x_ref[...], g_ref[...]).astype(BF16)

    o_ref[...] = jnp.dot(xn_ref[...], w_ref[...], preferred_element_type=F32).astype(o_ref.dtype)


def norm_matmul(x, g, w, tm, tn, out_dtype=F32):
    t, d = x.shape
    n = w.shape[1]
    osz = jnp.dtype(out_dtype).itemsize
    vmem = 2 * (tm * d * 4 + d * tn * 2 + tm * tn * osz) + tm * d * 2 + (4 << 20)
    return pl.pallas_call(
        _norm_matmul_body,
        grid=(t // tm, n // tn),
        in_specs=[pl.BlockSpec((tm, d), lambda i, j: (i, 0)),
                  pl.BlockSpec((1, d), lambda i, j: (0, 0)),
                  pl.BlockSpec((d, tn), lambda i, j: (0, j))],
        out_specs=pl.BlockSpec((tm, tn), lambda i, j: (i, j)),
        out_shape=jax.ShapeDtypeStruct((t, n), out_dtype),
        scratch_shapes=[pltpu.VMEM((tm, d), BF16)],
        compiler_params=_params(("parallel", "arbitrary"), vmem),
        name="norm_matmul",
    )(x, g.reshape(1, d), w)


def _matmul_residual_body(*refs, n_in):
    xs = refs[:n_in]
    ws = refs[n_in:2 * n_in]
    h_ref = refs[2 * n_in]
    o_ref = refs[2 * n_in + 1]
    acc = h_ref[...]
    for x_ref, w_ref in zip(xs, ws):
        acc = acc + jnp.dot(x_ref[...], w_ref[...], preferred_element_type=F32)
    o_ref[...] = acc


def matmul_residual(xs, ws, h, tm, tn):
    t, n = h.shape
    n_in = len(xs)
    in_specs = ([pl.BlockSpec((tm, x.shape[1]), lambda i, j: (i, 0)) for x in xs]
                + [pl.BlockSpec((w.shape[0], tn), lambda i, j: (0, j)) for w in ws]
                + [pl.BlockSpec((tm, tn), lambda i, j: (i, j))])
    ksum = sum(x.shape[1] for x in xs)
    vmem = 2 * (tm * ksum * 2 + ksum * tn * 2 + 2 * tm * tn * 4) + (4 << 20)
    return pl.pallas_call(
        functools.partial(_matmul_residual_body, n_in=n_in),
        grid=(t // tm, n // tn),
        in_specs=in_specs,
        out_specs=pl.BlockSpec((tm, tn), lambda i, j: (i, j)),
        out_shape=jax.ShapeDtypeStruct((t, n), F32),
        compiler_params=_params(("parallel", "parallel"), vmem),
        name="matmul_residual",
    )(*xs, *ws, h)


FFN_HALO = 16
FFN_ROW_BLOCK = 64


def _ffn_ple_body(xm_ref, xp_ref, xx_ref, g_ref,
                  wa0_ref, wb0_ref, wa1_ref, wb1_ref,
                  cwa0_ref, cwb0_ref, cba0_ref, cbb0_ref, wd0_ref,
                  cwa1_ref, cwb1_ref, cba1_ref, cbb1_ref, wd1_ref,
                  pg_ref, wg_ref, p_ref, wp_ref,
                  o_ref, xn_ref, za0_ref, zb0_ref, za1_ref, zb1_ref, act_ref, *, tm, tiles_per_seq, nd):
    i = pl.program_id(0)
    m = pl.program_id(1)
    hl = FFN_HALO
    rows = tm + 2 * hl

    @pl.when(m == 0)
    def _():
        g = g_ref[...]
        first = (i % tiles_per_seq) == 0
        last = (i % tiles_per_seq) == tiles_per_seq - 1
        xm = xm_ref[...]
        xn_ref[hl:hl + tm, :] = _rms_rows(xm, g).astype(BF16)
        xn_ref[0:hl, :] = jnp.where(first, 0.0, _rms_rows(xp_ref[...], g)).astype(BF16)
        xn_ref[hl + tm:2 * hl + tm, :] = jnp.where(last, 0.0, _rms_rows(xx_ref[...], g)).astype(BF16)
        o_ref[...] = xm
        za1_ref[...] = jnp.zeros_like(za1_ref)
        zb1_ref[...] = jnp.zeros_like(zb1_ref)

    def conv(z_ref, r0, cw, cb):
        return (z_ref[r0 - 1:r0 - 1 + FFN_ROW_BLOCK, :] * cw[0:1, :] + z_ref[r0:r0 + FFN_ROW_BLOCK, :] * cw[1:2, :]
                + z_ref[r0 + 1:r0 + 1 + FFN_ROW_BLOCK, :] * cw[2:3, :] + cb)

    def contribution(za_ref, zb_ref, cwa_ref, cwb_ref, cba_ref, cbb_ref, wd_ref):
        cwa, cwb, cba, cbb = cwa_ref[...], cwb_ref[...], cba_ref[...], cbb_ref[...]
        for blk in range(tm // FFN_ROW_BLOCK):
            r0 = hl + blk * FFN_ROW_BLOCK
            act = jax.nn.gelu(conv(za_ref, r0, cwa, cba)) * conv(zb_ref, r0, cwb, cbb)
            act_ref[blk * FFN_ROW_BLOCK:(blk + 1) * FFN_ROW_BLOCK, :] = act.astype(BF16)
        return jnp.dot(act_ref[...], wd_ref[...], preferred_element_type=F32)

    xn = xn_ref[...]
    za0_ref[...] = jnp.dot(xn, wa0_ref[...], preferred_element_type=F32)
    zb0_ref[...] = jnp.dot(xn, wb0_ref[...], preferred_element_type=F32)
    ca = contribution(za1_ref, zb1_ref, cwa0_ref, cwb0_ref, cba0_ref, cbb0_ref, wd0_ref)
    o_ref[...] += jnp.where(m > 0, ca, 0.0)
    za1_ref[...] = jnp.dot(xn, wa1_ref[...], preferred_element_type=F32)
    zb1_ref[...] = jnp.dot(xn, wb1_ref[...], preferred_element_type=F32)
    o_ref[...] += contribution(za0_ref, zb0_ref, cwa1_ref, cwb1_ref, cba1_ref, cbb1_ref, wd1_ref)

    @pl.when(m == nd - 1)
    def _():
        h2 = o_ref[...]
        gate = jax.nn.sigmoid(jnp.dot(_rms_rows(h2, pg_ref[...]).astype(BF16), wg_ref[...],
                                      preferred_element_type=F32))
        proj = jnp.dot(p_ref[...].astype(BF16), wp_ref[...], preferred_element_type=F32)
        o_ref[...] = h2 + gate * proj


def conv_ffn_ple(h, norm_g, w_up, conv_w, conv_b, w_down, p, ple_g, w_gate, w_proj, seq_len, tm, tf):
    t, d = h.shape
    f = w_down.shape[0]
    pd = p.shape[1]
    nf = f // tf
    assert nf % 2 == 1
    nd = (nf + 1) // 2
    hl = FFN_HALO
    tiles_per_seq = seq_len // tm
    nhb = t // hl
    last = nf - 1
    up0 = lambda m: jnp.minimum(2 * m, last)
    up1 = lambda m: jnp.minimum(2 * m + 1, last)
    pr0 = lambda m: jnp.maximum(2 * m - 1, 0)
    pr1 = lambda m: 2 * m
    wspec = lambda fn, off: pl.BlockSpec((d, tf), lambda i, m: (0, off + fn(m)))
    cspec = lambda fn, off: pl.BlockSpec((3, tf), lambda i, m: (0, off + fn(m)))
    bspec = lambda fn, off: pl.BlockSpec((1, tf), lambda i, m: (0, off + fn(m)))
    dspec = lambda fn: pl.BlockSpec((tf, d), lambda i, m: (fn(m), 0))
    cb = conv_b.reshape(1, 2 * f)
    zbuf = pltpu.VMEM((tm + 2 * hl, tf), F32)
    vmem = (2 * (2 * tm * d * 4 + 2 * hl * d * 4 + 4 * d * tf * 2 + 2 * tf * d * 2 + tm * pd * 4 + d * d * 2 + pd * d * 2)
            + (tm + 2 * hl) * d * 2 + 4 * (tm + 2 * hl) * tf * 4 + 8 * tm * tf * 4 + 2 * tm * d * 4 + (4 << 20))
    return pl.pallas_call(
        functools.partial(_ffn_ple_body, tm=tm, tiles_per_seq=tiles_per_seq, nd=nd),
        grid=(t // tm, nd),
        in_specs=[
            pl.BlockSpec((tm, d), lambda i, m: (i, 0)),
            pl.BlockSpec((hl, d), lambda i, m: (jnp.maximum(i * (tm // hl) - 1, 0), 0)),
            pl.BlockSpec((hl, d), lambda i, m: (jnp.minimum((i + 1) * (tm // hl), nhb - 1), 0)),
            pl.BlockSpec((1, d), lambda i, m: (0, 0)),
            wspec(up0, 0), wspec(up0, nf), wspec(up1, 0), wspec(up1, nf),
            cspec(pr0, 0), cspec(pr0, nf), bspec(pr0, 0), bspec(pr0, nf), dspec(pr0),
            cspec(pr1, 0), cspec(pr1, nf), bspec(pr1, 0), bspec(pr1, nf), dspec(pr1),
            pl.BlockSpec((1, d), lambda i, m: (0, 0)),
            pl.BlockSpec((d, d), lambda i, m: (0, 0)),
            pl.BlockSpec((tm, pd), lambda i, m: (i, 0)),
            pl.BlockSpec((pd, d), lambda i, m: (0, 0)),
        ],
        out_specs=pl.BlockSpec((tm, d), lambda i, m: (i, 0)),
        out_shape=jax.ShapeDtypeStruct((t, d), F32),
        scratch_shapes=[pltpu.VMEM((tm + 2 * hl, d), BF16), zbuf, zbuf, zbuf, zbuf, pltpu.VMEM((tm, tf), BF16)],
        compiler_params=_params(("parallel", "arbitrary"), vmem),
        name="conv_ffn_ple",
    )(h, h, h, norm_g.reshape(1, d),
      w_up, w_up, w_up, w_up,
      conv_w, conv_w, cb, cb, w_down,
      conv_w, conv_w, cb, cb, w_down,
      ple_g.reshape(1, d), w_gate, p, w_proj)


NA_KNORM_ROWS = 512


def _pair_head_rms(x, g, first):
    xx = x * x
    ss_a = jnp.sum(jnp.where(first, xx, 0.0), axis=-1, keepdims=True)
    ss_b = jnp.sum(jnp.where(first, 0.0, xx), axis=-1, keepdims=True)
    inv = lax.rsqrt(jnp.where(first, ss_a, ss_b) * (1.0 / NA_HEAD_DIM) + EPS)
    return x * inv * g


def _natten_body(q_ref, k_ref, v_ref, qg_ref, kg_ref, b_ref, o_ref, kn_ref, *, rows, rb):
    ib = pl.program_id(2)
    first = lax.broadcasted_iota(jnp.int32, (1, LANES), 1) < NA_HEAD_DIM
    kwin = NA_WIN_ROWS * GRID_W
    seq = rows * GRID_W

    @pl.when(ib == 0)
    def _():
        def norm_rows(c, carry):
            sl = pl.ds(pl.multiple_of(c * NA_KNORM_ROWS, NA_KNORM_ROWS), NA_KNORM_ROWS)
            kn_ref[sl, :] = _pair_head_rms(k_ref[0, sl, :].astype(F32), kg_ref[...], first).astype(BF16)
            return carry
        lax.fori_loop(0, seq // NA_KNORM_ROWS, norm_rows, 0)

    qn = _pair_head_rms(q_ref[0].astype(F32), qg_ref[...], first)

    combos = [(r, a) for r in range(rb) for a in range(2)]
    kstart, var = [], []
    for r in range(rb):
        i = ib * rb + r
        rs = jnp.clip(i - NA_WIN_ROWS // 2, 0, rows - NA_WIN_ROWS)
        var.append(i - rs)
        kstart.append(pl.multiple_of(rs * GRID_W, GRID_W))
    s = []
    for r, a in combos:
        q = qn[r * GRID_W:(r + 1) * GRID_W, :]
        keep = first if a == 0 else jnp.logical_not(first)
        qa = jnp.where(keep, q, 0.0).astype(BF16)
        k = kn_ref[pl.ds(kstart[r], kwin), :]
        s.append(lax.dot_general(qa, k, (((1,), (1,)), ((), ())), preferred_element_type=F32) + b_ref[a, var[r]])
    p, l = [], []
    for n in range(len(combos)):
        e = jnp.exp(s[n] - jnp.max(s[n], axis=-1, keepdims=True))
        l.append(jnp.sum(e, axis=-1, keepdims=True))
        p.append(e.astype(BF16))
    o = [jnp.dot(p[n], v_ref[0, pl.ds(kstart[r], kwin), :], preferred_element_type=F32) / l[n]
         for n, (r, a) in enumerate(combos)]
    for r in range(rb):
        o_ref[0, r * GRID_W:(r + 1) * GRID_W, :] = jnp.where(first, o[2 * r], o[2 * r + 1]).astype(o_ref.dtype)


def _natten_bias(rpb):
    kh, kw, w = NA_WIN_ROWS, NA_WIN_COLS, GRID_W
    var = np.arange(kh)
    jr = np.arange(kh)
    dr = jr[None, :] + (kh - 1) - var[:, None]
    sel_r = (dr[:, :, None] == np.arange(2 * kh - 1)[None, None, :]).astype(np.float32)
    col = np.arange(w)
    cstart = np.clip(col - kw // 2, 0, w - kw)
    dc = col[None, :] - col[:, None] + (kw - 1)
    valid = (col[None, :] >= cstart[:, None]) & (col[None, :] < cstart[:, None] + kw)
    sel_c = ((dc[:, :, None] == np.arange(2 * kw - 1)[None, None, :]) & valid[:, :, None]).astype(np.float32)
    b = jnp.einsum('hrd,vjr,cgd->hvcjg', rpb, jnp.asarray(sel_r), jnp.asarray(sel_c), precision=HIGHEST)
    mask = np.where(valid, 0.0, NEG_BIG).astype(np.float32)[None, None, :, None, :]
    b = b + jnp.asarray(mask)
    return b.reshape(rpb.shape[0], kh, w, kh * w)


def neighbourhood_attention(z, q_g, k_g, rpb, rb):
    bsz, seq, d3 = z.shape
    d = d3 // 3
    rows = seq // GRID_W
    bias = _natten_bias(rpb)
    hg = d // LANES
    kwin = NA_WIN_ROWS * GRID_W
    qg = jnp.tile(q_g.reshape(1, NA_HEAD_DIM), (1, 2)) * (NA_HEAD_DIM ** -0.5)
    kg = jnp.tile(k_g.reshape(1, NA_HEAD_DIM), (1, 2))
    vmem = (2 * (2 * seq * LANES * 2 + 2 * rb * GRID_W * LANES * 2 + 2 * NA_WIN_ROWS * GRID_W * kwin * 4)
            + seq * LANES * 2 + 6 * rb * GRID_W * kwin * 4 + (4 << 20))
    return pl.pallas_call(
        functools.partial(_natten_body, rows=rows, rb=rb),
        grid=(bsz, hg, rows // rb),
        in_specs=[pl.BlockSpec((1, rb * GRID_W, LANES), lambda b, h, i: (b, i, h)),
                  pl.BlockSpec((1, seq, LANES), lambda b, h, i: (b, 0, hg + h)),
                  pl.BlockSpec((1, seq, LANES), lambda b, h, i: (b, 0, 2 * hg + h)),
                  pl.BlockSpec((1, LANES), lambda b, h, i: (0, 0)),
                  pl.BlockSpec((1, LANES), lambda b, h, i: (0, 0)),
                  pl.BlockSpec((2, NA_WIN_ROWS, GRID_W, kwin), lambda b, h, i: (h, 0, 0, 0))],
        out_specs=pl.BlockSpec((1, rb * GRID_W, LANES), lambda b, h, i: (b, i, h)),
        out_shape=jax.ShapeDtypeStruct((bsz, seq, d), BF16),
        scratch_shapes=[pltpu.VMEM((seq, LANES), BF16)],
        compiler_params=_params(("parallel", "parallel", "arbitrary"), vmem),
        name="natten",
    )(z, z, z, qg, kg, bias)


def _hy_filter_body(f_ref, w1_ref, b1_ref, w2_ref, b2_ref, w3_ref, b3_ref, w4_ref, fr_ref, dl_ref,
                    h_ref, ss_ref):
    d = functools.partial(jnp.dot, preferred_element_type=F32, precision=HIGHEST)
    f = f_ref[...]
    fr = fr_ref[...]
    x = jnp.sin(fr * (d(f, w1_ref[...]) + b1_ref[...]))
    x = jnp.sin(fr * (d(x, w2_ref[...]) + b2_ref[...]))
    x = jnp.sin(fr * (d(x, w3_ref[...]) + b3_ref[...]))
    h = d(x, w4_ref[0]) * jnp.exp(-f[:, 0:1] * dl_ref[...])
    h_ref[...] = h * f[:, FILTER_EMB:FILTER_EMB + 1]

    @pl.when(pl.program_id(0) == 0)
    def _():
        ss_ref[...] = jnp.zeros_like(ss_ref)

    ss_ref[...] += jnp.sum(h * h, axis=0, keepdims=True)


def hyena_filter(seq, w1, b1, w2, b2, w3, b3, w4, freq, tl):
    c = w4.shape[1] // 2
    hid = w1.shape[1]
    t = np.linspace(0.0, 1.0, seq, dtype=np.float32)[:, None]
    bands = (FILTER_EMB - 1) // 2
    ang = (np.float32(2.0 * math.pi / seq) * np.arange(seq, dtype=np.float32)[:, None]
           * np.linspace(1e-4, bands - 1, bands, dtype=np.float32)[None])
    feats = np.concatenate([t, np.cos(ang), -np.sin(ang), np.ones((seq, 1), np.float32),
                            np.zeros((seq, FILTER_EMB_PAD - FILTER_EMB - 1), np.float32)], axis=-1)
    feats_b = np.concatenate([feats[0:1], feats[:0:-1]], axis=0)
    feats_b[0, FILTER_EMB] = 0.0
    feats2 = np.concatenate([feats, feats_b], axis=0)
    w1p = jnp.concatenate([w1, jnp.zeros((FILTER_EMB_PAD - FILTER_EMB, hid), F32)], axis=0)
    w4s = jnp.stack([w4[:, :c], w4[:, c:]])
    deltas = np.abs(np.linspace(DECAY_MIN, DECAY_MAX, c, dtype=np.float32))[None]
    full = lambda a: pl.BlockSpec(a.shape, lambda i: (0,) * a.ndim)
    nblk = seq // tl
    args = [jnp.asarray(feats2), w1p, b1.reshape(1, hid), w2, b2.reshape(1, hid), w3, b3.reshape(1, hid),
            w4s, freq.reshape(1, hid), jnp.asarray(deltas)]
    in_specs = ([pl.BlockSpec((tl, FILTER_EMB_PAD), lambda i: (i, 0))] + [full(a) for a in args[1:7]]
                + [pl.BlockSpec((1, hid, c), lambda i: (i // nblk, 0, 0)), full(args[8]), full(args[9])])
    return pl.pallas_call(
        _hy_filter_body,
        grid=(2 * nblk,),
        in_specs=in_specs,
        out_specs=[pl.BlockSpec((tl, c), lambda i: (i, 0)), pl.BlockSpec((1, c), lambda i: (0, 0))],
        out_shape=[jax.ShapeDtypeStruct((2 * seq, c), F32), jax.ShapeDtypeStruct((1, c), F32)],
        compiler_params=_params(("arbitrary",), 32 << 20),
        name="hyena_filter",
    )(*args)


def _hy_gate_body(z0_ref, z1_ref, zv_ref, w0_ref, w1_ref, wv_ref, b0_ref, b1_ref, bv_ref, s_ref, x0_ref,
                  *, seq):
    row = lax.broadcasted_iota(jnp.int32, (seq, LANES), 0)
    top = row == 0
    bot = row == seq - 1

    def conv(z_ref, w_ref, b_ref):
        z = z_ref[0]
        w = w_ref[...]
        zp = jnp.where(top, 0.0, pltpu.roll(z, 1, 0))
        zn = jnp.where(bot, 0.0, pltpu.roll(z, seq - 1, 0))
        return zp * w[0:1, :] + z * w[1:2, :] + zn * w[2:3, :] + b_ref[...]

    x0_ref[0] = conv(z0_ref, w0_ref, b0_ref)
    s_ref[0] = conv(zv_ref, wv_ref, bv_ref) * conv(z1_ref, w1_ref, b1_ref)


def hyena_gate(z, short_w, short_b, col0):
    bsz, seq, _ = z.shape
    c = HYENA_WIDTH
    nb = c // LANES
    o0 = col0 // LANES
    sb = short_b.reshape(1, 3 * c)
    zspec = lambda off: pl.BlockSpec((1, seq, LANES), lambda b, j: (b, 0, off + j))
    wspec = lambda off: pl.BlockSpec((3, LANES), lambda b, j: (0, off + j))
    bspec = lambda off: pl.BlockSpec((1, LANES), lambda b, j: (0, off + j))
    out = jax.ShapeDtypeStruct((bsz, seq, c), F32)
    return pl.pallas_call(
        functools.partial(_hy_gate_body, seq=seq),
        grid=(bsz, nb),
        in_specs=[zspec(o0), zspec(o0 + nb), zspec(o0 + 2 * nb),
                  wspec(0), wspec(nb), wspec(2 * nb), bspec(0), bspec(nb), bspec(2 * nb)],
        out_specs=[pl.BlockSpec((1, seq, LANES), lambda b, j: (b, 0, j))] * 2,
        out_shape=[out, out],
        compiler_params=_params(("parallel", "parallel"), 2 * 5 * seq * LANES * 4 + 8 * seq * LANES * 4 + (4 << 20)),
        name="hyena_gate",
    )(z, z, z, short_w, short_w, short_w, sb, sb, sb)


def _dft_consts(n, n_in):
    k = np.arange(n)[:, None].astype(np.float64)
    t = np.arange(n_in)[None, :].astype(np.float64)
    th = 2.0 * np.pi * k * t / n
    w_first = np.concatenate([np.cos(th), -np.sin(th)], axis=0)
    w_last = np.concatenate([np.cos(th).T, -np.sin(th).T], axis=1)
    tt = np.arange(n)[None, :].astype(np.float64)
    th2 = 2.0 * np.pi * k * tt / n
    cm, sm = np.cos(th2), np.sin(th2)
    m_mid = np.block([[cm, sm], [-sm, cm]])
    ph = 2.0 * np.pi * k * tt / (n * n)
    tw = np.stack([np.cos(ph), np.sin(ph)], axis=1)[..., None]
    return w_first, w_last, m_mid, tw


def _hy_stage1_body(w_ref, x_ref, o_ref, *, tb):
    w = w_ref[...]
    for j in range(tb):
        o_ref[0, j] = jnp.dot(w, x_ref[0, j], preferred_element_type=F32).astype(o_ref.dtype)


def hyena_stage(w, x, tb, out_dtype):
    bsz, n2, kk, c = x.shape
    m = w.shape[0]
    osz = jnp.dtype(out_dtype).itemsize
    vmem = 2 * (tb * kk * c * 2 + tb * m * c * osz + m * kk * 2) + 2 * m * c * 4 + (4 << 20)
    return pl.pallas_call(
        functools.partial(_hy_stage1_body, tb=tb),
        grid=(bsz, n2 // tb),
        in_specs=[pl.BlockSpec((m, kk), lambda b, j: (0, 0)),
                  pl.BlockSpec((1, tb, kk, c), lambda b, j: (b, j, 0, 0))],
        out_specs=pl.BlockSpec((1, tb, m, c), lambda b, j: (b, j, 0, 0)),
        out_shape=jax.ShapeDtypeStruct((bsz, n2, m, c), out_dtype),
        compiler_params=_params(("parallel", "parallel"), vmem),
        name="hyena_dft_outer",
    )(w, x)


def _twiddle(xr, xi, tc, ts, sign):
    return xr * tc + sign * (xi * ts), xi * tc - sign * (xr * ts)


HYENA_K1_PER_STEP = 4


def _hy_twiddled_input(a_ref, tw_ref, j, n):
    a = a_ref[0, j].astype(F32)
    xr, xi = _twiddle(a[0:n], a[n:2 * n], tw_ref[j, 0], tw_ref[j, 1], 1.0)
    return jnp.concatenate([xr, xi], axis=0).astype(BF16)


def _hy_spec_body(m_ref, tw_ref, a_ref, o_ref, *, n, kb):
    x = [_hy_twiddled_input(a_ref, tw_ref, j, n) for j in range(kb)]
    for j in range(kb):
        o_ref[0, j] = jnp.dot(m_ref[...], x[j], preferred_element_type=F32)


def _hy_mid_body(m_ref, mt_ref, tw_ref, hs_ref, a_ref, o_ref, *, n, kb):
    x = [_hy_twiddled_input(a_ref, tw_ref, j, n) for j in range(kb)]
    s = [jnp.dot(m_ref[...], x[j], preferred_element_type=F32) for j in range(kb)]
    p = []
    for j in range(kb):
        hs = hs_ref[0, j]
        sr, si = s[j][0:n], s[j][n:2 * n]
        hr, hi = hs[0:n], hs[n:2 * n]
        p.append(jnp.concatenate([sr * hr - si * hi, sr * hi + si * hr], axis=0).astype(BF16))
    y = [jnp.dot(mt_ref[...], p[j], preferred_element_type=F32) for j in range(kb)]
    for j in range(kb):
        yr, yi = _twiddle(y[j][0:n], y[j][n:2 * n], tw_ref[j, 0], tw_ref[j, 1], -1.0)
        o_ref[0, j] = jnp.concatenate([yr, yi], axis=0).astype(o_ref.dtype)


def hyena_spectrum(a, m_mid, tw):
    _, n, n2x, c = a.shape
    kb = min(HYENA_K1_PER_STEP, n)
    return pl.pallas_call(
        functools.partial(_hy_spec_body, n=n, kb=kb),
        grid=(n // kb, 1),
        in_specs=[pl.BlockSpec((n2x, n2x), lambda k, b: (0, 0)),
                  pl.BlockSpec((kb, 2, n, 1), lambda k, b: (k, 0, 0, 0)),
                  pl.BlockSpec((1, kb, n2x, c), lambda k, b: (b, k, 0, 0))],
        out_specs=pl.BlockSpec((1, kb, n2x, c), lambda k, b: (b, k, 0, 0)),
        out_shape=jax.ShapeDtypeStruct((1, n, n2x, c), F32),
        compiler_params=_params(("parallel", "parallel"), 40 << 20),
        name="hyena_filter_spectrum",
    )(m_mid, tw, a)


def hyena_mid(a, hspec, m_mid, m_mid_t, tw):
    bsz, n, n2x, c = a.shape
    kb = min(HYENA_K1_PER_STEP, n)
    return pl.pallas_call(
        functools.partial(_hy_mid_body, n=n, kb=kb),
        grid=(n // kb, bsz),
        in_specs=[pl.BlockSpec((n2x, n2x), lambda k, b: (0, 0)),
                  pl.BlockSpec((n2x, n2x), lambda k, b: (0, 0)),
                  pl.BlockSpec((kb, 2, n, 1), lambda k, b: (k, 0, 0, 0)),
                  pl.BlockSpec((1, kb, n2x, c), lambda k, b: (0, k, 0, 0)),
                  pl.BlockSpec((1, kb, n2x, c), lambda k, b: (b, k, 0, 0))],
        out_specs=pl.BlockSpec((1, kb, n2x, c), lambda k, b: (b, k, 0, 0)),
        out_shape=jax.ShapeDtypeStruct((bsz, n, n2x, c), BF16),
        compiler_params=_params(("parallel", "parallel"), 40 << 20),
        name="hyena_dft_mid",
    )(m_mid, m_mid_t, tw, hspec, a)


def _hy_combine_body(y_ref, s_ref, x0_ref, sc_ref, bi_ref, o_ref):
    o_ref[...] = ((y_ref[...] * sc_ref[...] + s_ref[...] * bi_ref[...]) * x0_ref[...]).astype(o_ref.dtype)


def hyena_combine(y, s, x0, scale, bias, tm):
    t, c = y.shape
    row = lambda: pl.BlockSpec((tm, c), lambda i: (i, 0))
    vec = lambda: pl.BlockSpec((1, c), lambda i: (0, 0))
    return pl.pallas_call(
        _hy_combine_body,
        grid=(t // tm,),
        in_specs=[row(), row(), row(), vec(), vec()],
        out_specs=row(),
        out_shape=jax.ShapeDtypeStruct((t, c), BF16),
        compiler_params=_params(("parallel",), 2 * 4 * tm * c * 4 + (4 << 20)),
        name="hyena_combine",
    )(y, s, x0, scale.reshape(1, c), bias.reshape(1, c))


def hyena_bidir(z, col0, short_w, short_b, f_w1, f_b1, f_w2, f_b2, f_w3, f_b3, f_w4, f_freq, bias):
    bsz, seq, _ = z.shape
    c = HYENA_WIDTH
    n = int(round(math.sqrt(2 * seq)))
    assert n * n == 2 * seq and n % 16 == 0
    nh = n // 2
    tb = 8

    w_first, w_last, m_mid, tw = _dft_consts(n, n)
    w_first_x = jnp.asarray(w_first[:, :nh], BF16)
    w_first_h = jnp.asarray(w_first, BF16)
    w_last_y = jnp.asarray(w_last[:nh], BF16)
    m_mid_j = jnp.asarray(m_mid, BF16)
    m_mid_t = jnp.asarray(m_mid.T, BF16)
    tw_j = jnp.asarray(tw, F32)

    kern, ss = hyena_filter(seq, f_w1, f_b1, f_w2, f_b2, f_w3, f_b3, f_w4, f_freq, tl=min(seq, 1024))
    scale = lax.rsqrt(ss[0] + EPS) * (1.0 / (2 * seq))
    kern_t = jnp.transpose(kern.reshape(1, n, n, c), (0, 2, 1, 3)).astype(BF16)
    ah = hyena_stage(w_first_h, kern_t, tb, BF16)
    ah = jnp.transpose(ah.reshape(1, n, 2, n, c), (0, 3, 2, 1, 4)).reshape(1, n, 2 * n, c)
    hspec = hyena_spectrum(ah, m_mid_j, tw_j)

    s, x0 = hyena_gate(z, short_w, short_b, col0)
    s_t = jnp.transpose(s.reshape(bsz, nh, n, c), (0, 2, 1, 3)).astype(BF16)
    a = hyena_stage(w_first_x, s_t, tb, BF16)
    a = jnp.transpose(a.reshape(bsz, n, 2, n, c), (0, 3, 2, 1, 4)).reshape(bsz, n, 2 * n, c)
    bm = hyena_mid(a, hspec, m_mid_j, m_mid_t, tw_j)
    bm = jnp.transpose(bm.reshape(bsz, n, 2, n, c), (0, 3, 2, 1, 4)).reshape(bsz, n, 2 * n, c)
    y = hyena_stage(w_last_y, bm, tb, F32)
    y = jnp.transpose(y, (0, 2, 1, 3)).reshape(bsz * seq, c)
    return hyena_combine(y, s.reshape(bsz * seq, c), x0.reshape(bsz * seq, c), scale, bias,
                         tm=min(bsz * seq, 2048))


RWKV_PAIRS = RWKV_WIDTH // LANES
RWKV_PREP_CHUNKS = 4


def _pair_rows(x, first):
    return jnp.concatenate([jnp.where(first, x, 0.0), jnp.where(first, 0.0, x)], axis=0)


def _fold_rows(x2, ch):
    return x2[0:ch] + x2[ch:2 * ch]


def _rwkv_prep_body(z_ref, halo_ref, mu_ref, w0_ref, wup_ref, a0_ref, aup_ref, kk_ref, ka_ref, rk_ref, sel_ref,
                    wt_ref, rt_ref, kb_ref, ab_ref, v_ref, ut_ref, yk_ref, ara_ref, ge_ref, bonus_ref,
                    *, reverse, nsteps, cps):
    ch = RWKV_CHUNK
    cw = RWKV_WIDTH
    rows = cps * ch
    c_idx = pl.program_id(1)

    zs = z_ref[0]
    row = lax.broadcasted_iota(jnp.int32, (rows, 1), 0)
    if reverse:
        nbr = jnp.where(c_idx == nsteps - 1, 0.0, halo_ref[0, 0:1, :])
        shifted = jnp.where(row == rows - 1, jnp.broadcast_to(nbr, zs.shape), pltpu.roll(zs, rows - 1, 0))
    else:
        nbr = jnp.where(c_idx == 0, 0.0, halo_ref[0, SUBLANES - 1:SUBLANES, :])
        shifted = jnp.where(row == 0, jnp.broadcast_to(nbr, zs.shape), pltpu.roll(zs, 1, 0))
    zd = zs + (shifted - zs) * mu_ref[...]

    r = zd[:, 0:cw]
    k = zd[:, cw:2 * cw]
    v = zd[:, 2 * cw:3 * cw]
    lora = zd[:, 3 * cw:3 * cw + DECAY_RANK + ICLR_RANK]
    x = w0_ref[...] + _bdot(jnp.tanh(lora), wup_ref[...])
    log_w = jnp.minimum(x, 0.0) - jnp.log(1.0 + jnp.exp(-jnp.abs(x))) - 0.5
    lnw = -jnp.exp(log_w)
    a = jax.nn.sigmoid(a0_ref[...] + _bdot(lora, aup_ref[...]))
    sel = sel_ref[...]
    kkr = k * kk_ref[...]
    kkn = jnp.sqrt(_bdot(kkr * kkr, sel))
    kk = kkr / jnp.maximum(kkn, 1e-12)
    k2 = k * (1.0 + (a - 1.0) * ka_ref[...])
    ah = kk * a
    bonus_ref[0] = _bdot(r * k2 * rk_ref[...], sel) * v

    ri = lax.broadcasted_iota(jnp.int32, (rows, rows), 0)
    ci = lax.broadcasted_iota(jnp.int32, (rows, rows), 1)
    csh = ch.bit_length() - 1
    tri = ((ci >= ri) if reverse else (ci <= ri)) & ((ri >> csh) == (ci >> csh))
    cs = _dot_exact_lhs(tri.astype(BF16), lnw)
    total = jnp.concatenate(
        [jnp.broadcast_to(jnp.sum(lnw[g * ch:(g + 1) * ch], axis=0, keepdims=True), (ch, cw)) for g in range(cps)],
        axis=0)
    e_inv = jnp.exp(-cs)
    e_end = jnp.exp(total - cs)
    rt = r * jnp.exp(cs)
    kkt = kk * jnp.exp(cs - lnw)
    kh = k2 * e_inv
    ahh = ah * e_inv
    rt_ref[0] = rt.astype(rt_ref.dtype)
    kb_ref[0] = (k2 * e_end).astype(kb_ref.dtype)
    ab_ref[0] = (ah * e_end).astype(ab_ref.dtype)
    v_ref[0] = v.astype(v_ref.dtype)
    for g in range(cps):
        ge_ref[0, g] = jnp.exp(total[g * ch:g * ch + 1])

    r2 = lax.broadcasted_iota(jnp.int32, (2 * ch, 2 * ch), 0)
    c2 = lax.broadcasted_iota(jnp.int32, (2 * ch, 2 * ch), 1)
    same_head = (r2 < ch) == (c2 < ch)
    si = r2 & (ch - 1)
    sj = c2 & (ch - 1)
    before = (sj > si) if reverse else (sj < si)
    strict = same_head & before
    incl = same_head & jnp.logical_or(before, si == sj)
    eye = (r2 == c2).astype(F32)
    first = lax.broadcasted_iota(jnp.int32, (1, LANES), 1) < RWKV_HEAD_DIM

    units = [(g, p) for g in range(cps) for p in range(RWKV_PAIRS)]
    un = range(len(units))
    tile = lambda arr, g, p: arr[g * ch:(g + 1) * ch, p * LANES:(p + 1) * LANES]
    l2 = [_pair_rows(tile(kkt, g, p), first) for g, p in units]
    l4 = [jnp.concatenate([l2[n], _pair_rows(tile(rt, g, p), first)], axis=0) for n, (g, p) in enumerate(units)]
    v2 = [_pair_rows(tile(v, g, p), first) for g, p in units]
    sk = [_bdot_nt(l4[n], jnp.concatenate([tile(kh, g, p)] * 2, axis=0)) for n, (g, p) in enumerate(units)]
    sa = [_bdot_nt(l4[n], jnp.concatenate([tile(ahh, g, p)] * 2, axis=0)) for n, (g, p) in enumerate(units)]
    nm = [jnp.where(strict, sa[n][0:2 * ch], 0.0) for n in un]
    for n, (g, p) in enumerate(units):
        ara_ref[0, g, p] = jnp.where(incl, sa[n][2 * ch:4 * ch], 0.0).astype(ara_ref.dtype)
    akr = [jnp.concatenate([jnp.where(strict, sk[n][0:2 * ch], 0.0),
                            jnp.where(incl, sk[n][2 * ch:4 * ch], 0.0)], axis=0) for n in un]
    av = [_bdot(akr[n], v2[n]) for n in un]

    lvl = (si >> 1) == (sj >> 1)
    t = [eye - jnp.where(lvl, nm[n], 0.0) for n in un]
    s = 2
    while s < ch:
        sh = s.bit_length() - 1
        lvl = ((si >> (sh + 1)) == (sj >> (sh + 1))) & ((si >> sh) != (sj >> sh))
        x = [_bdot(jnp.where(lvl, nm[n], 0.0), t[n]) for n in un]
        t = [t[n] - _bdot(t[n], x[n]) for n in un]
        s *= 2

    wu = [_bdot(t[n], jnp.concatenate([l2[n], av[n][0:2 * ch]], axis=1)) for n in un]
    gather = lambda f: jnp.concatenate(
        [jnp.concatenate([f(g * RWKV_PAIRS + p) for p in range(RWKV_PAIRS)], axis=1) for g in range(cps)], axis=0)
    wt_ref[0] = gather(lambda n: _fold_rows(wu[n][:, 0:LANES], ch)).astype(wt_ref.dtype)
    ut_ref[0] = gather(lambda n: _fold_rows(wu[n][:, LANES:2 * LANES], ch))
    yk_ref[0] = gather(lambda n: _fold_rows(av[n][2 * ch:4 * ch], ch))


def rwkv7_prep(z, mu, w0, w_up, a0, a_up, k_k, k_a, r_k, reverse):
    bsz, seq, _ = z.shape
    ch = RWKV_CHUNK
    cw = RWKV_WIDTH
    nc = seq // ch
    cps = min(RWKV_PREP_CHUNKS, nc)
    nsteps = nc // cps
    cpb = cps * ch // SUBLANES
    nhb = seq // SUBLANES
    sel_np = (np.arange(cw)[:, None] // RWKV_HEAD_DIM == np.arange(cw)[None, :] // RWKV_HEAD_DIM)
    sel = jnp.asarray(sel_np, BF16)
    zero_pad = jnp.zeros((DECAY_RANK, cw), F32)
    wup = jnp.concatenate([w_up, zero_pad], axis=0).astype(BF16)
    aup = jnp.concatenate([zero_pad, a_up], axis=0).astype(BF16)
    vec = lambda a: a.reshape(1, -1)
    params = [vec(mu), vec(w0), wup, vec(a0), aup, vec(k_k), vec(k_a), vec(r_k), sel]
    full = lambda a: pl.BlockSpec(a.shape, lambda b, c: (0,) * a.ndim)
    if reverse:
        halo = pl.BlockSpec((1, SUBLANES, RWKV_SHIFT_COLS), lambda b, c: (b, jnp.minimum((c + 1) * cpb, nhb - 1), 0))
    else:
        halo = pl.BlockSpec((1, SUBLANES, RWKV_SHIFT_COLS), lambda b, c: (b, jnp.maximum(c * cpb - 1, 0), 0))
    tok = pl.BlockSpec((1, cps * ch, cw), lambda b, c: (b, c, 0))
    tok_bf = jax.ShapeDtypeStruct((bsz, seq, cw), BF16)
    tok_f = jax.ShapeDtypeStruct((bsz, seq, cw), F32)
    return pl.pallas_call(
        functools.partial(_rwkv_prep_body, reverse=reverse, nsteps=nsteps, cps=cps),
        grid=(bsz, nsteps),
        in_specs=[pl.BlockSpec((1, cps * ch, RWKV_SHIFT_COLS), lambda b, c: (b, c, 0)), halo]
                 + [full(a) for a in params],
        out_specs=[tok] * 7
                  + [pl.BlockSpec((1, cps, RWKV_PAIRS, 2 * ch, 2 * ch), lambda b, c: (b, c, 0, 0, 0)),
                     pl.BlockSpec((1, cps, 1, cw), lambda b, c: (b, c, 0, 0)),
                     tok],
        out_shape=[tok_bf, tok_bf, tok_bf, tok_bf, tok_bf, tok_f, tok_f,
                   jax.ShapeDtypeStruct((bsz, nc, RWKV_PAIRS, 2 * ch, 2 * ch), BF16),
                   jax.ShapeDtypeStruct((bsz, nc, 1, cw), F32),
                   tok_f],
        compiler_params=_params(("parallel", "parallel"), 40 << 20),
        name="rwkv7_prep_bwd" if reverse else "rwkv7_prep_fwd",
    )(z, z, *params)


def _rwkv_scan_body(*refs, bsz, final):
    if final:
        (wt_ref, rt_ref, kb_ref, ab_ref, v_ref, ut_ref, yk_ref, ara_ref, ge_ref,
         yin_ref, b1_ref, b2_ref, gd_ref, gup_ref, lnw_ref, lnb_ref, sel_ref, o_ref, st_ref) = refs
    else:
        (wt_ref, rt_ref, kb_ref, ab_ref, v_ref, ut_ref, yk_ref, ara_ref, ge_ref, o_ref, st_ref) = refs
    ch = RWKV_CHUNK

    @pl.when(pl.program_id(0) == 0)
    def _():
        st_ref[...] = jnp.zeros_like(st_ref)

    first = lax.broadcasted_iota(jnp.int32, (1, LANES), 1) < RWKV_HEAD_DIM
    blk = ((lax.broadcasted_iota(jnp.int32, (LANES, LANES), 0) < RWKV_HEAD_DIM)
           == (lax.broadcasted_iota(jnp.int32, (LANES, LANES), 1) < RWKV_HEAD_DIM))
    idx = [(b, p) for b in range(bsz) for p in range(RWKV_PAIRS)]
    sl = [slice(p * LANES, (p + 1) * LANES) for p in range(RWKV_PAIRS)]
    st = [st_ref[b * RWKV_PAIRS + p] for b, p in idx]
    lm0 = [_bdot_nt(jnp.concatenate([wt_ref[b, :, sl[p]], rt_ref[b, :, sl[p]]], axis=0), st[n])
           for n, (b, p) in enumerate(idx)]
    u = [lm0[n][0:ch] + ut_ref[b, :, sl[p]] for n, (b, p) in enumerate(idx)]
    yr = [_bdot(ara_ref[b, 0, p], _pair_rows(u[n], first)) for n, (b, p) in enumerate(idx)]
    upd = [_bdot_tn(jnp.concatenate([v_ref[b, :, sl[p]].astype(F32), -u[n]], axis=0),
                    jnp.concatenate([kb_ref[b, :, sl[p]], ab_ref[b, :, sl[p]]], axis=0))
           for n, (b, p) in enumerate(idx)]
    for n, (b, p) in enumerate(idx):
        st_ref[b * RWKV_PAIRS + p] = st[n] * ge_ref[b, 0, :, sl[p]] + jnp.where(blk, upd[n], 0.0)
    for b in range(bsz):
        y = jnp.concatenate(
            [lm0[b * RWKV_PAIRS + p][ch:2 * ch] + yk_ref[b, :, sl[p]] - _fold_rows(yr[b * RWKV_PAIRS + p], ch)
             for p in range(RWKV_PAIRS)], axis=1)
        if not final:
            o_ref[b] = y
        else:
            sel = sel_ref[...]
            yt = y + yin_ref[b]
            mean = _bdot(yt, sel) * (1.0 / RWKV_HEAD_DIM)
            dv = yt - mean
            var = _bdot(dv * dv, sel) * (1.0 / RWKV_HEAD_DIM)
            yn = dv * lax.rsqrt(var + RWKV_GN_EPS) * lnw_ref[...] + lnb_ref[...]
            g = _bdot(jax.nn.sigmoid(gd_ref[b]), gup_ref[...])
            o_ref[b] = ((yn + b1_ref[b] + b2_ref[b]) * g).astype(o_ref.dtype)


def rwkv7_scan(prep, reverse, final_args=None):
    wt, rt, kb, ab, v, ut, yk, ara, ge, _ = prep
    bsz, seq, cw = wt.shape
    ch = RWKV_CHUNK
    nc = seq // ch
    cidx = (lambda c: nc - 1 - c) if reverse else (lambda c: c)
    tok = pl.BlockSpec((bsz, ch, cw), lambda c: (0, cidx(c), 0))
    in_specs = [tok] * 7 + [pl.BlockSpec((bsz, 1, RWKV_PAIRS, 2 * ch, 2 * ch), lambda c: (0, cidx(c), 0, 0, 0)),
                            pl.BlockSpec((bsz, 1, 1, cw), lambda c: (0, cidx(c), 0, 0))]
    args = [wt, rt, kb, ab, v, ut, yk, ara, ge]
    final = final_args is not None
    if final:
        z, y_in, bonus_a, bonus_b, g_up, ln_w, ln_b = final_args
        sel_np = (np.arange(cw)[:, None] // RWKV_HEAD_DIM == np.arange(cw)[None, :] // RWKV_HEAD_DIM)
        full = lambda a: pl.BlockSpec(a.shape, lambda c: (0,) * a.ndim)
        extra = [g_up.astype(BF16), ln_w.reshape(1, cw), ln_b.reshape(1, cw), jnp.asarray(sel_np, BF16)]
        in_specs += [tok, tok, tok,
                     pl.BlockSpec((bsz, ch, GATE_RANK), lambda c: (0, cidx(c), RWKV_SHIFT_COLS // GATE_RANK))]
        in_specs += [full(a) for a in extra]
        args += [y_in, bonus_a, bonus_b, z] + extra
    out_dtype = BF16 if final else F32
    return pl.pallas_call(
        functools.partial(_rwkv_scan_body, bsz=bsz, final=final),
        grid=(nc,),
        in_specs=in_specs,
        out_specs=tok,
        out_shape=jax.ShapeDtypeStruct((bsz, seq, cw), out_dtype),
        scratch_shapes=[pltpu.VMEM((bsz * RWKV_PAIRS, LANES, LANES), F32)],
        compiler_params=_params(("arbitrary",), 40 << 20),
        name="rwkv7_scan_bwd" if reverse else "rwkv7_scan_fwd",
    )(*args)


def rwkv7_bidir(z, mu, w0, w_up, a0, a_up, g_up, k_k, k_a, r_k, ln_w, ln_b):
    rk = r_k.reshape(-1)
    prep_f = rwkv7_prep(z, mu[0], w0[0], w_up[0], a0[0], a_up[0], k_k, k_a, rk, reverse=False)
    prep_b = rwkv7_prep(z, mu[1], w0[1], w_up[1], a0[1], a_up[1], k_k, k_a, rk, reverse=True)
    y_f = rwkv7_scan(prep_f, reverse=False)
    return rwkv7_scan(prep_b, reverse=True, final_args=(z, y_f, prep_f[9], prep_b[9], g_up, ln_w, ln_b))


def kernel(x, p, mix_norm, mix_w_in, rwkv_mu, rwkv_w0, rwkv_w_up, rwkv_a0, rwkv_a_up, rwkv_g_up, rwkv_k_k, rwkv_k_a, rwkv_r_k, rwkv_ln_w, rwkv_ln_b, hy_short_w, hy_short_b, hy_f_w1, hy_f_b1, hy_f_w2, hy_f_b2, hy_f_w3, hy_f_b3, hy_f_w4, hy_f_freq, hy_bias, mix_w_out, na_norm, na_w_qkv, na_q_g, na_k_g, na_rpb, na_w_out, ffn_norm, ffn_w_up, ffn_conv_w, ffn_conv_b, ffn_w_down, ple_norm, ple_w_gate, ple_w_proj):
    bsz, seq, d = x.shape
    depth = p.shape[0]
    t = bsz * seq
    tm = min(1024, seq)
    h = x.reshape(t, d)
    for i in range(depth):
        j = i // 2
        if i % 2 == 0:
            w_in = mix_w_in[j].astype(BF16)
            z = norm_matmul(h, mix_norm[j], w_in, tm=tm, tn=w_in.shape[1] // 2).reshape(bsz, seq, -1)
            y_a = rwkv7_bidir(z, rwkv_mu[j], rwkv_w0[j], rwkv_w_up[j], rwkv_a0[j], rwkv_a_up[j], rwkv_g_up[j],
                              rwkv_k_k[j], rwkv_k_a[j], rwkv_r_k[j], rwkv_ln_w[j], rwkv_ln_b[j])
            y_b = hyena_bidir(z, RWKV_COLS, hy_short_w[j], hy_short_b[j], hy_f_w1[j], hy_f_b1[j], hy_f_w2[j],
                              hy_f_b2[j], hy_f_w3[j], hy_f_b3[j], hy_f_w4[j], hy_f_freq[j], hy_bias[j])
            w_out = mix_w_out[j].astype(BF16)
            h = matmul_residual([y_a.reshape(t, -1), y_b], [w_out[:RWKV_WIDTH], w_out[RWKV_WIDTH:]], h,
                                tm=tm, tn=512)
        else:
            z = norm_matmul(h, na_norm[j], na_w_qkv[j].astype(BF16), tm=tm, tn=1024, out_dtype=BF16)
            o = neighbourhood_attention(z.reshape(bsz, seq, 3 * d), na_q_g[j], na_k_g[j], na_rpb[j], rb=16)
            h = matmul_residual([o.reshape(t, d)], [na_w_out[j].astype(BF16)], h, tm=tm, tn=512)
        h = conv_ffn_ple(h, ffn_norm[i], ffn_w_up[i].astype(BF16), ffn_conv_w[i], ffn_conv_b[i],
                         ffn_w_down[i].astype(BF16), p[i].reshape(t, -1), ple_norm[i],
                         ple_w_gate[i].astype(BF16), ple_w_proj[i].astype(BF16), seq_len=seq, tm=tm, tf=256)
    return h.reshape(bsz, seq, d)
```

```python
import functools
import math

import numpy as np
import jax
import jax.numpy as jnp
from jax import lax
from jax.experimental import pallas as pl
from jax.experimental.pallas import tpu as pltpu

F32 = jnp.float32
BF16 = jnp.bfloat16
HIGHEST = lax.Precision.HIGHEST

EPS = 1e-6
GRID_W = 64
RWKV_HEADS = 8
RWKV_HEAD_DIM = 64
RWKV_WIDTH = RWKV_HEADS * RWKV_HEAD_DIM
DECAY_RANK = 64
ICLR_RANK = 64
GATE_RANK = 128
RWKV_SHIFT_COLS = 3 * RWKV_WIDTH + DECAY_RANK + ICLR_RANK
RWKV_COLS = RWKV_SHIFT_COLS + GATE_RANK
RWKV_GN_EPS = 64e-5
RWKV_CHUNK = 64
HYENA_WIDTH = 512
FILTER_EMB = 17
FILTER_EMB_PAD = 32
DECAY_MIN = math.log(1e-2) / 1.5
DECAY_MAX = math.log(1e-2) / 0.3
NA_HEADS = 16
NA_HEAD_DIM = 64
NA_WIN_ROWS = 8
NA_WIN_COLS = 16
NEG_BIG = -1e30

LANES = 128
SUBLANES = 8
V7X_VMEM_BYTES = 64 * 1024 * 1024
VMEM_CAP = V7X_VMEM_BYTES - 8 * 1024 * 1024


def _params(semantics, vmem_bytes):
    return pltpu.CompilerParams(dimension_semantics=semantics,
                                vmem_limit_bytes=int(min(max(vmem_bytes, 16 * 1024 * 1024), VMEM_CAP)))


def _rms_rows(x, g):
    ms = jnp.mean(x * x, axis=-1, keepdims=True)
    return x * lax.rsqrt(ms + EPS) * g


def _bdot(a, b):
    return jnp.dot(a.astype(BF16), b.astype(BF16), preferred_element_type=F32)


def _bdot_nt(a, b):
    return lax.dot_general(a.astype(BF16), b.astype(BF16), (((1,), (1,)), ((), ())),
                           preferred_element_type=F32)


def _bdot_tn(a, b):
    return lax.dot_general(a.astype(BF16), b.astype(BF16), (((0,), (0,)), ((), ())),
                           preferred_element_type=F32)


def _dot_exact_lhs(a_bf16, b):
    d = functools.partial(jnp.dot, preferred_element_type=F32)
    b1 = b.astype(BF16)
    r1 = b - b1.astype(F32)
    b2 = r1.astype(BF16)
    b3 = (r1 - b2.astype(F32)).astype(BF16)
    return d(a_bf16, b1) + (d(a_bf16, b2) + d(a_bf16, b3))


def _norm_matmul_body(x_ref, g_ref, w_ref, o_ref, xn_ref):
    @pl.when(pl.program_id(1) == 0)
    def _():
        xn_ref[...] = _rms_rows(x_ref[...], g_ref[...]).astype(BF16)

    o_ref[...] = jnp.dot(xn_ref[...], w_ref[...], preferred_element_type=F32).astype(o_ref.dtype)


def norm_matmul(x, g, w, tm, tn, out_dtype=F32):
    t, d = x.shape
    n = w.shape[1]
    osz = jnp.dtype(out_dtype).itemsize
    vmem = 2 * (tm * d * 4 + d * tn * 2 + tm * tn * osz) + tm * d * 2 + (4 << 20)
    return pl.pallas_call(
        _norm_matmul_body,
        grid=(t // tm, n // tn),
        in_specs=[pl.BlockSpec((tm, d), lambda i, j: (i, 0)),
                  pl.BlockSpec((1, d), lambda i, j: (0, 0)),
                  pl.BlockSpec((d, tn), lambda i, j: (0, j))],
        out_specs=pl.BlockSpec((tm, tn), lambda i, j: (i, j)),
        out_shape=jax.ShapeDtypeStruct((t, n), out_dtype),
        scratch_shapes=[pltpu.VMEM((tm, d), BF16)],
        compiler_params=_params(("parallel", "arbitrary"), vmem),
        name="norm_matmul",
    )(x, g.reshape(1, d), w)


FFN_HALO = 16
FFN_ROW_BLOCK = 64


def _ffn_ple_body(*refs, tm, tiles_per_seq, nd, n_in):
    xm_ref, xp_ref, xx_ref, g_ref = refs[0:4]
    ym, yp, yx, wo = (refs[4 + k * n_in:4 + (k + 1) * n_in] for k in range(4))
    (wa0_ref, wb0_ref, wa1_ref, wb1_ref,
     cwa0_ref, cwb0_ref, cba0_ref, cbb0_ref, wd0_ref,
     cwa1_ref, cwb1_ref, cba1_ref, cbb1_ref, wd1_ref,
     pg_ref, wg_ref, p_ref, wp_ref,
     o_ref, xn_ref, za0_ref, zb0_ref, za1_ref, zb1_ref, act_ref) = refs[4 + 4 * n_in:]
    i = pl.program_id(0)
    m = pl.program_id(1)
    hl = FFN_HALO

    @pl.when(m == 0)
    def _():
        def mixed(x_ref, ys):
            acc = x_ref[...]
            for y_ref, w_ref in zip(ys, wo):
                acc = acc + jnp.dot(y_ref[...], w_ref[...], preferred_element_type=F32)
            return acc

        g = g_ref[...]
        first = (i % tiles_per_seq) == 0
        last = (i % tiles_per_seq) == tiles_per_seq - 1
        xm = mixed(xm_ref, ym)
        xn_ref[hl:hl + tm, :] = _rms_rows(xm, g).astype(BF16)
        xn_ref[0:hl, :] = jnp.where(first, 0.0, _rms_rows(mixed(xp_ref, yp), g)).astype(BF16)
        xn_ref[hl + tm:2 * hl + tm, :] = jnp.where(last, 0.0, _rms_rows(mixed(xx_ref, yx), g)).astype(BF16)
        o_ref[...] = xm
        za1_ref[...] = jnp.zeros_like(za1_ref)
        zb1_ref[...] = jnp.zeros_like(zb1_ref)

    def conv(z_ref, r0, cw, cb):
        return (z_ref[r0 - 1:r0 - 1 + FFN_ROW_BLOCK, :] * cw[0:1, :] + z_ref[r0:r0 + FFN_ROW_BLOCK, :] * cw[1:2, :]
                + z_ref[r0 + 1:r0 + 1 + FFN_ROW_BLOCK, :] * cw[2:3, :] + cb)

    def contribution(za_ref, zb_ref, cwa_ref, cwb_ref, cba_ref, cbb_ref, wd_ref):
        cwa, cwb, cba, cbb = cwa_ref[...], cwb_ref[...], cba_ref[...], cbb_ref[...]
        for blk in range(tm // FFN_ROW_BLOCK):
            r0 = hl + blk * FFN_ROW_BLOCK
            act = jax.nn.gelu(conv(za_ref, r0, cwa, cba)) * conv(zb_ref, r0, cwb, cbb)
            act_ref[blk * FFN_ROW_BLOCK:(blk + 1) * FFN_ROW_BLOCK, :] = act.astype(BF16)
        return jnp.dot(act_ref[...], wd_ref[...], preferred_element_type=F32)

    xn = xn_ref[...]
    za0_ref[...] = jnp.dot(xn, wa0_ref[...], preferred_element_type=F32)
    zb0_ref[...] = jnp.dot(xn, wb0_ref[...], preferred_element_type=F32)
    ca = contribution(za1_ref, zb1_ref, cwa0_ref, cwb0_ref, cba0_ref, cbb0_ref, wd0_ref)
    o_ref[...] += jnp.where(m > 0, ca, 0.0)
    za1_ref[...] = jnp.dot(xn, wa1_ref[...], preferred_element_type=F32)
    zb1_ref[...] = jnp.dot(xn, wb1_ref[...], preferred_element_type=F32)
    o_ref[...] += contribution(za0_ref, zb0_ref, cwa1_ref, cwb1_ref, cba1_ref, cbb1_ref, wd1_ref)

    @pl.when(m == nd - 1)
    def _():
        h2 = o_ref[...]
        gate = jax.nn.sigmoid(jnp.dot(_rms_rows(h2, pg_ref[...]).astype(BF16), wg_ref[...],
                                      preferred_element_type=F32))
        proj = jnp.dot(p_ref[...].astype(BF16), wp_ref[...], preferred_element_type=F32)
        o_ref[...] = h2 + gate * proj


def conv_ffn_ple(h, ys, wos, norm_g, w_up, conv_w, conv_b, w_down, p, ple_g, w_gate, w_proj, seq_len, tm, tf):
    t, d = h.shape
    n_in = len(ys)
    f = w_down.shape[0]
    pd = p.shape[1]
    nf = f // tf
    assert nf % 2 == 1
    nd = (nf + 1) // 2
    hl = FFN_HALO
    tiles_per_seq = seq_len // tm
    nhb = t // hl
    last = nf - 1
    up0 = lambda m: jnp.minimum(2 * m, last)
    up1 = lambda m: jnp.minimum(2 * m + 1, last)
    pr0 = lambda m: jnp.maximum(2 * m - 1, 0)
    pr1 = lambda m: 2 * m
    wspec = lambda fn, off: pl.BlockSpec((d, tf), lambda i, m: (0, off + fn(m)))
    cspec = lambda fn, off: pl.BlockSpec((3, tf), lambda i, m: (0, off + fn(m)))
    bspec = lambda fn, off: pl.BlockSpec((1, tf), lambda i, m: (0, off + fn(m)))
    dspec = lambda fn: pl.BlockSpec((tf, d), lambda i, m: (fn(m), 0))
    cb = conv_b.reshape(1, 2 * f)
    zbuf = pltpu.VMEM((tm + 2 * hl, tf), F32)
    prev_blk = lambda i, m: (jnp.maximum(i * (tm // hl) - 1, 0), 0)
    next_blk = lambda i, m: (jnp.minimum((i + 1) * (tm // hl), nhb - 1), 0)
    ksum = sum(y.shape[1] for y in ys)
    vmem = (2 * (2 * tm * d * 4 + 2 * hl * d * 4 + 4 * d * tf * 2 + 2 * tf * d * 2 + tm * pd * 4 + d * d * 2 + pd * d * 2
                 + (tm + 2 * hl) * ksum * 2 + ksum * d * 2)
            + (tm + 2 * hl) * d * 2 + 4 * (tm + 2 * hl) * tf * 4 + 8 * tm * tf * 4 + 2 * tm * d * 4 + (4 << 20))
    return pl.pallas_call(
        functools.partial(_ffn_ple_body, tm=tm, tiles_per_seq=tiles_per_seq, nd=nd, n_in=n_in),
        grid=(t // tm, nd),
        in_specs=[
            pl.BlockSpec((tm, d), lambda i, m: (i, 0)),
            pl.BlockSpec((hl, d), prev_blk),
            pl.BlockSpec((hl, d), next_blk),
            pl.BlockSpec((1, d), lambda i, m: (0, 0)),
            *[pl.BlockSpec((tm, y.shape[1]), lambda i, m: (i, 0)) for y in ys],
            *[pl.BlockSpec((hl, y.shape[1]), prev_blk) for y in ys],
            *[pl.BlockSpec((hl, y.shape[1]), next_blk) for y in ys],
            *[pl.BlockSpec(w.shape, lambda i, m: (0, 0)) for w in wos],
            wspec(up0, 0), wspec(up0, nf), wspec(up1, 0), wspec(up1, nf),
            cspec(pr0, 0), cspec(pr0, nf), bspec(pr0, 0), bspec(pr0, nf), dspec(pr0),
            cspec(pr1, 0), cspec(pr1, nf), bspec(pr1, 0), bspec(pr1, nf), dspec(pr1),
            pl.BlockSpec((1, d), lambda i, m: (0, 0)),
            pl.BlockSpec((d, d), lambda i, m: (0, 0)),
            pl.BlockSpec((tm, pd), lambda i, m: (i, 0)),
            pl.BlockSpec((pd, d), lambda i, m: (0, 0)),
        ],
        out_specs=pl.BlockSpec((tm, d), lambda i, m: (i, 0)),
        out_shape=jax.ShapeDtypeStruct((t, d), F32),
        scratch_shapes=[pltpu.VMEM((tm + 2 * hl, d), BF16), zbuf, zbuf, zbuf, zbuf, pltpu.VMEM((tm, tf), BF16)],
        compiler_params=_params(("parallel", "arbitrary"), vmem),
        name="conv_ffn_ple",
    )(h, h, h, norm_g.reshape(1, d), *ys, *ys, *ys, *wos,
      w_up, w_up, w_up, w_up,
      conv_w, conv_w, cb, cb, w_down,
      conv_w, conv_w, cb, cb, w_down,
      ple_g.reshape(1, d), w_gate, p, w_proj)


NA_KNORM_ROWS = 512


def _pair_head_rms(x, g, first):
    xx = x * x
    ss_a = jnp.sum(jnp.where(first, xx, 0.0), axis=-1, keepdims=True)
    ss_b = jnp.sum(jnp.where(first, 0.0, xx), axis=-1, keepdims=True)
    inv = lax.rsqrt(jnp.where(first, ss_a, ss_b) * (1.0 / NA_HEAD_DIM) + EPS)
    return x * inv * g


def _natten_body(q_ref, k_ref, v_ref, qg_ref, kg_ref, b_ref, o_ref, kn_ref, *, rows, rb):
    ib = pl.program_id(2)
    first = lax.broadcasted_iota(jnp.int32, (1, LANES), 1) < NA_HEAD_DIM
    kwin = NA_WIN_ROWS * GRID_W
    seq = rows * GRID_W

    @pl.when(ib == 0)
    def _():
        def norm_rows(c, carry):
            sl = pl.ds(pl.multiple_of(c * NA_KNORM_ROWS, NA_KNORM_ROWS), NA_KNORM_ROWS)
            kn_ref[sl, :] = _pair_head_rms(k_ref[0, sl, :].astype(F32), kg_ref[...], first).astype(BF16)
            return carry
        lax.fori_loop(0, seq // NA_KNORM_ROWS, norm_rows, 0)

    qn = _pair_head_rms(q_ref[0].astype(F32), qg_ref[...], first)

    combos = [(r, a) for r in range(rb) for a in range(2)]
    kstart, var = [], []
    for r in range(rb):
        i = ib * rb + r
        rs = jnp.clip(i - NA_WIN_ROWS // 2, 0, rows - NA_WIN_ROWS)
        var.append(i - rs)
        kstart.append(pl.multiple_of(rs * GRID_W, GRID_W))
    s = []
    for r, a in combos:
        q = qn[r * GRID_W:(r + 1) * GRID_W, :]
        keep = first if a == 0 else jnp.logical_not(first)
        qa = jnp.where(keep, q, 0.0).astype(BF16)
        k = kn_ref[pl.ds(kstart[r], kwin), :]
        s.append(lax.dot_general(qa, k, (((1,), (1,)), ((), ())), preferred_element_type=F32) + b_ref[a, var[r]])
    p, l = [], []
    for n in range(len(combos)):
        e = jnp.exp(s[n] - jnp.max(s[n], axis=-1, keepdims=True))
        l.append(jnp.sum(e, axis=-1, keepdims=True))
        p.append(e.astype(BF16))
    o = [jnp.dot(p[n], v_ref[0, pl.ds(kstart[r], kwin), :], preferred_element_type=F32) / l[n]
         for n, (r, a) in enumerate(combos)]
    for r in range(rb):
        o_ref[0, r * GRID_W:(r + 1) * GRID_W, :] = jnp.where(first, o[2 * r], o[2 * r + 1]).astype(o_ref.dtype)


def _natten_bias(rpb):
    kh, kw, w = NA_WIN_ROWS, NA_WIN_COLS, GRID_W
    var = np.arange(kh)
    jr = np.arange(kh)
    dr = jr[None, :] + (kh - 1) - var[:, None]
    sel_r = (dr[:, :, None] == np.arange(2 * kh - 1)[None, None, :]).astype(np.float32)
    col = np.arange(w)
    cstart = np.clip(col - kw // 2, 0, w - kw)
    dc = col[None, :] - col[:, None] + (kw - 1)
    valid = (col[None, :] >= cstart[:, None]) & (col[None, :] < cstart[:, None] + kw)
    sel_c = ((dc[:, :, None] == np.arange(2 * kw - 1)[None, None, :]) & valid[:, :, None]).astype(np.float32)
    b = jnp.einsum('hrd,vjr,cgd->hvcjg', rpb, jnp.asarray(sel_r), jnp.asarray(sel_c), precision=HIGHEST)
    mask = np.where(valid, 0.0, NEG_BIG).astype(np.float32)[None, None, :, None, :]
    b = b + jnp.asarray(mask)
    return b.reshape(rpb.shape[0], kh, w, kh * w)


def neighbourhood_attention(z, q_g, k_g, rpb, rb):
    bsz, seq, d3 = z.shape
    d = d3 // 3
    rows = seq // GRID_W
    bias = _natten_bias(rpb)
    hg = d // LANES
    kwin = NA_WIN_ROWS * GRID_W
    qg = jnp.tile(q_g.reshape(1, NA_HEAD_DIM), (1, 2)) * (NA_HEAD_DIM ** -0.5)
    kg = jnp.tile(k_g.reshape(1, NA_HEAD_DIM), (1, 2))
    vmem = (2 * (2 * seq * LANES * 2 + 2 * rb * GRID_W * LANES * 2 + 2 * NA_WIN_ROWS * GRID_W * kwin * 4)
            + seq * LANES * 2 + 6 * rb * GRID_W * kwin * 4 + (4 << 20))
    return pl.pallas_call(
        functools.partial(_natten_body, rows=rows, rb=rb),
        grid=(bsz, hg, rows // rb),
        in_specs=[pl.BlockSpec((1, rb * GRID_W, LANES), lambda b, h, i: (b, i, h)),
                  pl.BlockSpec((1, seq, LANES), lambda b, h, i: (b, 0, hg + h)),
                  pl.BlockSpec((1, seq, LANES), lambda b, h, i: (b, 0, 2 * hg + h)),
                  pl.BlockSpec((1, LANES), lambda b, h, i: (0, 0)),
                  pl.BlockSpec((1, LANES), lambda b, h, i: (0, 0)),
                  pl.BlockSpec((2, NA_WIN_ROWS, GRID_W, kwin), lambda b, h, i: (h, 0, 0, 0))],
        out_specs=pl.BlockSpec((1, rb * GRID_W, LANES), lambda b, h, i: (b, i, h)),
        out_shape=jax.ShapeDtypeStruct((bsz, seq, d), BF16),
        scratch_shapes=[pltpu.VMEM((seq, LANES), BF16)],
        compiler_params=_params(("parallel", "parallel", "arbitrary"), vmem),
        name="natten",
    )(z, z, z, qg, kg, bias)


def _hy_filter_body(f_ref, w1_ref, b1_ref, w2_ref, b2_ref, w3_ref, b3_ref, w4_ref, fr_ref, dl_ref,
                    h_ref, ss_ref):
    d = functools.partial(jnp.dot, preferred_element_type=F32, precision=HIGHEST)
    f = f_ref[...]
    fr = fr_ref[...]
    x = jnp.sin(fr * (d(f, w1_ref[...]) + b1_ref[...]))
    x = jnp.sin(fr * (d(x, w2_ref[...]) + b2_ref[...]))
    x = jnp.sin(fr * (d(x, w3_ref[...]) + b3_ref[...]))
    h = d(x, w4_ref[0]) * jnp.exp(-f[:, 0:1] * dl_ref[...])
    h_ref[...] = h * f[:, FILTER_EMB:FILTER_EMB + 1]

    @pl.when(pl.program_id(0) == 0)
    def _():
        ss_ref[...] = jnp.zeros_like(ss_ref)

    ss_ref[...] += jnp.sum(h * h, axis=0, keepdims=True)


def hyena_filter(seq, w1, b1, w2, b2, w3, b3, w4, freq, tl):
    c = w4.shape[1] // 2
    hid = w1.shape[1]
    t = np.linspace(0.0, 1.0, seq, dtype=np.float32)[:, None]
    bands = (FILTER_EMB - 1) // 2
    ang = (np.float32(2.0 * math.pi / seq) * np.arange(seq, dtype=np.float32)[:, None]
           * np.linspace(1e-4, bands - 1, bands, dtype=np.float32)[None])
    feats = np.concatenate([t, np.cos(ang), -np.sin(ang), np.ones((seq, 1), np.float32),
                            np.zeros((seq, FILTER_EMB_PAD - FILTER_EMB - 1), np.float32)], axis=-1)
    feats_b = np.concatenate([feats[0:1], feats[:0:-1]], axis=0)
    feats_b[0, FILTER_EMB] = 0.0
    feats2 = np.concatenate([feats, feats_b], axis=0)
    w1p = jnp.concatenate([w1, jnp.zeros((FILTER_EMB_PAD - FILTER_EMB, hid), F32)], axis=0)
    w4s = jnp.stack([w4[:, :c], w4[:, c:]])
    deltas = np.abs(np.linspace(DECAY_MIN, DECAY_MAX, c, dtype=np.float32))[None]
    full = lambda a: pl.BlockSpec(a.shape, lambda i: (0,) * a.ndim)
    nblk = seq // tl
    args = [jnp.asarray(feats2), w1p, b1.reshape(1, hid), w2, b2.reshape(1, hid), w3, b3.reshape(1, hid),
            w4s, freq.reshape(1, hid), jnp.asarray(deltas)]
    in_specs = ([pl.BlockSpec((tl, FILTER_EMB_PAD), lambda i: (i, 0))] + [full(a) for a in args[1:7]]
                + [pl.BlockSpec((1, hid, c), lambda i: (i // nblk, 0, 0)), full(args[8]), full(args[9])])
    return pl.pallas_call(
        _hy_filter_body,
        grid=(2 * nblk,),
        in_specs=in_specs,
        out_specs=[pl.BlockSpec((tl, c), lambda i: (i, 0)), pl.BlockSpec((1, c), lambda i: (0, 0))],
        out_shape=[jax.ShapeDtypeStruct((2 * seq, c), F32), jax.ShapeDtypeStruct((1, c), F32)],
        compiler_params=_params(("arbitrary",), 32 << 20),
        name="hyena_filter",
    )(*args)


def _hy_gate_body(z0_ref, z1_ref, zv_ref, w0_ref, w1_ref, wv_ref, b0_ref, b1_ref, bv_ref, s_ref, x0_ref,
                  *, seq):
    row = lax.broadcasted_iota(jnp.int32, (seq, LANES), 0)
    top = row == 0
    bot = row == seq - 1

    def conv(z_ref, w_ref, b_ref):
        z = z_ref[0]
        w = w_ref[...]
        zp = jnp.where(top, 0.0, pltpu.roll(z, 1, 0))
        zn = jnp.where(bot, 0.0, pltpu.roll(z, seq - 1, 0))
        return zp * w[0:1, :] + z * w[1:2, :] + zn * w[2:3, :] + b_ref[...]

    x0_ref[0] = conv(z0_ref, w0_ref, b0_ref)
    s_ref[0] = conv(zv_ref, wv_ref, bv_ref) * conv(z1_ref, w1_ref, b1_ref)


def hyena_gate(z, short_w, short_b, col0):
    bsz, seq, _ = z.shape
    c = HYENA_WIDTH
    nb = c // LANES
    o0 = col0 // LANES
    sb = short_b.reshape(1, 3 * c)
    zspec = lambda off: pl.BlockSpec((1, seq, LANES), lambda b, j: (b, 0, off + j))
    wspec = lambda off: pl.BlockSpec((3, LANES), lambda b, j: (0, off + j))
    bspec = lambda off: pl.BlockSpec((1, LANES), lambda b, j: (0, off + j))
    out = jax.ShapeDtypeStruct((bsz, seq, c), F32)
    return pl.pallas_call(
        functools.partial(_hy_gate_body, seq=seq),
        grid=(bsz, nb),
        in_specs=[zspec(o0), zspec(o0 + nb), zspec(o0 + 2 * nb),
                  wspec(0), wspec(nb), wspec(2 * nb), bspec(0), bspec(nb), bspec(2 * nb)],
        out_specs=[pl.BlockSpec((1, seq, LANES), lambda b, j: (b, 0, j))] * 2,
        out_shape=[out, out],
        compiler_params=_params(("parallel", "parallel"), 2 * 5 * seq * LANES * 4 + 8 * seq * LANES * 4 + (4 << 20)),
        name="hyena_gate",
    )(z, z, z, short_w, short_w, short_w, sb, sb, sb)


def _dft_consts(n, n_in):
    k = np.arange(n)[:, None].astype(np.float64)
    t = np.arange(n_in)[None, :].astype(np.float64)
    th = 2.0 * np.pi * k * t / n
    w_first = np.concatenate([np.cos(th), -np.sin(th)], axis=0)
    w_last = np.concatenate([np.cos(th).T, -np.sin(th).T], axis=1)
    tt = np.arange(n)[None, :].astype(np.float64)
    th2 = 2.0 * np.pi * k * tt / n
    cm, sm = np.cos(th2), np.sin(th2)
    m_mid = np.block([[cm, sm], [-sm, cm]])
    ph = 2.0 * np.pi * k * tt / (n * n)
    tw = np.stack([np.cos(ph), np.sin(ph)], axis=1)[..., None]
    return w_first, w_last, m_mid, tw


def _hy_stage1_body(w_ref, x_ref, o_ref, *, tb):
    w = w_ref[...]
    for j in range(tb):
        o_ref[0, j] = jnp.dot(w, x_ref[0, j], preferred_element_type=F32).astype(o_ref.dtype)


def hyena_stage(w, x, tb, out_dtype):
    bsz, n2, kk, c = x.shape
    m = w.shape[0]
    osz = jnp.dtype(out_dtype).itemsize
    vmem = 2 * (tb * kk * c * 2 + tb * m * c * osz + m * kk * 2) + 2 * m * c * 4 + (4 << 20)
    return pl.pallas_call(
        functools.partial(_hy_stage1_body, tb=tb),
        grid=(bsz, n2 // tb),
        in_specs=[pl.BlockSpec((m, kk), lambda b, j: (0, 0)),
                  pl.BlockSpec((1, tb, kk, c), lambda b, j: (b, j, 0, 0))],
        out_specs=pl.BlockSpec((1, tb, m, c), lambda b, j: (b, j, 0, 0)),
        out_shape=jax.ShapeDtypeStruct((bsz, n2, m, c), out_dtype),
        compiler_params=_params(("parallel", "parallel"), vmem),
        name="hyena_dft_outer",
    )(w, x)


def _twiddle(xr, xi, tc, ts, sign):
    return xr * tc + sign * (xi * ts), xi * tc - sign * (xr * ts)


HYENA_K1_PER_STEP = 4


def _hy_twiddled_input(a_ref, tw_ref, j, n):
    a = a_ref[0, j].astype(F32)
    xr, xi = _twiddle(a[0:n], a[n:2 * n], tw_ref[j, 0], tw_ref[j, 1], 1.0)
    return jnp.concatenate([xr, xi], axis=0).astype(BF16)


def _hy_spec_body(m_ref, tw_ref, a_ref, o_ref, *, n, kb):
    x = [_hy_twiddled_input(a_ref, tw_ref, j, n) for j in range(kb)]
    for j in range(kb):
        o_ref[0, j] = jnp.dot(m_ref[...], x[j], preferred_element_type=F32)


def _hy_mid_body(m_ref, mt_ref, tw_ref, hs_ref, a_ref, o_ref, *, n, kb):
    x = [_hy_twiddled_input(a_ref, tw_ref, j, n) for j in range(kb)]
    s = [jnp.dot(m_ref[...], x[j], preferred_element_type=F32) for j in range(kb)]
    p = []
    for j in range(kb):
        hs = hs_ref[0, j]
        sr, si = s[j][0:n], s[j][n:2 * n]
        hr, hi = hs[0:n], hs[n:2 * n]
        p.append(jnp.concatenate([sr * hr - si * hi, sr * hi + si * hr], axis=0).astype(BF16))
    y = [jnp.dot(mt_ref[...], p[j], preferred_element_type=F32) for j in range(kb)]
    for j in range(kb):
        yr, yi = _twiddle(y[j][0:n], y[j][n:2 * n], tw_ref[j, 0], tw_ref[j, 1], -1.0)
        o_ref[0, j] = jnp.concatenate([yr, yi], axis=0).astype(o_ref.dtype)


def hyena_spectrum(a, m_mid, tw):
    _, n, n2x, c = a.shape
    kb = min(HYENA_K1_PER_STEP, n)
    return pl.pallas_call(
        functools.partial(_hy_spec_body, n=n, kb=kb),
        grid=(n // kb, 1),
        in_specs=[pl.BlockSpec((n2x, n2x), lambda k, b: (0, 0)),
                  pl.BlockSpec((kb, 2, n, 1), lambda k, b: (k, 0, 0, 0)),
                  pl.BlockSpec((1, kb, n2x, c), lambda k, b: (b, k, 0, 0))],
        out_specs=pl.BlockSpec((1, kb, n2x, c), lambda k, b: (b, k, 0, 0)),
        out_shape=jax.ShapeDtypeStruct((1, n, n2x, c), F32),
        compiler_params=_params(("parallel", "parallel"), 40 << 20),
        name="hyena_filter_spectrum",
    )(m_mid, tw, a)


def hyena_mid(a, hspec, m_mid, m_mid_t, tw):
    bsz, n, n2x, c = a.shape
    kb = min(HYENA_K1_PER_STEP, n)
    return pl.pallas_call(
        functools.partial(_hy_mid_body, n=n, kb=kb),
        grid=(n // kb, bsz),
        in_specs=[pl.BlockSpec((n2x, n2x), lambda k, b: (0, 0)),
                  pl.BlockSpec((n2x, n2x), lambda k, b: (0, 0)),
                  pl.BlockSpec((kb, 2, n, 1), lambda k, b: (k, 0, 0, 0)),
                  pl.BlockSpec((1, kb, n2x, c), lambda k, b: (0, k, 0, 0)),
                  pl.BlockSpec((1, kb, n2x, c), lambda k, b: (b, k, 0, 0))],
        out_specs=pl.BlockSpec((1, kb, n2x, c), lambda k, b: (b, k, 0, 0)),
        out_shape=jax.ShapeDtypeStruct((bsz, n, n2x, c), BF16),
        compiler_params=_params(("parallel", "parallel"), 40 << 20),
        name="hyena_dft_mid",
    )(m_mid, m_mid_t, tw, hspec, a)


def _hy_combine_body(y_ref, s_ref, x0_ref, sc_ref, bi_ref, o_ref):
    o_ref[...] = ((y_ref[...] * sc_ref[...] + s_ref[...] * bi_ref[...]) * x0_ref[...]).astype(o_ref.dtype)


def hyena_combine(y, s, x0, scale, bias, tm):
    t, c = y.shape
    row = lambda: pl.BlockSpec((tm, c), lambda i: (i, 0))
    vec = lambda: pl.BlockSpec((1, c), lambda i: (0, 0))
    return pl.pallas_call(
        _hy_combine_body,
        grid=(t // tm,),
        in_specs=[row(), row(), row(), vec(), vec()],
        out_specs=row(),
        out_shape=jax.ShapeDtypeStruct((t, c), BF16),
        compiler_params=_params(("parallel",), 2 * 4 * tm * c * 4 + (4 << 20)),
        name="hyena_combine",
    )(y, s, x0, scale.reshape(1, c), bias.reshape(1, c))


def hyena_bidir(z, col0, short_w, short_b, f_w1, f_b1, f_w2, f_b2, f_w3, f_b3, f_w4, f_freq, bias):
    bsz, seq, _ = z.shape
    c = HYENA_WIDTH
    n = int(round(math.sqrt(2 * seq)))
    assert n * n == 2 * seq and n % 16 == 0
    nh = n // 2
    tb = 8

    w_first, w_last, m_mid, tw = _dft_consts(n, n)
    w_first_x = jnp.asarray(w_first[:, :nh], BF16)
    w_first_h = jnp.asarray(w_first, BF16)
    w_last_y = jnp.asarray(w_last[:nh], BF16)
    m_mid_j = jnp.asarray(m_mid, BF16)
    m_mid_t = jnp.asarray(m_mid.T, BF16)
    tw_j = jnp.asarray(tw, F32)

    kern, ss = hyena_filter(seq, f_w1, f_b1, f_w2, f_b2, f_w3, f_b3, f_w4, f_freq, tl=min(seq, 1024))
    scale = lax.rsqrt(ss[0] + EPS) * (1.0 / (2 * seq))
    kern_t = jnp.transpose(kern.reshape(1, n, n, c), (0, 2, 1, 3)).astype(BF16)
    ah = hyena_stage(w_first_h, kern_t, tb, BF16)
    ah = jnp.transpose(ah.reshape(1, n, 2, n, c), (0, 3, 2, 1, 4)).reshape(1, n, 2 * n, c)
    hspec = hyena_spectrum(ah, m_mid_j, tw_j)

    s, x0 = hyena_gate(z, short_w, short_b, col0)
    s_t = jnp.transpose(s.reshape(bsz, nh, n, c), (0, 2, 1, 3)).astype(BF16)
    a = hyena_stage(w_first_x, s_t, tb, BF16)
    a = jnp.transpose(a.reshape(bsz, n, 2, n, c), (0, 3, 2, 1, 4)).reshape(bsz, n, 2 * n, c)
    bm = hyena_mid(a, hspec, m_mid_j, m_mid_t, tw_j)
    bm = jnp.transpose(bm.reshape(bsz, n, 2, n, c), (0, 3, 2, 1, 4)).reshape(bsz, n, 2 * n, c)
    y = hyena_stage(w_last_y, bm, tb, F32)
    y = jnp.transpose(y, (0, 2, 1, 3)).reshape(bsz * seq, c)
    return hyena_combine(y, s.reshape(bsz * seq, c), x0.reshape(bsz * seq, c), scale, bias,
                         tm=min(bsz * seq, 2048))


RWKV_PAIRS = RWKV_WIDTH // LANES
RWKV_PREP_CHUNKS = 4


def _pair_rows(x, first):
    return jnp.concatenate([jnp.where(first, x, 0.0), jnp.where(first, 0.0, x)], axis=0)


def _fold_rows(x2, ch):
    return x2[0:ch] + x2[ch:2 * ch]


def _rwkv_prep_body(z_ref, halo_ref, mu_ref, w0_ref, wup_ref, a0_ref, aup_ref, kk_ref, ka_ref, rk_ref, sel_ref,
                    wt_ref, rt_ref, kb_ref, ab_ref, v_ref, ut_ref, yk_ref, ara_ref, ge_ref, bonus_ref,
                    *, reverse, nsteps, cps):
    ch = RWKV_CHUNK
    cw = RWKV_WIDTH
    rows = cps * ch
    c_idx = pl.program_id(1)

    zs = z_ref[0]
    row = lax.broadcasted_iota(jnp.int32, (rows, 1), 0)
    if reverse:
        nbr = jnp.where(c_idx == nsteps - 1, 0.0, halo_ref[0, 0:1, :])
        shifted = jnp.where(row == rows - 1, jnp.broadcast_to(nbr, zs.shape), pltpu.roll(zs, rows - 1, 0))
    else:
        nbr = jnp.where(c_idx == 0, 0.0, halo_ref[0, SUBLANES - 1:SUBLANES, :])
        shifted = jnp.where(row == 0, jnp.broadcast_to(nbr, zs.shape), pltpu.roll(zs, 1, 0))
    zd = zs + (shifted - zs) * mu_ref[...]

    r = zd[:, 0:cw]
    k = zd[:, cw:2 * cw]
    v = zd[:, 2 * cw:3 * cw]
    lora = zd[:, 3 * cw:3 * cw + DECAY_RANK + ICLR_RANK]
    x = w0_ref[...] + _bdot(jnp.tanh(lora), wup_ref[...])
    log_w = jnp.minimum(x, 0.0) - jnp.log(1.0 + jnp.exp(-jnp.abs(x))) - 0.5
    lnw = -jnp.exp(log_w)
    a = jax.nn.sigmoid(a0_ref[...] + _bdot(lora, aup_ref[...]))
    sel = sel_ref[...]
    kkr = k * kk_ref[...]
    kkn = jnp.sqrt(_bdot(kkr * kkr, sel))
    kk = kkr / jnp.maximum(kkn, 1e-12)
    k2 = k * (1.0 + (a - 1.0) * ka_ref[...])
    ah = kk * a
    bonus_ref[0] = _bdot(r * k2 * rk_ref[...], sel) * v

    ri = lax.broadcasted_iota(jnp.int32, (rows, rows), 0)
    ci = lax.broadcasted_iota(jnp.int32, (rows, rows), 1)
    csh = ch.bit_length() - 1
    tri = ((ci >= ri) if reverse else (ci <= ri)) & ((ri >> csh) == (ci >> csh))
    cs = _dot_exact_lhs(tri.astype(BF16), lnw)
    total = jnp.concatenate(
        [jnp.broadcast_to(jnp.sum(lnw[g * ch:(g + 1) * ch], axis=0, keepdims=True), (ch, cw)) for g in range(cps)],
        axis=0)
    e_inv = jnp.exp(-cs)
    e_end = jnp.exp(total - cs)
    rt = r * jnp.exp(cs)
    kkt = kk * jnp.exp(cs - lnw)
    kh = k2 * e_inv
    ahh = ah * e_inv
    rt_ref[0] = rt.astype(rt_ref.dtype)
    kb_ref[0] = (k2 * e_end).astype(kb_ref.dtype)
    ab_ref[0] = (ah * e_end).astype(ab_ref.dtype)
    v_ref[0] = v.astype(v_ref.dtype)
    for g in range(cps):
        ge_ref[0, g] = jnp.exp(total[g * ch:g * ch + 1])

    r2 = lax.broadcasted_iota(jnp.int32, (2 * ch, 2 * ch), 0)
    c2 = lax.broadcasted_iota(jnp.int32, (2 * ch, 2 * ch), 1)
    same_head = (r2 < ch) == (c2 < ch)
    si = r2 & (ch - 1)
    sj = c2 & (ch - 1)
    before = (sj > si) if reverse else (sj < si)
    strict = same_head & before
    incl = same_head & jnp.logical_or(before, si == sj)
    eye = (r2 == c2).astype(F32)
    first = lax.broadcasted_iota(jnp.int32, (1, LANES), 1) < RWKV_HEAD_DIM

    units = [(g, p) for g in range(cps) for p in range(RWKV_PAIRS)]
    un = range(len(units))
    tile = lambda arr, g, p: arr[g * ch:(g + 1) * ch, p * LANES:(p + 1) * LANES]
    l2 = [_pair_rows(tile(kkt, g, p), first) for g, p in units]
    l4 = [jnp.concatenate([l2[n], _pair_rows(tile(rt, g, p), first)], axis=0) for n, (g, p) in enumerate(units)]
    v2 = [_pair_rows(tile(v, g, p), first) for g, p in units]
    sk = [_bdot_nt(l4[n], jnp.concatenate([tile(kh, g, p)] * 2, axis=0)) for n, (g, p) in enumerate(units)]
    sa = [_bdot_nt(l4[n], jnp.concatenate([tile(ahh, g, p)] * 2, axis=0)) for n, (g, p) in enumerate(units)]
    nm = [jnp.where(strict, sa[n][0:2 * ch], 0.0) for n in un]
    for n, (g, p) in enumerate(units):
        ara_ref[0, g, p] = jnp.where(incl, sa[n][2 * ch:4 * ch], 0.0).astype(ara_ref.dtype)
    akr = [jnp.concatenate([jnp.where(strict, sk[n][0:2 * ch], 0.0),
                            jnp.where(incl, sk[n][2 * ch:4 * ch], 0.0)], axis=0) for n in un]
    av = [_bdot(akr[n], v2[n]) for n in un]

    lvl = (si >> 1) == (sj >> 1)
    t = [eye - jnp.where(lvl, nm[n], 0.0) for n in un]
    s = 2
    while s < ch:
        sh = s.bit_length() - 1
        lvl = ((si >> (sh + 1)) == (sj >> (sh + 1))) & ((si >> sh) != (sj >> sh))
        x = [_bdot(jnp.where(lvl, nm[n], 0.0), t[n]) for n in un]
        t = [t[n] - _bdot(t[n], x[n]) for n in un]
        s *= 2

    wu = [_bdot(t[n], jnp.concatenate([l2[n], av[n][0:2 * ch]], axis=1)) for n in un]
    gather = lambda f: jnp.concatenate(
        [jnp.concatenate([f(g * RWKV_PAIRS + p) for p in range(RWKV_PAIRS)], axis=1) for g in range(cps)], axis=0)
    wt_ref[0] = gather(lambda n: _fold_rows(wu[n][:, 0:LANES], ch)).astype(wt_ref.dtype)
    ut_ref[0] = gather(lambda n: _fold_rows(wu[n][:, LANES:2 * LANES], ch)).astype(ut_ref.dtype)
    yk_ref[0] = gather(lambda n: _fold_rows(av[n][2 * ch:4 * ch], ch)).astype(yk_ref.dtype)


def rwkv7_prep(z, mu, w0, w_up, a0, a_up, k_k, k_a, r_k, reverse):
    bsz, seq, _ = z.shape
    ch = RWKV_CHUNK
    cw = RWKV_WIDTH
    nc = seq // ch
    cps = min(RWKV_PREP_CHUNKS, nc)
    nsteps = nc // cps
    cpb = cps * ch // SUBLANES
    nhb = seq // SUBLANES
    sel_np = (np.arange(cw)[:, None] // RWKV_HEAD_DIM == np.arange(cw)[None, :] // RWKV_HEAD_DIM)
    sel = jnp.asarray(sel_np, BF16)
    zero_pad = jnp.zeros((DECAY_RANK, cw), F32)
    wup = jnp.concatenate([w_up, zero_pad], axis=0).astype(BF16)
    aup = jnp.concatenate([zero_pad, a_up], axis=0).astype(BF16)
    vec = lambda a: a.reshape(1, -1)
    params = [vec(mu), vec(w0), wup, vec(a0), aup, vec(k_k), vec(k_a), vec(r_k), sel]
    full = lambda a: pl.BlockSpec(a.shape, lambda b, c: (0,) * a.ndim)
    if reverse:
        halo = pl.BlockSpec((1, SUBLANES, RWKV_SHIFT_COLS), lambda b, c: (b, jnp.minimum((c + 1) * cpb, nhb - 1), 0))
    else:
        halo = pl.BlockSpec((1, SUBLANES, RWKV_SHIFT_COLS), lambda b, c: (b, jnp.maximum(c * cpb - 1, 0), 0))
    tok = pl.BlockSpec((1, cps * ch, cw), lambda b, c: (b, c, 0))
    tok_bf = jax.ShapeDtypeStruct((bsz, seq, cw), BF16)
    tok_f = jax.ShapeDtypeStruct((bsz, seq, cw), F32)
    return pl.pallas_call(
        functools.partial(_rwkv_prep_body, reverse=reverse, nsteps=nsteps, cps=cps),
        grid=(bsz, nsteps),
        in_specs=[pl.BlockSpec((1, cps * ch, RWKV_SHIFT_COLS), lambda b, c: (b, c, 0)), halo]
                 + [full(a) for a in params],
        out_specs=[tok] * 7
                  + [pl.BlockSpec((1, cps, RWKV_PAIRS, 2 * ch, 2 * ch), lambda b, c: (b, c, 0, 0, 0)),
                     pl.BlockSpec((1, cps, 1, cw), lambda b, c: (b, c, 0, 0)),
                     tok],
        out_shape=[tok_bf, tok_bf, tok_bf, tok_bf, tok_bf, tok_bf, tok_bf,
                   jax.ShapeDtypeStruct((bsz, nc, RWKV_PAIRS, 2 * ch, 2 * ch), BF16),
                   jax.ShapeDtypeStruct((bsz, nc, 1, cw), F32),
                   tok_f],
        compiler_params=_params(("parallel", "parallel"), 40 << 20),
        name="rwkv7_prep_bwd" if reverse else "rwkv7_prep_fwd",
    )(z, z, *params)


def _rwkv_scan_body(*refs, bsz, final):
    if final:
        (wt_ref, rt_ref, kb_ref, ab_ref, v_ref, ut_ref, yk_ref, ara_ref, ge_ref,
         yin_ref, b1_ref, b2_ref, gd_ref, gup_ref, lnw_ref, lnb_ref, sel_ref, o_ref, st_ref) = refs
    else:
        (wt_ref, rt_ref, kb_ref, ab_ref, v_ref, ut_ref, yk_ref, ara_ref, ge_ref, o_ref, st_ref) = refs
    ch = RWKV_CHUNK

    @pl.when(pl.program_id(0) == 0)
    def _():
        st_ref[...] = jnp.zeros_like(st_ref)

    first = lax.broadcasted_iota(jnp.int32, (1, LANES), 1) < RWKV_HEAD_DIM
    blk = ((lax.broadcasted_iota(jnp.int32, (LANES, LANES), 0) < RWKV_HEAD_DIM)
           == (lax.broadcasted_iota(jnp.int32, (LANES, LANES), 1) < RWKV_HEAD_DIM))
    idx = [(b, p) for b in range(bsz) for p in range(RWKV_PAIRS)]
    sl = [slice(p * LANES, (p + 1) * LANES) for p in range(RWKV_PAIRS)]
    st = [st_ref[b * RWKV_PAIRS + p] for b, p in idx]
    lm0 = [_bdot_nt(jnp.concatenate([wt_ref[b, :, sl[p]], rt_ref[b, :, sl[p]]], axis=0), st[n])
           for n, (b, p) in enumerate(idx)]
    u = [lm0[n][0:ch] + ut_ref[b, :, sl[p]] for n, (b, p) in enumerate(idx)]
    yr = [_bdot(ara_ref[b, 0, p], _pair_rows(u[n], first)) for n, (b, p) in enumerate(idx)]
    upd = [_bdot_tn(jnp.concatenate([v_ref[b, :, sl[p]].astype(F32), -u[n]], axis=0),
                    jnp.concatenate([kb_ref[b, :, sl[p]], ab_ref[b, :, sl[p]]], axis=0))
           for n, (b, p) in enumerate(idx)]
    for n, (b, p) in enumerate(idx):
        st_ref[b * RWKV_PAIRS + p] = st[n] * ge_ref[b, 0, :, sl[p]] + jnp.where(blk, upd[n], 0.0)
    for b in range(bsz):
        y = jnp.concatenate(
            [lm0[b * RWKV_PAIRS + p][ch:2 * ch] + yk_ref[b, :, sl[p]] - _fold_rows(yr[b * RWKV_PAIRS + p], ch)
             for p in range(RWKV_PAIRS)], axis=1)
        if not final:
            o_ref[b] = y
        else:
            sel = sel_ref[...]
            yt = y + yin_ref[b]
            mean = _bdot(yt, sel) * (1.0 / RWKV_HEAD_DIM)
            dv = yt - mean
            var = _bdot(dv * dv, sel) * (1.0 / RWKV_HEAD_DIM)
            yn = dv * lax.rsqrt(var + RWKV_GN_EPS) * lnw_ref[...] + lnb_ref[...]
            g = _bdot(jax.nn.sigmoid(gd_ref[b]), gup_ref[...])
            o_ref[b] = ((yn + b1_ref[b] + b2_ref[b]) * g).astype(o_ref.dtype)


def rwkv7_scan(prep, reverse, final_args=None):
    wt, rt, kb, ab, v, ut, yk, ara, ge, _ = prep
    bsz, seq, cw = wt.shape
    ch = RWKV_CHUNK
    nc = seq // ch
    cidx = (lambda c: nc - 1 - c) if reverse else (lambda c: c)
    tok = pl.BlockSpec((bsz, ch, cw), lambda c: (0, cidx(c), 0))
    in_specs = [tok] * 7 + [pl.BlockSpec((bsz, 1, RWKV_PAIRS, 2 * ch, 2 * ch), lambda c: (0, cidx(c), 0, 0, 0)),
                            pl.BlockSpec((bsz, 1, 1, cw), lambda c: (0, cidx(c), 0, 0))]
    args = [wt, rt, kb, ab, v, ut, yk, ara, ge]
    final = final_args is not None
    if final:
        z, y_in, bonus_a, bonus_b, g_up, ln_w, ln_b = final_args
        sel_np = (np.arange(cw)[:, None] // RWKV_HEAD_DIM == np.arange(cw)[None, :] // RWKV_HEAD_DIM)
        full = lambda a: pl.BlockSpec(a.shape, lambda c: (0,) * a.ndim)
        extra = [g_up.astype(BF16), ln_w.reshape(1, cw), ln_b.reshape(1, cw), jnp.asarray(sel_np, BF16)]
        in_specs += [tok, tok, tok,
                     pl.BlockSpec((bsz, ch, GATE_RANK), lambda c: (0, cidx(c), RWKV_SHIFT_COLS // GATE_RANK))]
        in_specs += [full(a) for a in extra]
        args += [y_in, bonus_a, bonus_b, z] + extra
    out_dtype = BF16 if final else F32
    return pl.pallas_call(
        functools.partial(_rwkv_scan_body, bsz=bsz, final=final),
        grid=(nc,),
        in_specs=in_specs,
        out_specs=tok,
        out_shape=jax.ShapeDtypeStruct((bsz, seq, cw), out_dtype),
        scratch_shapes=[pltpu.VMEM((bsz * RWKV_PAIRS, LANES, LANES), F32)],
        compiler_params=_params(("arbitrary",), 40 << 20),
        name="rwkv7_scan_bwd" if reverse else "rwkv7_scan_fwd",
    )(*args)


def rwkv7_bidir(z, mu, w0, w_up, a0, a_up, g_up, k_k, k_a, r_k, ln_w, ln_b):
    rk = r_k.reshape(-1)
    prep_f = rwkv7_prep(z, mu[0], w0[0], w_up[0], a0[0], a_up[0], k_k, k_a, rk, reverse=False)
    prep_b = rwkv7_prep(z, mu[1], w0[1], w_up[1], a0[1], a_up[1], k_k, k_a, rk, reverse=True)
    y_f = rwkv7_scan(prep_f, reverse=False)
    return rwkv7_scan(prep_b, reverse=True, final_args=(z, y_f, prep_f[9], prep_b[9], g_up, ln_w, ln_b))


def kernel(x, p, mix_norm, mix_w_in, rwkv_mu, rwkv_w0, rwkv_w_up, rwkv_a0, rwkv_a_up, rwkv_g_up, rwkv_k_k, rwkv_k_a, rwkv_r_k, rwkv_ln_w, rwkv_ln_b, hy_short_w, hy_short_b, hy_f_w1, hy_f_b1, hy_f_w2, hy_f_b2, hy_f_w3, hy_f_b3, hy_f_w4, hy_f_freq, hy_bias, mix_w_out, na_norm, na_w_qkv, na_q_g, na_k_g, na_rpb, na_w_out, ffn_norm, ffn_w_up, ffn_conv_w, ffn_conv_b, ffn_w_down, ple_norm, ple_w_gate, ple_w_proj):
    bsz, seq, d = x.shape
    depth = p.shape[0]
    t = bsz * seq
    tm = min(1024, seq)
    h = x.reshape(t, d)
    for i in range(depth):
        j = i // 2
        if i % 2 == 0:
            w_in = mix_w_in[j].astype(BF16)
            z = norm_matmul(h, mix_norm[j], w_in, tm=tm, tn=w_in.shape[1] // 2).reshape(bsz, seq, -1)
            y_a = rwkv7_bidir(z, rwkv_mu[j], rwkv_w0[j], rwkv_w_up[j], rwkv_a0[j], rwkv_a_up[j], rwkv_g_up[j],
                              rwkv_k_k[j], rwkv_k_a[j], rwkv_r_k[j], rwkv_ln_w[j], rwkv_ln_b[j])
            y_b = hyena_bidir(z, RWKV_COLS, hy_short_w[j], hy_short_b[j], hy_f_w1[j], hy_f_b1[j], hy_f_w2[j],
                              hy_f_b2[j], hy_f_w3[j], hy_f_b3[j], hy_f_w4[j], hy_f_freq[j], hy_bias[j])
            w_out = mix_w_out[j].astype(BF16)
            ys, wos = [y_a.reshape(t, -1), y_b], [w_out[:RWKV_WIDTH], w_out[RWKV_WIDTH:]]
        else:
            z = norm_matmul(h, na_norm[j], na_w_qkv[j].astype(BF16), tm=tm, tn=1024, out_dtype=BF16)
            o = neighbourhood_attention(z.reshape(bsz, seq, 3 * d), na_q_g[j], na_k_g[j], na_rpb[j], rb=32)
            ys, wos = [o.reshape(t, d)], [na_w_out[j].astype(BF16)]
        h = conv_ffn_ple(h, ys, wos, ffn_norm[i], ffn_w_up[i].astype(BF16), ffn_conv_w[i], ffn_conv_b[i],
                         ffn_w_down[i].astype(BF16), p[i].reshape(t, -1), ple_norm[i],
                         ple_w_gate[i].astype(BF16), ple_w_proj[i].astype(BF16), seq_len=seq, tm=tm, tf=256)
    return h.reshape(bsz, seq, d)
```

```python
import functools
import math

import numpy as np
import jax
import jax.numpy as jnp
from jax import lax
from jax.experimental import pallas as pl
from jax.experimental.pallas import tpu as pltpu

F32 = jnp.float32
BF16 = jnp.bfloat16
HIGHEST = lax.Precision.HIGHEST

EPS = 1e-6
GRID_W = 64
RWKV_HEADS = 8
RWKV_HEAD_DIM = 64
RWKV_WIDTH = RWKV_HEADS * RWKV_HEAD_DIM
DECAY_RANK = 64
ICLR_RANK = 64
GATE_RANK = 128
RWKV_SHIFT_COLS = 3 * RWKV_WIDTH + DECAY_RANK + ICLR_RANK
RWKV_COLS = RWKV_SHIFT_COLS + GATE_RANK
RWKV_GN_EPS = 64e-5
RWKV_CHUNK = 64
HYENA_WIDTH = 512
FILTER_EMB = 17
FILTER_EMB_PAD = 32
DECAY_MIN = math.log(1e-2) / 1.5
DECAY_MAX = math.log(1e-2) / 0.3
NA_HEADS = 16
NA_HEAD_DIM = 64
NA_WIN_ROWS = 8
NA_WIN_COLS = 16
NEG_BIG = -1e30

LANES = 128
SUBLANES = 8
V7X_VMEM_BYTES = 64 * 1024 * 1024
VMEM_CAP = V7X_VMEM_BYTES - 8 * 1024 * 1024


def _params(semantics, vmem_bytes):
    return pltpu.CompilerParams(dimension_semantics=semantics,
                                vmem_limit_bytes=int(min(max(vmem_bytes, 16 * 1024 * 1024), VMEM_CAP)))


def _rms_rows(x, g):
    ms = jnp.mean(x * x, axis=-1, keepdims=True)
    return x * lax.rsqrt(ms + EPS) * g


def _bdot(a, b):
    return jnp.dot(a.astype(BF16), b.astype(BF16), preferred_element_type=F32)


def _bdot_nt(a, b):
    return lax.dot_general(a.astype(BF16), b.astype(BF16), (((1,), (1,)), ((), ())),
                           preferred_element_type=F32)


def _bdot_tn(a, b):
    return lax.dot_general(a.astype(BF16), b.astype(BF16), (((0,), (0,)), ((), ())),
                           preferred_element_type=F32)


def _dot_exact_lhs(a_bf16, b):
    d = functools.partial(jnp.dot, preferred_element_type=F32)
    b1 = b.astype(BF16)
    r1 = b - b1.astype(F32)
    b2 = r1.astype(BF16)
    b3 = (r1 - b2.astype(F32)).astype(BF16)
    return d(a_bf16, b1) + (d(a_bf16, b2) + d(a_bf16, b3))


def _norm_matmul_body(x_ref, g_ref, w_ref, o_ref, xn_ref):
    @pl.when(pl.program_id(1) == 0)
    def _():
        xn_ref[...] = _rms_rows(x_ref[...], g_ref[...]).astype(BF16)

    o_ref[...] = jnp.dot(xn_ref[...], w_ref[...], preferred_element_type=F32).astype(o_ref.dtype)


def norm_matmul(x, g, w, tm, tn, out_dtype=F32):
    t, d = x.shape
    n = w.shape[1]
    osz = jnp.dtype(out_dtype).itemsize
    vmem = 2 * (tm * d * 4 + d * tn * 2 + tm * tn * osz) + tm * d * 2 + (4 << 20)
    return pl.pallas_call(
        _norm_matmul_body,
        grid=(t // tm, n // tn),
        in_specs=[pl.BlockSpec((tm, d), lambda i, j: (i, 0)),
                  pl.BlockSpec((1, d), lambda i, j: (0, 0)),
                  pl.BlockSpec((d, tn), lambda i, j: (0, j))],
        out_specs=pl.BlockSpec((tm, tn), lambda i, j: (i, j)),
        out_shape=jax.ShapeDtypeStruct((t, n), out_dtype),
        scratch_shapes=[pltpu.VMEM((tm, d), BF16)],
        compiler_params=_params(("parallel", "arbitrary"), vmem),
        name="norm_matmul",
    )(x, g.reshape(1, d), w)


FFN_HALO = 16
FFN_ROW_BLOCK = 64


def _ffn_ple_body(*refs, tm, tiles_per_seq, nd, n_in):
    xm_ref, xp_ref, xx_ref, g_ref = refs[0:4]
    ym, yp, yx, wo = (refs[4 + k * n_in:4 + (k + 1) * n_in] for k in range(4))
    (wa0_ref, wb0_ref, wa1_ref, wb1_ref,
     cwa0_ref, cwb0_ref, cba0_ref, cbb0_ref, wd0_ref,
     cwa1_ref, cwb1_ref, cba1_ref, cbb1_ref, wd1_ref,
     pg_ref, wg_ref, p_ref, wp_ref,
     o_ref, xn_ref, za0_ref, zb0_ref, za1_ref, zb1_ref, act_ref) = refs[4 + 4 * n_in:]
    i = pl.program_id(0)
    m = pl.program_id(1)
    hl = FFN_HALO

    def prologue():
        def mixed(x_ref, ys):
            acc = x_ref[...]
            for y_ref, w_ref in zip(ys, wo):
                acc = acc + jnp.dot(y_ref[...], w_ref[...], preferred_element_type=F32)
            return acc

        g = g_ref[...]
        first = (i % tiles_per_seq) == 0
        last = (i % tiles_per_seq) == tiles_per_seq - 1
        xm = mixed(xm_ref, ym)
        xn_ref[hl:hl + tm, :] = _rms_rows(xm, g).astype(BF16)
        xn_ref[0:hl, :] = jnp.where(first, 0.0, _rms_rows(mixed(xp_ref, yp), g)).astype(BF16)
        xn_ref[hl + tm:2 * hl + tm, :] = jnp.where(last, 0.0, _rms_rows(mixed(xx_ref, yx), g)).astype(BF16)
        o_ref[...] = xm

    def conv(z_ref, r0, cw, cb):
        return (z_ref[r0 - 1:r0 - 1 + FFN_ROW_BLOCK, :] * cw[0:1, :] + z_ref[r0:r0 + FFN_ROW_BLOCK, :] * cw[1:2, :]
                + z_ref[r0 + 1:r0 + 1 + FFN_ROW_BLOCK, :] * cw[2:3, :] + cb)

    def contribution(za_ref, zb_ref, cwa_ref, cwb_ref, cba_ref, cbb_ref, wd_ref):
        cwa, cwb, cba, cbb = cwa_ref[...], cwb_ref[...], cba_ref[...], cbb_ref[...]
        for blk in range(tm // FFN_ROW_BLOCK):
            r0 = hl + blk * FFN_ROW_BLOCK
            act = jax.nn.gelu(conv(za_ref, r0, cwa, cba)) * conv(zb_ref, r0, cwb, cbb)
            act_ref[blk * FFN_ROW_BLOCK:(blk + 1) * FFN_ROW_BLOCK, :] = act.astype(BF16)
        return jnp.dot(act_ref[...], wd_ref[...], preferred_element_type=F32)

    @pl.when(m == 0)
    def _():
        prologue()
        za1_ref[...] = jnp.zeros_like(za1_ref)
        zb1_ref[...] = jnp.zeros_like(zb1_ref)

    xn = xn_ref[...]
    za0_ref[...] = jnp.dot(xn, wa0_ref[...], preferred_element_type=F32)
    zb0_ref[...] = jnp.dot(xn, wb0_ref[...], preferred_element_type=F32)
    ca = contribution(za1_ref, zb1_ref, cwa0_ref, cwb0_ref, cba0_ref, cbb0_ref, wd0_ref)
    o_ref[...] += jnp.where(m > 0, ca, 0.0)
    za1_ref[...] = jnp.dot(xn, wa1_ref[...], preferred_element_type=F32)
    zb1_ref[...] = jnp.dot(xn, wb1_ref[...], preferred_element_type=F32)
    o_ref[...] += contribution(za0_ref, zb0_ref, cwa1_ref, cwb1_ref, cba1_ref, cbb1_ref, wd1_ref)

    @pl.when(m == nd - 1)
    def _():
        h2 = o_ref[...]
        gate = jax.nn.sigmoid(jnp.dot(_rms_rows(h2, pg_ref[...]).astype(BF16), wg_ref[...],
                                      preferred_element_type=F32))
        proj = jnp.dot(p_ref[...].astype(BF16), wp_ref[...], preferred_element_type=F32)
        o_ref[...] = h2 + gate * proj


def conv_ffn_ple(h, ys, wos, norm_g, w_up, conv_w, conv_b, w_down, p, ple_g, w_gate, w_proj, seq_len, tm, tf):
    t, d = h.shape
    n_in = len(ys)
    f = w_down.shape[0]
    pd = p.shape[1]
    nf = f // tf
    assert nf % 2 == 1
    nd = (nf + 1) // 2
    hl = FFN_HALO
    tiles_per_seq = seq_len // tm
    nhb = t // hl
    last = nf - 1
    up0 = lambda m: jnp.minimum(2 * m, last)
    up1 = lambda m: jnp.minimum(2 * m + 1, last)
    pr0 = lambda m: jnp.maximum(2 * m - 1, 0)
    pr1 = lambda m: 2 * m
    wspec = lambda fn, off: pl.BlockSpec((d, tf), lambda i, m: (0, off + fn(m)))
    cspec = lambda fn, off: pl.BlockSpec((3, tf), lambda i, m: (0, off + fn(m)))
    bspec = lambda fn, off: pl.BlockSpec((1, tf), lambda i, m: (0, off + fn(m)))
    dspec = lambda fn: pl.BlockSpec((tf, d), lambda i, m: (fn(m), 0))
    cb = conv_b.reshape(1, 2 * f)
    zbuf = pltpu.VMEM((tm + 2 * hl, tf), F32)
    prev_blk = lambda i, m: (jnp.maximum(i * (tm // hl) - 1, 0), 0)
    next_blk = lambda i, m: (jnp.minimum((i + 1) * (tm // hl), nhb - 1), 0)
    ksum = sum(y.shape[1] for y in ys)
    vmem = (2 * (2 * tm * d * 4 + 2 * hl * d * 4 + 4 * d * tf * 2 + 2 * tf * d * 2 + tm * pd * 4 + d * d * 2 + pd * d * 2
                 + (tm + 2 * hl) * ksum * 2 + ksum * d * 2)
            + (tm + 2 * hl) * d * 2 + 4 * (tm + 2 * hl) * tf * 4 + 8 * tm * tf * 4 + 2 * tm * d * 4 + (4 << 20))
    return pl.pallas_call(
        functools.partial(_ffn_ple_body, tm=tm, tiles_per_seq=tiles_per_seq, nd=nd, n_in=n_in),
        grid=(t // tm, nd),
        in_specs=[
            pl.BlockSpec((tm, d), lambda i, m: (i, 0)),
            pl.BlockSpec((hl, d), prev_blk),
            pl.BlockSpec((hl, d), next_blk),
            pl.BlockSpec((1, d), lambda i, m: (0, 0)),
            *[pl.BlockSpec((tm, y.shape[1]), lambda i, m: (i, 0)) for y in ys],
            *[pl.BlockSpec((hl, y.shape[1]), prev_blk) for y in ys],
            *[pl.BlockSpec((hl, y.shape[1]), next_blk) for y in ys],
            *[pl.BlockSpec(w.shape, lambda i, m: (0, 0)) for w in wos],
            wspec(up0, 0), wspec(up0, nf), wspec(up1, 0), wspec(up1, nf),
            cspec(pr0, 0), cspec(pr0, nf), bspec(pr0, 0), bspec(pr0, nf), dspec(pr0),
            cspec(pr1, 0), cspec(pr1, nf), bspec(pr1, 0), bspec(pr1, nf), dspec(pr1),
            pl.BlockSpec((1, d), lambda i, m: (0, 0)),
            pl.BlockSpec((d, d), lambda i, m: (0, 0)),
            pl.BlockSpec((tm, pd), lambda i, m: (i, 0)),
            pl.BlockSpec((pd, d), lambda i, m: (0, 0)),
        ],
        out_specs=pl.BlockSpec((tm, d), lambda i, m: (i, 0)),
        out_shape=jax.ShapeDtypeStruct((t, d), F32),
        scratch_shapes=[pltpu.VMEM((tm + 2 * hl, d), BF16), zbuf, zbuf, zbuf, zbuf, pltpu.VMEM((tm, tf), BF16)],
        compiler_params=_params(("parallel", "arbitrary"), vmem),
        name="conv_ffn_ple",
    )(h, h, h, norm_g.reshape(1, d), *ys, *ys, *ys, *wos,
      w_up, w_up, w_up, w_up,
      conv_w, conv_w, cb, cb, w_down,
      conv_w, conv_w, cb, cb, w_down,
      ple_g.reshape(1, d), w_gate, p, w_proj)


NA_KNORM_ROWS = 512


def _pair_head_rms(x, g, first):
    xx = x * x
    ss_a = jnp.sum(jnp.where(first, xx, 0.0), axis=-1, keepdims=True)
    ss_b = jnp.sum(jnp.where(first, 0.0, xx), axis=-1, keepdims=True)
    inv = lax.rsqrt(jnp.where(first, ss_a, ss_b) * (1.0 / NA_HEAD_DIM) + EPS)
    return x * inv * g


def _natten_body(q_ref, k_ref, v_ref, qg_ref, kg_ref, b_ref, o_ref, kn_ref, *, rows, rb):
    ib = pl.program_id(2)
    first = lax.broadcasted_iota(jnp.int32, (1, LANES), 1) < NA_HEAD_DIM
    kwin = NA_WIN_ROWS * GRID_W
    seq = rows * GRID_W

    @pl.when(ib == 0)
    def _():
        def norm_rows(c, carry):
            sl = pl.ds(pl.multiple_of(c * NA_KNORM_ROWS, NA_KNORM_ROWS), NA_KNORM_ROWS)
            kn_ref[sl, :] = _pair_head_rms(k_ref[0, sl, :].astype(F32), kg_ref[...], first).astype(BF16)
            return carry
        lax.fori_loop(0, seq // NA_KNORM_ROWS, norm_rows, 0)

    qn = _pair_head_rms(q_ref[0].astype(F32), qg_ref[...], first)

    combos = [(r, a) for r in range(rb) for a in range(2)]
    kstart, var = [], []
    for r in range(rb):
        i = ib * rb + r
        rs = jnp.clip(i - NA_WIN_ROWS // 2, 0, rows - NA_WIN_ROWS)
        var.append(i - rs)
        kstart.append(pl.multiple_of(rs * GRID_W, GRID_W))
    s = []
    for r, a in combos:
        q = qn[r * GRID_W:(r + 1) * GRID_W, :]
        keep = first if a == 0 else jnp.logical_not(first)
        qa = jnp.where(keep, q, 0.0).astype(BF16)
        k = kn_ref[pl.ds(kstart[r], kwin), :]
        s.append(lax.dot_general(qa, k, (((1,), (1,)), ((), ())), preferred_element_type=F32) + b_ref[a, var[r]])
    p, l = [], []
    for n in range(len(combos)):
        e = jnp.exp(s[n] - jnp.max(s[n], axis=-1, keepdims=True))
        l.append(jnp.sum(e, axis=-1, keepdims=True))
        p.append(e.astype(BF16))
    o = [jnp.dot(p[n], v_ref[0, pl.ds(kstart[r], kwin), :], preferred_element_type=F32) / l[n]
         for n, (r, a) in enumerate(combos)]
    for r in range(rb):
        o_ref[0, r * GRID_W:(r + 1) * GRID_W, :] = jnp.where(first, o[2 * r], o[2 * r + 1]).astype(o_ref.dtype)


def _natten_bias(rpb):
    kh, kw, w = NA_WIN_ROWS, NA_WIN_COLS, GRID_W
    var = np.arange(kh)
    jr = np.arange(kh)
    dr = jr[None, :] + (kh - 1) - var[:, None]
    sel_r = (dr[:, :, None] == np.arange(2 * kh - 1)[None, None, :]).astype(np.float32)
    col = np.arange(w)
    cstart = np.clip(col - kw // 2, 0, w - kw)
    dc = col[None, :] - col[:, None] + (kw - 1)
    valid = (col[None, :] >= cstart[:, None]) & (col[None, :] < cstart[:, None] + kw)
    sel_c = ((dc[:, :, None] == np.arange(2 * kw - 1)[None, None, :]) & valid[:, :, None]).astype(np.float32)
    b = jnp.einsum('hrd,vjr,cgd->hvcjg', rpb, jnp.asarray(sel_r), jnp.asarray(sel_c), precision=HIGHEST)
    mask = np.where(valid, 0.0, NEG_BIG).astype(np.float32)[None, None, :, None, :]
    b = b + jnp.asarray(mask)
    return b.reshape(rpb.shape[0], kh, w, kh * w)


def neighbourhood_attention(z, q_g, k_g, rpb, rb):
    bsz, seq, d3 = z.shape
    d = d3 // 3
    rows = seq // GRID_W
    bias = _natten_bias(rpb)
    hg = d // LANES
    kwin = NA_WIN_ROWS * GRID_W
    qg = jnp.tile(q_g.reshape(1, NA_HEAD_DIM), (1, 2)) * (NA_HEAD_DIM ** -0.5)
    kg = jnp.tile(k_g.reshape(1, NA_HEAD_DIM), (1, 2))
    vmem = (2 * (2 * seq * LANES * 2 + 2 * rb * GRID_W * LANES * 2 + 2 * NA_WIN_ROWS * GRID_W * kwin * 4)
            + seq * LANES * 2 + 6 * rb * GRID_W * kwin * 4 + (4 << 20))
    return pl.pallas_call(
        functools.partial(_natten_body, rows=rows, rb=rb),
        grid=(bsz, hg, rows // rb),
        in_specs=[pl.BlockSpec((1, rb * GRID_W, LANES), lambda b, h, i: (b, i, h)),
                  pl.BlockSpec((1, seq, LANES), lambda b, h, i: (b, 0, hg + h)),
                  pl.BlockSpec((1, seq, LANES), lambda b, h, i: (b, 0, 2 * hg + h)),
                  pl.BlockSpec((1, LANES), lambda b, h, i: (0, 0)),
                  pl.BlockSpec((1, LANES), lambda b, h, i: (0, 0)),
                  pl.BlockSpec((2, NA_WIN_ROWS, GRID_W, kwin), lambda b, h, i: (h, 0, 0, 0))],
        out_specs=pl.BlockSpec((1, rb * GRID_W, LANES), lambda b, h, i: (b, i, h)),
        out_shape=jax.ShapeDtypeStruct((bsz, seq, d), BF16),
        scratch_shapes=[pltpu.VMEM((seq, LANES), BF16)],
        compiler_params=_params(("parallel", "parallel", "arbitrary"), vmem),
        name="natten",
    )(z, z, z, qg, kg, bias)


def _hy_filter_body(f_ref, w1_ref, b1_ref, w2_ref, b2_ref, w3_ref, b3_ref, w4_ref, fr_ref, dl_ref,
                    h_ref, ss_ref, *, c):
    d = functools.partial(jnp.dot, preferred_element_type=F32, precision=HIGHEST)
    f = f_ref[...]
    fr = fr_ref[...]
    x = jnp.sin(fr * (d(f, w1_ref[...]) + b1_ref[...]))
    x = jnp.sin(fr * (d(x, w2_ref[...]) + b2_ref[...]))
    x = jnp.sin(fr * (d(x, w3_ref[...]) + b3_ref[...]))
    h = d(x, w4_ref[...])
    hf = h[:, 0:c] * jnp.exp(-f[:, 0:1] * dl_ref[...])
    hb = h[:, c:2 * c] * jnp.exp(-f[:, FILTER_EMB_PAD:FILTER_EMB_PAD + 1] * dl_ref[...])
    h_ref[0] = hf
    h_ref[1] = hb * f[:, FILTER_EMB_PAD + FILTER_EMB:FILTER_EMB_PAD + FILTER_EMB + 1]

    @pl.when(pl.program_id(0) == 0)
    def _():
        ss_ref[...] = jnp.zeros_like(ss_ref)

    ss_ref[...] += jnp.sum(hf * hf, axis=0, keepdims=True) + jnp.sum(hb * hb, axis=0, keepdims=True)


def hyena_filter(seq, w1, b1, w2, b2, w3, b3, w4, freq, tl):
    c = w4.shape[1] // 2
    hid = w1.shape[1]
    t = np.linspace(0.0, 1.0, seq, dtype=np.float32)[:, None]
    bands = (FILTER_EMB - 1) // 2
    ang = (np.float32(2.0 * math.pi / seq) * np.arange(seq, dtype=np.float32)[:, None]
           * np.linspace(1e-4, bands - 1, bands, dtype=np.float32)[None])
    feats = np.concatenate([t, np.cos(ang), -np.sin(ang), np.ones((seq, 1), np.float32),
                            np.zeros((seq, FILTER_EMB_PAD - FILTER_EMB - 1), np.float32)], axis=-1)
    feats_b = np.concatenate([feats[0:1], feats[:0:-1]], axis=0)
    feats_b[0, FILTER_EMB] = 0.0
    feats2 = np.concatenate([feats, feats_b], axis=1)
    pair = lambda a: jnp.concatenate([a, a], axis=-1)
    bdiag = lambda a, b: jnp.concatenate(
        [jnp.concatenate([a, jnp.zeros((a.shape[0], b.shape[1]), F32)], axis=1),
         jnp.concatenate([jnp.zeros((b.shape[0], a.shape[1]), F32), b], axis=1)], axis=0)
    w1p = jnp.concatenate([w1, jnp.zeros((FILTER_EMB_PAD - FILTER_EMB, hid), F32)], axis=0)
    deltas = np.abs(np.linspace(DECAY_MIN, DECAY_MAX, c, dtype=np.float32))[None]
    full = lambda a: pl.BlockSpec(a.shape, lambda i: (0,) * a.ndim)
    args = [jnp.asarray(feats2), bdiag(w1p, w1p), pair(b1.reshape(1, hid)), bdiag(w2, w2), pair(b2.reshape(1, hid)),
            bdiag(w3, w3), pair(b3.reshape(1, hid)), bdiag(w4[:, :c], w4[:, c:]), pair(freq.reshape(1, hid)),
            jnp.asarray(deltas)]
    in_specs = [pl.BlockSpec((tl, 2 * FILTER_EMB_PAD), lambda i: (i, 0))] + [full(a) for a in args[1:]]
    kern, ss = pl.pallas_call(
        functools.partial(_hy_filter_body, c=c),
        grid=(seq // tl,),
        in_specs=in_specs,
        out_specs=[pl.BlockSpec((2, tl, c), lambda i: (0, i, 0)), pl.BlockSpec((1, c), lambda i: (0, 0))],
        out_shape=[jax.ShapeDtypeStruct((2, seq, c), F32), jax.ShapeDtypeStruct((1, c), F32)],
        compiler_params=_params(("arbitrary",), 40 << 20),
        name="hyena_filter",
    )(*args)
    return kern.reshape(2 * seq, c), ss


def _hy_gate_body(z0_ref, z1_ref, zv_ref, w0_ref, w1_ref, wv_ref, b0_ref, b1_ref, bv_ref, s_ref, x0_ref,
                  *, seq):
    row = lax.broadcasted_iota(jnp.int32, (seq, LANES), 0)
    top = row == 0
    bot = row == seq - 1

    def conv(z_ref, w_ref, b_ref):
        z = z_ref[0]
        w = w_ref[...]
        zp = jnp.where(top, 0.0, pltpu.roll(z, 1, 0))
        zn = jnp.where(bot, 0.0, pltpu.roll(z, seq - 1, 0))
        return zp * w[0:1, :] + z * w[1:2, :] + zn * w[2:3, :] + b_ref[...]

    x0_ref[0] = conv(z0_ref, w0_ref, b0_ref)
    s_ref[0] = conv(zv_ref, wv_ref, bv_ref) * conv(z1_ref, w1_ref, b1_ref)


def hyena_gate(z, short_w, short_b, col0):
    bsz, seq, _ = z.shape
    c = HYENA_WIDTH
    nb = c // LANES
    o0 = col0 // LANES
    sb = short_b.reshape(1, 3 * c)
    zspec = lambda off: pl.BlockSpec((1, seq, LANES), lambda b, j: (b, 0, off + j))
    wspec = lambda off: pl.BlockSpec((3, LANES), lambda b, j: (0, off + j))
    bspec = lambda off: pl.BlockSpec((1, LANES), lambda b, j: (0, off + j))
    out = jax.ShapeDtypeStruct((bsz, seq, c), F32)
    return pl.pallas_call(
        functools.partial(_hy_gate_body, seq=seq),
        grid=(bsz, nb),
        in_specs=[zspec(o0), zspec(o0 + nb), zspec(o0 + 2 * nb),
                  wspec(0), wspec(nb), wspec(2 * nb), bspec(0), bspec(nb), bspec(2 * nb)],
        out_specs=[pl.BlockSpec((1, seq, LANES), lambda b, j: (b, 0, j))] * 2,
        out_shape=[out, out],
        compiler_params=_params(("parallel", "parallel"), 2 * 5 * seq * LANES * 4 + 8 * seq * LANES * 4 + (4 << 20)),
        name="hyena_gate",
    )(z, z, z, short_w, short_w, short_w, sb, sb, sb)


def _dft_consts(n, n_in):
    k = np.arange(n)[:, None].astype(np.float64)
    t = np.arange(n_in)[None, :].astype(np.float64)
    th = 2.0 * np.pi * k * t / n
    w_first = np.concatenate([np.cos(th), -np.sin(th)], axis=0)
    w_last = np.concatenate([np.cos(th).T, -np.sin(th).T], axis=1)
    tt = np.arange(n)[None, :].astype(np.float64)
    th2 = 2.0 * np.pi * k * tt / n
    cm, sm = np.cos(th2), np.sin(th2)
    m_mid = np.block([[cm, sm], [-sm, cm]])
    ph = 2.0 * np.pi * k * tt / (n * n)
    tw = np.stack([np.cos(ph), np.sin(ph)], axis=1)[..., None]
    return w_first, w_last, m_mid, tw


def _hy_stage1_body(w_ref, x_ref, o_ref, *, tb):
    w = w_ref[...]
    for j in range(tb):
        o_ref[0, j] = jnp.dot(w, x_ref[0, j], preferred_element_type=F32).astype(o_ref.dtype)


def hyena_stage(w, x, tb, out_dtype):
    bsz, n2, kk, c = x.shape
    m = w.shape[0]
    osz = jnp.dtype(out_dtype).itemsize
    vmem = 2 * (tb * kk * c * 2 + tb * m * c * osz + m * kk * 2) + 2 * m * c * 4 + (4 << 20)
    return pl.pallas_call(
        functools.partial(_hy_stage1_body, tb=tb),
        grid=(bsz, n2 // tb),
        in_specs=[pl.BlockSpec((m, kk), lambda b, j: (0, 0)),
                  pl.BlockSpec((1, tb, kk, c), lambda b, j: (b, j, 0, 0))],
        out_specs=pl.BlockSpec((1, tb, m, c), lambda b, j: (b, j, 0, 0)),
        out_shape=jax.ShapeDtypeStruct((bsz, n2, m, c), out_dtype),
        compiler_params=_params(("parallel", "parallel"), vmem),
        name="hyena_dft_outer",
    )(w, x)


def _twiddle(xr, xi, tc, ts, sign):
    return xr * tc + sign * (xi * ts), xi * tc - sign * (xr * ts)


HYENA_K1_PER_STEP = 4


def _hy_twiddled_input(a_ref, tw_ref, j, n):
    a = a_ref[0, j].astype(F32)
    xr, xi = _twiddle(a[0:n], a[n:2 * n], tw_ref[j, 0], tw_ref[j, 1], 1.0)
    return jnp.concatenate([xr, xi], axis=0).astype(BF16)


def _hy_spec_body(m_ref, tw_ref, a_ref, o_ref, *, n, kb):
    x = [_hy_twiddled_input(a_ref, tw_ref, j, n) for j in range(kb)]
    for j in range(kb):
        o_ref[0, j] = jnp.dot(m_ref[...], x[j], preferred_element_type=F32)


def _hy_mid_body(m_ref, mt_ref, tw_ref, hs_ref, a_ref, o_ref, *, n, kb):
    x = [_hy_twiddled_input(a_ref, tw_ref, j, n) for j in range(kb)]
    s = [jnp.dot(m_ref[...], x[j], preferred_element_type=F32) for j in range(kb)]
    p = []
    for j in range(kb):
        hs = hs_ref[0, j]
        sr, si = s[j][0:n], s[j][n:2 * n]
        hr, hi = hs[0:n], hs[n:2 * n]
        p.append(jnp.concatenate([sr * hr - si * hi, sr * hi + si * hr], axis=0).astype(BF16))
    y = [jnp.dot(mt_ref[...], p[j], preferred_element_type=F32) for j in range(kb)]
    for j in range(kb):
        yr, yi = _twiddle(y[j][0:n], y[j][n:2 * n], tw_ref[j, 0], tw_ref[j, 1], -1.0)
        o_ref[0, j] = jnp.concatenate([yr, yi], axis=0).astype(o_ref.dtype)


def hyena_spectrum(a, m_mid, tw):
    _, n, n2x, c = a.shape
    kb = min(HYENA_K1_PER_STEP, n)
    return pl.pallas_call(
        functools.partial(_hy_spec_body, n=n, kb=kb),
        grid=(n // kb, 1),
        in_specs=[pl.BlockSpec((n2x, n2x), lambda k, b: (0, 0)),
                  pl.BlockSpec((kb, 2, n, 1), lambda k, b: (k, 0, 0, 0)),
                  pl.BlockSpec((1, kb, n2x, c), lambda k, b: (b, k, 0, 0))],
        out_specs=pl.BlockSpec((1, kb, n2x, c), lambda k, b: (b, k, 0, 0)),
        out_shape=jax.ShapeDtypeStruct((1, n, n2x, c), F32),
        compiler_params=_params(("parallel", "parallel"), 40 << 20),
        name="hyena_filter_spectrum",
    )(m_mid, tw, a)


def hyena_mid(a, hspec, m_mid, m_mid_t, tw):
    bsz, n, n2x, c = a.shape
    kb = min(HYENA_K1_PER_STEP, n)
    return pl.pallas_call(
        functools.partial(_hy_mid_body, n=n, kb=kb),
        grid=(n // kb, bsz),
        in_specs=[pl.BlockSpec((n2x, n2x), lambda k, b: (0, 0)),
                  pl.BlockSpec((n2x, n2x), lambda k, b: (0, 0)),
                  pl.BlockSpec((kb, 2, n, 1), lambda k, b: (k, 0, 0, 0)),
                  pl.BlockSpec((1, kb, n2x, c), lambda k, b: (0, k, 0, 0)),
                  pl.BlockSpec((1, kb, n2x, c), lambda k, b: (b, k, 0, 0))],
        out_specs=pl.BlockSpec((1, kb, n2x, c), lambda k, b: (b, k, 0, 0)),
        out_shape=jax.ShapeDtypeStruct((bsz, n, n2x, c), BF16),
        compiler_params=_params(("parallel", "parallel"), 40 << 20),
        name="hyena_dft_mid",
    )(m_mid, m_mid_t, tw, hspec, a)


def _hy_combine_body(y_ref, s_ref, x0_ref, sc_ref, bi_ref, o_ref):
    o_ref[...] = ((y_ref[...] * sc_ref[...] + s_ref[...] * bi_ref[...]) * x0_ref[...]).astype(o_ref.dtype)


def hyena_combine(y, s, x0, scale, bias, tm):
    t, c = y.shape
    row = lambda: pl.BlockSpec((tm, c), lambda i: (i, 0))
    vec = lambda: pl.BlockSpec((1, c), lambda i: (0, 0))
    return pl.pallas_call(
        _hy_combine_body,
        grid=(t // tm,),
        in_specs=[row(), row(), row(), vec(), vec()],
        out_specs=row(),
        out_shape=jax.ShapeDtypeStruct((t, c), BF16),
        compiler_params=_params(("parallel",), 2 * 4 * tm * c * 4 + (4 << 20)),
        name="hyena_combine",
    )(y, s, x0, scale.reshape(1, c), bias.reshape(1, c))


def hyena_bidir(z, col0, short_w, short_b, f_w1, f_b1, f_w2, f_b2, f_w3, f_b3, f_w4, f_freq, bias):
    bsz, seq, _ = z.shape
    c = HYENA_WIDTH
    n = int(round(math.sqrt(2 * seq)))
    assert n * n == 2 * seq and n % 16 == 0
    nh = n // 2
    tb = 8

    w_first, w_last, m_mid, tw = _dft_consts(n, n)
    w_first_x = jnp.asarray(w_first[:, :nh], BF16)
    w_first_h = jnp.asarray(w_first, BF16)
    w_last_y = jnp.asarray(w_last[:nh], BF16)
    m_mid_j = jnp.asarray(m_mid, BF16)
    m_mid_t = jnp.asarray(m_mid.T, BF16)
    tw_j = jnp.asarray(tw, F32)

    kern, ss = hyena_filter(seq, f_w1, f_b1, f_w2, f_b2, f_w3, f_b3, f_w4, f_freq, tl=min(seq, 1024))
    scale = lax.rsqrt(ss[0] + EPS) * (1.0 / (2 * seq))
    kern_t = jnp.transpose(kern.reshape(1, n, n, c), (0, 2, 1, 3)).astype(BF16)
    ah = hyena_stage(w_first_h, kern_t, tb, BF16)
    ah = jnp.transpose(ah.reshape(1, n, 2, n, c), (0, 3, 2, 1, 4)).reshape(1, n, 2 * n, c)
    hspec = hyena_spectrum(ah, m_mid_j, tw_j)

    s, x0 = hyena_gate(z, short_w, short_b, col0)
    s_t = jnp.transpose(s.reshape(bsz, nh, n, c), (0, 2, 1, 3)).astype(BF16)
    a = hyena_stage(w_first_x, s_t, tb, BF16)
    a = jnp.transpose(a.reshape(bsz, n, 2, n, c), (0, 3, 2, 1, 4)).reshape(bsz, n, 2 * n, c)
    bm = hyena_mid(a, hspec, m_mid_j, m_mid_t, tw_j)
    bm = jnp.transpose(bm.reshape(bsz, n, 2, n, c), (0, 3, 2, 1, 4)).reshape(bsz, n, 2 * n, c)
    y = hyena_stage(w_last_y, bm, tb, F32)
    y = jnp.transpose(y, (0, 2, 1, 3)).reshape(bsz * seq, c)
    return hyena_combine(y, s.reshape(bsz * seq, c), x0.reshape(bsz * seq, c), scale, bias,
                         tm=min(bsz * seq, 2048))


RWKV_PAIRS = RWKV_WIDTH // LANES
RWKV_PREP_CHUNKS = 4


def _pair_rows(x, first):
    return jnp.concatenate([jnp.where(first, x, 0.0), jnp.where(first, 0.0, x)], axis=0)


def _fold_rows(x2, ch):
    return x2[0:ch] + x2[ch:2 * ch]


def _rwkv_prep_body(z_ref, halo_ref, mu_ref, w0_ref, wup_ref, a0_ref, aup_ref, kk_ref, ka_ref, rk_ref, sel_ref,
                    wt_ref, rt_ref, kb_ref, ab_ref, v_ref, ut_ref, yk_ref, ara_ref, ge_ref, bonus_ref,
                    *, reverse, nsteps, cps):
    ch = RWKV_CHUNK
    cw = RWKV_WIDTH
    rows = cps * ch
    c_idx = pl.program_id(1)

    zs = z_ref[0]
    row = lax.broadcasted_iota(jnp.int32, (rows, 1), 0)
    if reverse:
        nbr = jnp.where(c_idx == nsteps - 1, 0.0, halo_ref[0, 0:1, :])
        shifted = jnp.where(row == rows - 1, jnp.broadcast_to(nbr, zs.shape), pltpu.roll(zs, rows - 1, 0))
    else:
        nbr = jnp.where(c_idx == 0, 0.0, halo_ref[0, SUBLANES - 1:SUBLANES, :])
        shifted = jnp.where(row == 0, jnp.broadcast_to(nbr, zs.shape), pltpu.roll(zs, 1, 0))
    zd = zs + (shifted - zs) * mu_ref[...]

    r = zd[:, 0:cw]
    k = zd[:, cw:2 * cw]
    v = zd[:, 2 * cw:3 * cw]
    lora = zd[:, 3 * cw:3 * cw + DECAY_RANK + ICLR_RANK]
    x = w0_ref[...] + _bdot(jnp.tanh(lora), wup_ref[...])
    log_w = jnp.minimum(x, 0.0) - jnp.log(1.0 + jnp.exp(-jnp.abs(x))) - 0.5
    lnw = -jnp.exp(log_w)
    a = jax.nn.sigmoid(a0_ref[...] + _bdot(lora, aup_ref[...]))
    sel = sel_ref[...]
    kkr = k * kk_ref[...]
    kkn = jnp.sqrt(_bdot(kkr * kkr, sel))
    kk = kkr / jnp.maximum(kkn, 1e-12)
    k2 = k * (1.0 + (a - 1.0) * ka_ref[...])
    ah = kk * a
    bonus_ref[0] = (_bdot(r * k2 * rk_ref[...], sel) * v).astype(bonus_ref.dtype)

    ri = lax.broadcasted_iota(jnp.int32, (rows, rows), 0)
    ci = lax.broadcasted_iota(jnp.int32, (rows, rows), 1)
    csh = ch.bit_length() - 1
    tri = ((ci >= ri) if reverse else (ci <= ri)) & ((ri >> csh) == (ci >> csh))
    cs = _dot_exact_lhs(tri.astype(BF16), lnw)
    total = jnp.concatenate(
        [jnp.broadcast_to(jnp.sum(lnw[g * ch:(g + 1) * ch], axis=0, keepdims=True), (ch, cw)) for g in range(cps)],
        axis=0)
    e_inv = jnp.exp(-cs)
    e_end = jnp.exp(total - cs)
    rt = r * jnp.exp(cs)
    kkt = kk * jnp.exp(cs - lnw)
    kh = k2 * e_inv
    ahh = ah * e_inv
    rt_ref[0] = rt.astype(rt_ref.dtype)
    kb_ref[0] = (k2 * e_end).astype(kb_ref.dtype)
    ab_ref[0] = (ah * e_end).astype(ab_ref.dtype)
    v_ref[0] = v.astype(v_ref.dtype)
    for g in range(cps):
        ge_ref[0, g] = jnp.exp(total[g * ch:g * ch + 1])

    r2 = lax.broadcasted_iota(jnp.int32, (2 * ch, 2 * ch), 0)
    c2 = lax.broadcasted_iota(jnp.int32, (2 * ch, 2 * ch), 1)
    same_head = (r2 < ch) == (c2 < ch)
    si = r2 & (ch - 1)
    sj = c2 & (ch - 1)
    before = (sj > si) if reverse else (sj < si)
    strict = same_head & before
    incl = same_head & jnp.logical_or(before, si == sj)
    eye = (r2 == c2).astype(F32)
    first = lax.broadcasted_iota(jnp.int32, (1, LANES), 1) < RWKV_HEAD_DIM

    units = [(g, p) for g in range(cps) for p in range(RWKV_PAIRS)]
    un = range(len(units))
    tile = lambda arr, g, p: arr[g * ch:(g + 1) * ch, p * LANES:(p + 1) * LANES]
    l2 = [_pair_rows(tile(kkt, g, p), first) for g, p in units]
    l4 = [jnp.concatenate([l2[n], _pair_rows(tile(rt, g, p), first)], axis=0) for n, (g, p) in enumerate(units)]
    v2 = [_pair_rows(tile(v, g, p), first) for g, p in units]
    sk = [_bdot_nt(l4[n], jnp.concatenate([tile(kh, g, p)] * 2, axis=0)) for n, (g, p) in enumerate(units)]
    sa = [_bdot_nt(l4[n], jnp.concatenate([tile(ahh, g, p)] * 2, axis=0)) for n, (g, p) in enumerate(units)]
    nm = [jnp.where(strict, sa[n][0:2 * ch], 0.0) for n in un]
    for n, (g, p) in enumerate(units):
        ara_ref[0, g, p] = jnp.where(incl, sa[n][2 * ch:4 * ch], 0.0).astype(ara_ref.dtype)
    akr = [jnp.concatenate([jnp.where(strict, sk[n][0:2 * ch], 0.0),
                            jnp.where(incl, sk[n][2 * ch:4 * ch], 0.0)], axis=0) for n in un]
    av = [_bdot(akr[n], v2[n]) for n in un]

    lvl = (si >> 1) == (sj >> 1)
    t = [eye - jnp.where(lvl, nm[n], 0.0) for n in un]
    s = 2
    while s < ch:
        sh = s.bit_length() - 1
        lvl = ((si >> (sh + 1)) == (sj >> (sh + 1))) & ((si >> sh) != (sj >> sh))
        x = [_bdot(jnp.where(lvl, nm[n], 0.0), t[n]) for n in un]
        t = [t[n] - _bdot(t[n], x[n]) for n in un]
        s *= 2

    wu = [_bdot(t[n], jnp.concatenate([l2[n], av[n][0:2 * ch]], axis=1)) for n in un]
    gather = lambda f: jnp.concatenate(
        [jnp.concatenate([f(g * RWKV_PAIRS + p) for p in range(RWKV_PAIRS)], axis=1) for g in range(cps)], axis=0)
    wt_ref[0] = gather(lambda n: _fold_rows(wu[n][:, 0:LANES], ch)).astype(wt_ref.dtype)
    ut_ref[0] = gather(lambda n: _fold_rows(wu[n][:, LANES:2 * LANES], ch)).astype(ut_ref.dtype)
    yk_ref[0] = gather(lambda n: _fold_rows(av[n][2 * ch:4 * ch], ch)).astype(yk_ref.dtype)


def rwkv7_prep(z, mu, w0, w_up, a0, a_up, k_k, k_a, r_k, reverse):
    bsz, seq, _ = z.shape
    ch = RWKV_CHUNK
    cw = RWKV_WIDTH
    nc = seq // ch
    cps = min(RWKV_PREP_CHUNKS, nc)
    nsteps = nc // cps
    cpb = cps * ch // SUBLANES
    nhb = seq // SUBLANES
    sel_np = (np.arange(cw)[:, None] // RWKV_HEAD_DIM == np.arange(cw)[None, :] // RWKV_HEAD_DIM)
    sel = jnp.asarray(sel_np, BF16)
    zero_pad = jnp.zeros((DECAY_RANK, cw), F32)
    wup = jnp.concatenate([w_up, zero_pad], axis=0).astype(BF16)
    aup = jnp.concatenate([zero_pad, a_up], axis=0).astype(BF16)
    vec = lambda a: a.reshape(1, -1)
    params = [vec(mu), vec(w0), wup, vec(a0), aup, vec(k_k), vec(k_a), vec(r_k), sel]
    full = lambda a: pl.BlockSpec(a.shape, lambda b, c: (0,) * a.ndim)
    if reverse:
        halo = pl.BlockSpec((1, SUBLANES, RWKV_SHIFT_COLS), lambda b, c: (b, jnp.minimum((c + 1) * cpb, nhb - 1), 0))
    else:
        halo = pl.BlockSpec((1, SUBLANES, RWKV_SHIFT_COLS), lambda b, c: (b, jnp.maximum(c * cpb - 1, 0), 0))
    tok = pl.BlockSpec((1, cps * ch, cw), lambda b, c: (b, c, 0))
    tok_bf = jax.ShapeDtypeStruct((bsz, seq, cw), BF16)
    return pl.pallas_call(
        functools.partial(_rwkv_prep_body, reverse=reverse, nsteps=nsteps, cps=cps),
        grid=(bsz, nsteps),
        in_specs=[pl.BlockSpec((1, cps * ch, RWKV_SHIFT_COLS), lambda b, c: (b, c, 0)), halo]
                 + [full(a) for a in params],
        out_specs=[tok] * 7
                  + [pl.BlockSpec((1, cps, RWKV_PAIRS, 2 * ch, 2 * ch), lambda b, c: (b, c, 0, 0, 0)),
                     pl.BlockSpec((1, cps, 1, cw), lambda b, c: (b, c, 0, 0)),
                     tok],
        out_shape=[tok_bf, tok_bf, tok_bf, tok_bf, tok_bf, tok_bf, tok_bf,
                   jax.ShapeDtypeStruct((bsz, nc, RWKV_PAIRS, 2 * ch, 2 * ch), BF16),
                   jax.ShapeDtypeStruct((bsz, nc, 1, cw), F32),
                   tok_bf],
        compiler_params=_params(("parallel", "parallel"), 40 << 20),
        name="rwkv7_prep_bwd" if reverse else "rwkv7_prep_fwd",
    )(z, z, *params)


def _rwkv_scan_body(*refs, bsz, final):
    if final:
        (wt_ref, rt_ref, kb_ref, ab_ref, v_ref, ut_ref, yk_ref, ara_ref, ge_ref,
         yin_ref, b1_ref, b2_ref, gd_ref, gup_ref, lnw_ref, lnb_ref, sel_ref, o_ref, st_ref) = refs
    else:
        (wt_ref, rt_ref, kb_ref, ab_ref, v_ref, ut_ref, yk_ref, ara_ref, ge_ref, o_ref, st_ref) = refs
    ch = RWKV_CHUNK

    @pl.when(pl.program_id(0) == 0)
    def _():
        st_ref[...] = jnp.zeros_like(st_ref)

    first = lax.broadcasted_iota(jnp.int32, (1, LANES), 1) < RWKV_HEAD_DIM
    blk = ((lax.broadcasted_iota(jnp.int32, (LANES, LANES), 0) < RWKV_HEAD_DIM)
           == (lax.broadcasted_iota(jnp.int32, (LANES, LANES), 1) < RWKV_HEAD_DIM))
    idx = [(b, p) for b in range(bsz) for p in range(RWKV_PAIRS)]
    sl = [slice(p * LANES, (p + 1) * LANES) for p in range(RWKV_PAIRS)]
    st = [st_ref[b * RWKV_PAIRS + p] for b, p in idx]
    lm0 = [_bdot_nt(jnp.concatenate([wt_ref[b, :, sl[p]], rt_ref[b, :, sl[p]]], axis=0), st[n])
           for n, (b, p) in enumerate(idx)]
    u = [lm0[n][0:ch] + ut_ref[b, :, sl[p]] for n, (b, p) in enumerate(idx)]
    yr = [_bdot(ara_ref[b, 0, p], _pair_rows(u[n], first)) for n, (b, p) in enumerate(idx)]
    upd = [_bdot_tn(jnp.concatenate([v_ref[b, :, sl[p]].astype(F32), -u[n]], axis=0),
                    jnp.concatenate([kb_ref[b, :, sl[p]], ab_ref[b, :, sl[p]]], axis=0))
           for n, (b, p) in enumerate(idx)]
    for n, (b, p) in enumerate(idx):
        st_ref[b * RWKV_PAIRS + p] = st[n] * ge_ref[b, 0, :, sl[p]] + jnp.where(blk, upd[n], 0.0)
    for b in range(bsz):
        y = jnp.concatenate(
            [lm0[b * RWKV_PAIRS + p][ch:2 * ch] + yk_ref[b, :, sl[p]] - _fold_rows(yr[b * RWKV_PAIRS + p], ch)
             for p in range(RWKV_PAIRS)], axis=1)
        if not final:
            o_ref[b] = y.astype(o_ref.dtype)
        else:
            sel = sel_ref[...]
            yt = y + yin_ref[b]
            mean = _bdot(yt, sel) * (1.0 / RWKV_HEAD_DIM)
            dv = yt - mean
            var = _bdot(dv * dv, sel) * (1.0 / RWKV_HEAD_DIM)
            yn = dv * lax.rsqrt(var + RWKV_GN_EPS) * lnw_ref[...] + lnb_ref[...]
            g = _bdot(jax.nn.sigmoid(gd_ref[b]), gup_ref[...])
            o_ref[b] = ((yn + b1_ref[b] + b2_ref[b]) * g).astype(o_ref.dtype)


def rwkv7_scan(prep, reverse, final_args=None):
    wt, rt, kb, ab, v, ut, yk, ara, ge, _ = prep
    bsz, seq, cw = wt.shape
    ch = RWKV_CHUNK
    nc = seq // ch
    cidx = (lambda c: nc - 1 - c) if reverse else (lambda c: c)
    tok = pl.BlockSpec((bsz, ch, cw), lambda c: (0, cidx(c), 0))
    in_specs = [tok] * 7 + [pl.BlockSpec((bsz, 1, RWKV_PAIRS, 2 * ch, 2 * ch), lambda c: (0, cidx(c), 0, 0, 0)),
                            pl.BlockSpec((bsz, 1, 1, cw), lambda c: (0, cidx(c), 0, 0))]
    args = [wt, rt, kb, ab, v, ut, yk, ara, ge]
    final = final_args is not None
    if final:
        z, y_in, bonus_a, bonus_b, g_up, ln_w, ln_b = final_args
        sel_np = (np.arange(cw)[:, None] // RWKV_HEAD_DIM == np.arange(cw)[None, :] // RWKV_HEAD_DIM)
        full = lambda a: pl.BlockSpec(a.shape, lambda c: (0,) * a.ndim)
        extra = [g_up.astype(BF16), ln_w.reshape(1, cw), ln_b.reshape(1, cw), jnp.asarray(sel_np, BF16)]
        in_specs += [tok, tok, tok,
                     pl.BlockSpec((bsz, ch, GATE_RANK), lambda c: (0, cidx(c), RWKV_SHIFT_COLS // GATE_RANK))]
        in_specs += [full(a) for a in extra]
        args += [y_in, bonus_a, bonus_b, z] + extra
    out_dtype = BF16
    return pl.pallas_call(
        functools.partial(_rwkv_scan_body, bsz=bsz, final=final),
        grid=(nc,),
        in_specs=in_specs,
        out_specs=tok,
        out_shape=jax.ShapeDtypeStruct((bsz, seq, cw), out_dtype),
        scratch_shapes=[pltpu.VMEM((bsz * RWKV_PAIRS, LANES, LANES), F32)],
        compiler_params=_params(("arbitrary",), 40 << 20),
        name="rwkv7_scan_bwd" if reverse else "rwkv7_scan_fwd",
    )(*args)


def rwkv7_bidir(z, mu, w0, w_up, a0, a_up, g_up, k_k, k_a, r_k, ln_w, ln_b):
    rk = r_k.reshape(-1)
    prep_f = rwkv7_prep(z, mu[0], w0[0], w_up[0], a0[0], a_up[0], k_k, k_a, rk, reverse=False)
    prep_b = rwkv7_prep(z, mu[1], w0[1], w_up[1], a0[1], a_up[1], k_k, k_a, rk, reverse=True)
    y_f = rwkv7_scan(prep_f, reverse=False)
    return rwkv7_scan(prep_b, reverse=True, final_args=(z, y_f, prep_f[9], prep_b[9], g_up, ln_w, ln_b))


def kernel(x, p, mix_norm, mix_w_in, rwkv_mu, rwkv_w0, rwkv_w_up, rwkv_a0, rwkv_a_up, rwkv_g_up, rwkv_k_k, rwkv_k_a, rwkv_r_k, rwkv_ln_w, rwkv_ln_b, hy_short_w, hy_short_b, hy_f_w1, hy_f_b1, hy_f_w2, hy_f_b2, hy_f_w3, hy_f_b3, hy_f_w4, hy_f_freq, hy_bias, mix_w_out, na_norm, na_w_qkv, na_q_g, na_k_g, na_rpb, na_w_out, ffn_norm, ffn_w_up, ffn_conv_w, ffn_conv_b, ffn_w_down, ple_norm, ple_w_gate, ple_w_proj):
    bsz, seq, d = x.shape
    depth = p.shape[0]
    t = bsz * seq
    tm = min(1024, seq)
    h = x.reshape(t, d)
    for i in range(depth):
        j = i // 2
        if i % 2 == 0:
            w_in = mix_w_in[j].astype(BF16)
            z = norm_matmul(h, mix_norm[j], w_in, tm=tm, tn=w_in.shape[1] // 2).reshape(bsz, seq, -1)
            y_a = rwkv7_bidir(z, rwkv_mu[j], rwkv_w0[j], rwkv_w_up[j], rwkv_a0[j], rwkv_a_up[j], rwkv_g_up[j],
                              rwkv_k_k[j], rwkv_k_a[j], rwkv_r_k[j], rwkv_ln_w[j], rwkv_ln_b[j])
            y_b = hyena_bidir(z, RWKV_COLS, hy_short_w[j], hy_short_b[j], hy_f_w1[j], hy_f_b1[j], hy_f_w2[j],
                              hy_f_b2[j], hy_f_w3[j], hy_f_b3[j], hy_f_w4[j], hy_f_freq[j], hy_bias[j])
            w_out = mix_w_out[j].astype(BF16)
            ys, wos = [y_a.reshape(t, -1), y_b], [w_out[:RWKV_WIDTH], w_out[RWKV_WIDTH:]]
        else:
            z = norm_matmul(h, na_norm[j], na_w_qkv[j].astype(BF16), tm=tm, tn=1024, out_dtype=BF16)
            o = neighbourhood_attention(z.reshape(bsz, seq, 3 * d), na_q_g[j], na_k_g[j], na_rpb[j], rb=32)
            ys, wos = [o.reshape(t, d)], [na_w_out[j].astype(BF16)]
        h = conv_ffn_ple(h, ys, wos, ffn_norm[i], ffn_w_up[i].astype(BF16), ffn_conv_w[i], ffn_conv_b[i],
                         ffn_w_down[i].astype(BF16), p[i].reshape(t, -1), ple_norm[i],
                         ple_w_gate[i].astype(BF16), ple_w_proj[i].astype(BF16), seq_len=seq, tm=tm, tf=256)
    return h.reshape(bsz, seq, d)
```

```python
import functools
import math

import numpy as np
import jax
import jax.numpy as jnp
from jax import lax
from jax.experimental import pallas as pl
from jax.experimental.pallas import tpu as pltpu

F32 = jnp.float32
BF16 = jnp.bfloat16
HIGHEST = lax.Precision.HIGHEST

EPS = 1e-6
GRID_W = 64
RWKV_HEADS = 8
RWKV_HEAD_DIM = 64
RWKV_WIDTH = RWKV_HEADS * RWKV_HEAD_DIM
DECAY_RANK = 64
ICLR_RANK = 64
GATE_RANK = 128
RWKV_SHIFT_COLS = 3 * RWKV_WIDTH + DECAY_RANK + ICLR_RANK
RWKV_COLS = RWKV_SHIFT_COLS + GATE_RANK
RWKV_GN_EPS = 64e-5
RWKV_CHUNK = 64
HYENA_WIDTH = 512
FILTER_EMB = 17
FILTER_EMB_PAD = 32
DECAY_MIN = math.log(1e-2) / 1.5
DECAY_MAX = math.log(1e-2) / 0.3
NA_HEADS = 16
NA_HEAD_DIM = 64
NA_WIN_ROWS = 8
NA_WIN_COLS = 16
NEG_BIG = -1e30

LANES = 128
SUBLANES = 8
V7X_VMEM_BYTES = 64 * 1024 * 1024
VMEM_CAP = V7X_VMEM_BYTES - 8 * 1024 * 1024


def _params(semantics, vmem_bytes):
    return pltpu.CompilerParams(dimension_semantics=semantics,
                                vmem_limit_bytes=int(min(max(vmem_bytes, 16 * 1024 * 1024), VMEM_CAP)))


def _rms_rows(x, g):
    ms = jnp.mean(x * x, axis=-1, keepdims=True)
    return x * lax.rsqrt(ms + EPS) * g


def _bdot(a, b):
    return jnp.dot(a.astype(BF16), b.astype(BF16), preferred_element_type=F32)


def _bdot_nt(a, b):
    return lax.dot_general(a.astype(BF16), b.astype(BF16), (((1,), (1,)), ((), ())),
                           preferred_element_type=F32)


def _bdot_tn(a, b):
    return lax.dot_general(a.astype(BF16), b.astype(BF16), (((0,), (0,)), ((), ())),
                           preferred_element_type=F32)


def _dot_exact_lhs(a_bf16, b):
    d = functools.partial(jnp.dot, preferred_element_type=F32)
    b1 = b.astype(BF16)
    r1 = b - b1.astype(F32)
    b2 = r1.astype(BF16)
    b3 = (r1 - b2.astype(F32)).astype(BF16)
    return d(a_bf16, b1) + (d(a_bf16, b2) + d(a_bf16, b3))


def _norm_matmul_body(x_ref, g_ref, w_ref, o_ref, xn_ref):
    @pl.when(pl.program_id(1) == 0)
    def _():
        xn_ref[...] = _rms_rows(x_ref[...], g_ref[...]).astype(BF16)

    o_ref[...] = jnp.dot(xn_ref[...], w_ref[...], preferred_element_type=F32).astype(o_ref.dtype)


def norm_matmul(x, g, w, tm, tn, out_dtype=F32):
    t, d = x.shape
    n = w.shape[1]
    osz = jnp.dtype(out_dtype).itemsize
    vmem = 2 * (tm * d * 4 + d * tn * 2 + tm * tn * osz) + tm * d * 2 + (4 << 20)
    return pl.pallas_call(
        _norm_matmul_body,
        grid=(t // tm, n // tn),
        in_specs=[pl.BlockSpec((tm, d), lambda i, j: (i, 0)),
                  pl.BlockSpec((1, d), lambda i, j: (0, 0)),
                  pl.BlockSpec((d, tn), lambda i, j: (0, j))],
        out_specs=pl.BlockSpec((tm, tn), lambda i, j: (i, j)),
        out_shape=jax.ShapeDtypeStruct((t, n), out_dtype),
        scratch_shapes=[pltpu.VMEM((tm, d), BF16)],
        compiler_params=_params(("parallel", "arbitrary"), vmem),
        name="norm_matmul",
    )(x, g.reshape(1, d), w)


FFN_HALO = 16
FFN_ROW_BLOCK = 64


def _ffn_ple_body(*refs, tm, tiles_per_seq, nd, n_in):
    xm_ref, xp_ref, xx_ref, g_ref = refs[0:4]
    ym, yp, yx, wo = (refs[4 + k * n_in:4 + (k + 1) * n_in] for k in range(4))
    (wa0_ref, wb0_ref, wa1_ref, wb1_ref,
     cwa0_ref, cwb0_ref, cba0_ref, cbb0_ref, wd0_ref,
     cwa1_ref, cwb1_ref, cba1_ref, cbb1_ref, wd1_ref,
     pg_ref, wg_ref, p_ref, wp_ref,
     o_ref, xn_ref, za0_ref, zb0_ref, za1_ref, zb1_ref, act_ref) = refs[4 + 4 * n_in:]
    i = pl.program_id(0)
    m = pl.program_id(1)
    hl = FFN_HALO

    def prologue():
        def mixed(x_ref, ys):
            acc = x_ref[...]
            for y_ref, w_ref in zip(ys, wo):
                acc = acc + jnp.dot(y_ref[...], w_ref[...], preferred_element_type=F32)
            return acc

        g = g_ref[...]
        first = (i % tiles_per_seq) == 0
        last = (i % tiles_per_seq) == tiles_per_seq - 1
        xm = mixed(xm_ref, ym)
        xn_ref[hl:hl + tm, :] = _rms_rows(xm, g).astype(BF16)
        xn_ref[0:hl, :] = jnp.where(first, 0.0, _rms_rows(mixed(xp_ref, yp), g)).astype(BF16)
        xn_ref[hl + tm:2 * hl + tm, :] = jnp.where(last, 0.0, _rms_rows(mixed(xx_ref, yx), g)).astype(BF16)
        o_ref[...] = xm

    def conv(z_ref, r0, cw, cb):
        return (z_ref[r0 - 1:r0 - 1 + FFN_ROW_BLOCK, :] * cw[0:1, :] + z_ref[r0:r0 + FFN_ROW_BLOCK, :] * cw[1:2, :]
                + z_ref[r0 + 1:r0 + 1 + FFN_ROW_BLOCK, :] * cw[2:3, :] + cb)

    def contribution(za_ref, zb_ref, cwa_ref, cwb_ref, cba_ref, cbb_ref, wd_ref):
        cwa, cwb, cba, cbb = cwa_ref[...], cwb_ref[...], cba_ref[...], cbb_ref[...]
        for blk in range(tm // FFN_ROW_BLOCK):
            r0 = hl + blk * FFN_ROW_BLOCK
            act = jax.nn.gelu(conv(za_ref, r0, cwa, cba)) * conv(zb_ref, r0, cwb, cbb)
            act_ref[blk * FFN_ROW_BLOCK:(blk + 1) * FFN_ROW_BLOCK, :] = act.astype(BF16)
        return jnp.dot(act_ref[...], wd_ref[...], preferred_element_type=F32)

    @pl.when(m == 0)
    def _():
        prologue()
        za1_ref[...] = jnp.zeros_like(za1_ref)
        zb1_ref[...] = jnp.zeros_like(zb1_ref)

    xn = xn_ref[...]
    za0_ref[...] = jnp.dot(xn, wa0_ref[...], preferred_element_type=F32)
    zb0_ref[...] = jnp.dot(xn, wb0_ref[...], preferred_element_type=F32)
    ca = contribution(za1_ref, zb1_ref, cwa0_ref, cwb0_ref, cba0_ref, cbb0_ref, wd0_ref)
    o_ref[...] += jnp.where(m > 0, ca, 0.0)
    za1_ref[...] = jnp.dot(xn, wa1_ref[...], preferred_element_type=F32)
    zb1_ref[...] = jnp.dot(xn, wb1_ref[...], preferred_element_type=F32)
    o_ref[...] += contribution(za0_ref, zb0_ref, cwa1_ref, cwb1_ref, cba1_ref, cbb1_ref, wd1_ref)

    @pl.when(m == nd - 1)
    def _():
        h2 = o_ref[...]
        gate = jax.nn.sigmoid(jnp.dot(_rms_rows(h2, pg_ref[...]).astype(BF16), wg_ref[...],
                                      preferred_element_type=F32))
        proj = jnp.dot(p_ref[...].astype(BF16), wp_ref[...], preferred_element_type=F32)
        o_ref[...] = h2 + gate * proj


def conv_ffn_ple(h, ys, wos, norm_g, w_up, conv_w, conv_b, w_down, p, ple_g, w_gate, w_proj, seq_len, tm, tf):
    t, d = h.shape
    n_in = len(ys)
    f = w_down.shape[0]
    pd = p.shape[1]
    nf = f // tf
    assert nf % 2 == 1
    nd = (nf + 1) // 2
    hl = FFN_HALO
    tiles_per_seq = seq_len // tm
    nhb = t // hl
    last = nf - 1
    up0 = lambda m: jnp.minimum(2 * m, last)
    up1 = lambda m: jnp.minimum(2 * m + 1, last)
    pr0 = lambda m: jnp.maximum(2 * m - 1, 0)
    pr1 = lambda m: 2 * m
    wspec = lambda fn, off: pl.BlockSpec((d, tf), lambda i, m: (0, off + fn(m)))
    cspec = lambda fn, off: pl.BlockSpec((3, tf), lambda i, m: (0, off + fn(m)))
    bspec = lambda fn, off: pl.BlockSpec((1, tf), lambda i, m: (0, off + fn(m)))
    dspec = lambda fn: pl.BlockSpec((tf, d), lambda i, m: (fn(m), 0))
    cb = conv_b.reshape(1, 2 * f)
    zbuf = pltpu.VMEM((tm + 2 * hl, tf), F32)
    prev_blk = lambda i, m: (jnp.maximum(i * (tm // hl) - 1, 0), 0)
    next_blk = lambda i, m: (jnp.minimum((i + 1) * (tm // hl), nhb - 1), 0)
    ksum = sum(y.shape[1] for y in ys)
    vmem = (2 * (2 * tm * d * 4 + 2 * hl * d * 4 + 4 * d * tf * 2 + 2 * tf * d * 2 + tm * pd * 4 + d * d * 2 + pd * d * 2
                 + (tm + 2 * hl) * ksum * 2 + ksum * d * 2)
            + (tm + 2 * hl) * d * 2 + 4 * (tm + 2 * hl) * tf * 4 + 8 * tm * tf * 4 + 2 * tm * d * 4 + (4 << 20))
    return pl.pallas_call(
        functools.partial(_ffn_ple_body, tm=tm, tiles_per_seq=tiles_per_seq, nd=nd, n_in=n_in),
        grid=(t // tm, nd),
        in_specs=[
            pl.BlockSpec((tm, d), lambda i, m: (i, 0)),
            pl.BlockSpec((hl, d), prev_blk),
            pl.BlockSpec((hl, d), next_blk),
            pl.BlockSpec((1, d), lambda i, m: (0, 0)),
            *[pl.BlockSpec((tm, y.shape[1]), lambda i, m: (i, 0)) for y in ys],
            *[pl.BlockSpec((hl, y.shape[1]), prev_blk) for y in ys],
            *[pl.BlockSpec((hl, y.shape[1]), next_blk) for y in ys],
            *[pl.BlockSpec(w.shape, lambda i, m: (0, 0)) for w in wos],
            wspec(up0, 0), wspec(up0, nf), wspec(up1, 0), wspec(up1, nf),
            cspec(pr0, 0), cspec(pr0, nf), bspec(pr0, 0), bspec(pr0, nf), dspec(pr0),
            cspec(pr1, 0), cspec(pr1, nf), bspec(pr1, 0), bspec(pr1, nf), dspec(pr1),
            pl.BlockSpec((1, d), lambda i, m: (0, 0)),
            pl.BlockSpec((d, d), lambda i, m: (0, 0)),
            pl.BlockSpec((tm, pd), lambda i, m: (i, 0)),
            pl.BlockSpec((pd, d), lambda i, m: (0, 0)),
        ],
        out_specs=pl.BlockSpec((tm, d), lambda i, m: (i, 0)),
        out_shape=jax.ShapeDtypeStruct((t, d), F32),
        scratch_shapes=[pltpu.VMEM((tm + 2 * hl, d), BF16), zbuf, zbuf, zbuf, zbuf, pltpu.VMEM((tm, tf), BF16)],
        compiler_params=_params(("parallel", "arbitrary"), vmem),
        name="conv_ffn_ple",
    )(h, h, h, norm_g.reshape(1, d), *ys, *ys, *ys, *wos,
      w_up, w_up, w_up, w_up,
      conv_w, conv_w, cb, cb, w_down,
      conv_w, conv_w, cb, cb, w_down,
      ple_g.reshape(1, d), w_gate, p, w_proj)


NA_KNORM_ROWS = 512


def _pair_head_rms(x, g, first):
    xx = x * x
    ss_a = jnp.sum(jnp.where(first, xx, 0.0), axis=-1, keepdims=True)
    ss_b = jnp.sum(jnp.where(first, 0.0, xx), axis=-1, keepdims=True)
    inv = lax.rsqrt(jnp.where(first, ss_a, ss_b) * (1.0 / NA_HEAD_DIM) + EPS)
    return x * inv * g


def _natten_body(q_ref, k_ref, v_ref, qg_ref, kg_ref, b_ref, o_ref, kn_ref, *, rows, rb):
    ib = pl.program_id(2)
    first = lax.broadcasted_iota(jnp.int32, (1, LANES), 1) < NA_HEAD_DIM
    kwin = NA_WIN_ROWS * GRID_W
    seq = rows * GRID_W

    @pl.when(ib == 0)
    def _():
        def norm_rows(c, carry):
            sl = pl.ds(pl.multiple_of(c * NA_KNORM_ROWS, NA_KNORM_ROWS), NA_KNORM_ROWS)
            kn_ref[sl, :] = _pair_head_rms(k_ref[0, sl, :].astype(F32), kg_ref[...], first).astype(BF16)
            return carry
        lax.fori_loop(0, seq // NA_KNORM_ROWS, norm_rows, 0)

    qn = _pair_head_rms(q_ref[0].astype(F32), qg_ref[...], first)

    combos = [(r, a) for r in range(rb) for a in range(2)]
    kstart, var = [], []
    for r in range(rb):
        i = ib * rb + r
        rs = jnp.clip(i - NA_WIN_ROWS // 2, 0, rows - NA_WIN_ROWS)
        var.append(i - rs)
        kstart.append(pl.multiple_of(rs * GRID_W, GRID_W))
    s = []
    for r, a in combos:
        q = qn[r * GRID_W:(r + 1) * GRID_W, :]
        keep = first if a == 0 else jnp.logical_not(first)
        qa = jnp.where(keep, q, 0.0).astype(BF16)
        k = kn_ref[pl.ds(kstart[r], kwin), :]
        s.append(lax.dot_general(qa, k, (((1,), (1,)), ((), ())), preferred_element_type=F32) + b_ref[a, var[r]])
    p, l = [], []
    for n in range(len(combos)):
        e = jnp.exp(s[n] - jnp.max(s[n], axis=-1, keepdims=True))
        l.append(jnp.sum(e, axis=-1, keepdims=True))
        p.append(e.astype(BF16))
    o = [jnp.dot(p[n], v_ref[0, pl.ds(kstart[r], kwin), :], preferred_element_type=F32) / l[n]
         for n, (r, a) in enumerate(combos)]
    for r in range(rb):
        o_ref[0, r * GRID_W:(r + 1) * GRID_W, :] = jnp.where(first, o[2 * r], o[2 * r + 1]).astype(o_ref.dtype)


def _natten_bias(rpb):
    kh, kw, w = NA_WIN_ROWS, NA_WIN_COLS, GRID_W
    var = np.arange(kh)
    jr = np.arange(kh)
    dr = jr[None, :] + (kh - 1) - var[:, None]
    sel_r = (dr[:, :, None] == np.arange(2 * kh - 1)[None, None, :]).astype(np.float32)
    col = np.arange(w)
    cstart = np.clip(col - kw // 2, 0, w - kw)
    dc = col[None, :] - col[:, None] + (kw - 1)
    valid = (col[None, :] >= cstart[:, None]) & (col[None, :] < cstart[:, None] + kw)
    sel_c = ((dc[:, :, None] == np.arange(2 * kw - 1)[None, None, :]) & valid[:, :, None]).astype(np.float32)
    b = jnp.einsum('hrd,vjr,cgd->hvcjg', rpb, jnp.asarray(sel_r), jnp.asarray(sel_c), precision=HIGHEST)
    mask = np.where(valid, 0.0, NEG_BIG).astype(np.float32)[None, None, :, None, :]
    b = b + jnp.asarray(mask)
    return b.reshape(rpb.shape[0], kh, w, kh * w)


def neighbourhood_attention(z, q_g, k_g, rpb, rb):
    bsz, seq, d3 = z.shape
    d = d3 // 3
    rows = seq // GRID_W
    bias = _natten_bias(rpb)
    hg = d // LANES
    kwin = NA_WIN_ROWS * GRID_W
    qg = jnp.tile(q_g.reshape(1, NA_HEAD_DIM), (1, 2)) * (NA_HEAD_DIM ** -0.5)
    kg = jnp.tile(k_g.reshape(1, NA_HEAD_DIM), (1, 2))
    vmem = (2 * (2 * seq * LANES * 2 + 2 * rb * GRID_W * LANES * 2 + 2 * NA_WIN_ROWS * GRID_W * kwin * 4)
            + seq * LANES * 2 + 6 * rb * GRID_W * kwin * 4 + (4 << 20))
    return pl.pallas_call(
        functools.partial(_natten_body, rows=rows, rb=rb),
        grid=(bsz, hg, rows // rb),
        in_specs=[pl.BlockSpec((1, rb * GRID_W, LANES), lambda b, h, i: (b, i, h)),
                  pl.BlockSpec((1, seq, LANES), lambda b, h, i: (b, 0, hg + h)),
                  pl.BlockSpec((1, seq, LANES), lambda b, h, i: (b, 0, 2 * hg + h)),
                  pl.BlockSpec((1, LANES), lambda b, h, i: (0, 0)),
                  pl.BlockSpec((1, LANES), lambda b, h, i: (0, 0)),
                  pl.BlockSpec((2, NA_WIN_ROWS, GRID_W, kwin), lambda b, h, i: (h, 0, 0, 0))],
        out_specs=pl.BlockSpec((1, rb * GRID_W, LANES), lambda b, h, i: (b, i, h)),
        out_shape=jax.ShapeDtypeStruct((bsz, seq, d), BF16),
        scratch_shapes=[pltpu.VMEM((seq, LANES), BF16)],
        compiler_params=_params(("parallel", "parallel", "arbitrary"), vmem),
        name="natten",
    )(z, z, z, qg, kg, bias)


def _hy_filter_body(f_ref, w1_ref, b1_ref, w2_ref, b2_ref, w3_ref, b3_ref, w4_ref, fr_ref, dl_ref,
                    h_ref, ss_ref, *, c):
    d = functools.partial(jnp.dot, preferred_element_type=F32, precision=HIGHEST)
    f = f_ref[...]
    fr = fr_ref[...]
    x = jnp.sin(fr * (d(f, w1_ref[...]) + b1_ref[...]))
    x = jnp.sin(fr * (d(x, w2_ref[...]) + b2_ref[...]))
    x = jnp.sin(fr * (d(x, w3_ref[...]) + b3_ref[...]))
    h = d(x, w4_ref[...])
    hf = h[:, 0:c] * jnp.exp(-f[:, 0:1] * dl_ref[...])
    hb = h[:, c:2 * c] * jnp.exp(-f[:, FILTER_EMB_PAD:FILTER_EMB_PAD + 1] * dl_ref[...])
    h_ref[0] = hf
    h_ref[1] = hb * f[:, FILTER_EMB_PAD + FILTER_EMB:FILTER_EMB_PAD + FILTER_EMB + 1]

    @pl.when(pl.program_id(0) == 0)
    def _():
        ss_ref[...] = jnp.zeros_like(ss_ref)

    ss_ref[...] += jnp.sum(hf * hf, axis=0, keepdims=True) + jnp.sum(hb * hb, axis=0, keepdims=True)


def hyena_filter(seq, w1, b1, w2, b2, w3, b3, w4, freq, tl):
    c = w4.shape[1] // 2
    hid = w1.shape[1]
    t = np.linspace(0.0, 1.0, seq, dtype=np.float32)[:, None]
    bands = (FILTER_EMB - 1) // 2
    ang = (np.float32(2.0 * math.pi / seq) * np.arange(seq, dtype=np.float32)[:, None]
           * np.linspace(1e-4, bands - 1, bands, dtype=np.float32)[None])
    feats = np.concatenate([t, np.cos(ang), -np.sin(ang), np.ones((seq, 1), np.float32),
                            np.zeros((seq, FILTER_EMB_PAD - FILTER_EMB - 1), np.float32)], axis=-1)
    feats_b = np.concatenate([feats[0:1], feats[:0:-1]], axis=0)
    feats_b[0, FILTER_EMB] = 0.0
    feats2 = np.concatenate([feats, feats_b], axis=1)
    pair = lambda a: jnp.concatenate([a, a], axis=-1)
    bdiag = lambda a, b: jnp.concatenate(
        [jnp.concatenate([a, jnp.zeros((a.shape[0], b.shape[1]), F32)], axis=1),
         jnp.concatenate([jnp.zeros((b.shape[0], a.shape[1]), F32), b], axis=1)], axis=0)
    w1p = jnp.concatenate([w1, jnp.zeros((FILTER_EMB_PAD - FILTER_EMB, hid), F32)], axis=0)
    deltas = np.abs(np.linspace(DECAY_MIN, DECAY_MAX, c, dtype=np.float32))[None]
    full = lambda a: pl.BlockSpec(a.shape, lambda i: (0,) * a.ndim)
    args = [jnp.asarray(feats2), bdiag(w1p, w1p), pair(b1.reshape(1, hid)), bdiag(w2, w2), pair(b2.reshape(1, hid)),
            bdiag(w3, w3), pair(b3.reshape(1, hid)), bdiag(w4[:, :c], w4[:, c:]), pair(freq.reshape(1, hid)),
            jnp.asarray(deltas)]
    in_specs = [pl.BlockSpec((tl, 2 * FILTER_EMB_PAD), lambda i: (i, 0))] + [full(a) for a in args[1:]]
    kern, ss = pl.pallas_call(
        functools.partial(_hy_filter_body, c=c),
        grid=(seq // tl,),
        in_specs=in_specs,
        out_specs=[pl.BlockSpec((2, tl, c), lambda i: (0, i, 0)), pl.BlockSpec((1, c), lambda i: (0, 0))],
        out_shape=[jax.ShapeDtypeStruct((2, seq, c), F32), jax.ShapeDtypeStruct((1, c), F32)],
        compiler_params=_params(("arbitrary",), 40 << 20),
        name="hyena_filter",
    )(*args)
    return kern.reshape(2 * seq, c), ss


def _hy_gate_body(z0_ref, z1_ref, zv_ref, w0_ref, w1_ref, wv_ref, b0_ref, b1_ref, bv_ref, s_ref, x0_ref,
                  *, seq):
    row = lax.broadcasted_iota(jnp.int32, (seq, LANES), 0)
    top = row == 0
    bot = row == seq - 1

    def conv(z_ref, w_ref, b_ref):
        z = z_ref[0]
        w = w_ref[...]
        zp = jnp.where(top, 0.0, pltpu.roll(z, 1, 0))
        zn = jnp.where(bot, 0.0, pltpu.roll(z, seq - 1, 0))
        return zp * w[0:1, :] + z * w[1:2, :] + zn * w[2:3, :] + b_ref[...]

    x0_ref[0] = conv(z0_ref, w0_ref, b0_ref).astype(x0_ref.dtype)
    s_ref[0] = (conv(zv_ref, wv_ref, bv_ref) * conv(z1_ref, w1_ref, b1_ref)).astype(s_ref.dtype)


def hyena_gate(z, short_w, short_b, col0):
    bsz, seq, _ = z.shape
    c = HYENA_WIDTH
    nb = c // LANES
    o0 = col0 // LANES
    sb = short_b.reshape(1, 3 * c)
    zspec = lambda off: pl.BlockSpec((1, seq, LANES), lambda b, j: (b, 0, off + j))
    wspec = lambda off: pl.BlockSpec((3, LANES), lambda b, j: (0, off + j))
    bspec = lambda off: pl.BlockSpec((1, LANES), lambda b, j: (0, off + j))
    out = jax.ShapeDtypeStruct((bsz, seq, c), BF16)
    return pl.pallas_call(
        functools.partial(_hy_gate_body, seq=seq),
        grid=(bsz, nb),
        in_specs=[zspec(o0), zspec(o0 + nb), zspec(o0 + 2 * nb),
                  wspec(0), wspec(nb), wspec(2 * nb), bspec(0), bspec(nb), bspec(2 * nb)],
        out_specs=[pl.BlockSpec((1, seq, LANES), lambda b, j: (b, 0, j))] * 2,
        out_shape=[out, out],
        compiler_params=_params(("parallel", "parallel"), 2 * 5 * seq * LANES * 4 + 8 * seq * LANES * 4 + (4 << 20)),
        name="hyena_gate",
    )(z, z, z, short_w, short_w, short_w, sb, sb, sb)


def _dft_consts(n, n_in):
    k = np.arange(n)[:, None].astype(np.float64)
    t = np.arange(n_in)[None, :].astype(np.float64)
    th = 2.0 * np.pi * k * t / n
    w_first = np.concatenate([np.cos(th), -np.sin(th)], axis=0)
    w_last = np.concatenate([np.cos(th).T, -np.sin(th).T], axis=1)
    tt = np.arange(n)[None, :].astype(np.float64)
    th2 = 2.0 * np.pi * k * tt / n
    cm, sm = np.cos(th2), np.sin(th2)
    m_mid = np.block([[cm, sm], [-sm, cm]])
    ph = 2.0 * np.pi * k * tt / (n * n)
    tw = np.stack([np.cos(ph), np.sin(ph)], axis=1)[..., None]
    return w_first, w_last, m_mid, tw


def _hy_stage1_body(w_ref, x_ref, o_ref, *, tb):
    w = w_ref[...]
    for j in range(tb):
        o_ref[0, j] = jnp.dot(w, x_ref[0, j], preferred_element_type=F32).astype(o_ref.dtype)


def hyena_stage(w, x, tb, out_dtype):
    bsz, n2, kk, c = x.shape
    m = w.shape[0]
    osz = jnp.dtype(out_dtype).itemsize
    vmem = 2 * (tb * kk * c * 2 + tb * m * c * osz + m * kk * 2) + 2 * m * c * 4 + (4 << 20)
    return pl.pallas_call(
        functools.partial(_hy_stage1_body, tb=tb),
        grid=(bsz, n2 // tb),
        in_specs=[pl.BlockSpec((m, kk), lambda b, j: (0, 0)),
                  pl.BlockSpec((1, tb, kk, c), lambda b, j: (b, j, 0, 0))],
        out_specs=pl.BlockSpec((1, tb, m, c), lambda b, j: (b, j, 0, 0)),
        out_shape=jax.ShapeDtypeStruct((bsz, n2, m, c), out_dtype),
        compiler_params=_params(("parallel", "parallel"), vmem),
        name="hyena_dft_outer",
    )(w, x)


def _twiddle(xr, xi, tc, ts, sign):
    return xr * tc + sign * (xi * ts), xi * tc - sign * (xr * ts)


HYENA_K1_PER_STEP = 4


def _hy_twiddled_input(a_ref, tw_ref, j, n):
    a = a_ref[0, j].astype(F32)
    xr, xi = _twiddle(a[0:n], a[n:2 * n], tw_ref[j, 0], tw_ref[j, 1], 1.0)
    return jnp.concatenate([xr, xi], axis=0).astype(BF16)


def _hy_spec_body(m_ref, tw_ref, a_ref, o_ref, *, n, kb):
    x = [_hy_twiddled_input(a_ref, tw_ref, j, n) for j in range(kb)]
    for j in range(kb):
        o_ref[0, j] = jnp.dot(m_ref[...], x[j], preferred_element_type=F32)


def _hy_mid_body(m_ref, mt_ref, tw_ref, hs_ref, a_ref, o_ref, *, n, kb):
    x = [_hy_twiddled_input(a_ref, tw_ref, j, n) for j in range(kb)]
    s = [jnp.dot(m_ref[...], x[j], preferred_element_type=F32) for j in range(kb)]
    p = []
    for j in range(kb):
        hs = hs_ref[0, j]
        sr, si = s[j][0:n], s[j][n:2 * n]
        hr, hi = hs[0:n], hs[n:2 * n]
        p.append(jnp.concatenate([sr * hr - si * hi, sr * hi + si * hr], axis=0).astype(BF16))
    y = [jnp.dot(mt_ref[...], p[j], preferred_element_type=F32) for j in range(kb)]
    for j in range(kb):
        yr, yi = _twiddle(y[j][0:n], y[j][n:2 * n], tw_ref[j, 0], tw_ref[j, 1], -1.0)
        o_ref[0, j] = jnp.concatenate([yr, yi], axis=0).astype(o_ref.dtype)


def hyena_spectrum(a, m_mid, tw):
    _, n, n2x, c = a.shape
    kb = min(HYENA_K1_PER_STEP, n)
    return pl.pallas_call(
        functools.partial(_hy_spec_body, n=n, kb=kb),
        grid=(n // kb, 1),
        in_specs=[pl.BlockSpec((n2x, n2x), lambda k, b: (0, 0)),
                  pl.BlockSpec((kb, 2, n, 1), lambda k, b: (k, 0, 0, 0)),
                  pl.BlockSpec((1, kb, n2x, c), lambda k, b: (b, k, 0, 0))],
        out_specs=pl.BlockSpec((1, kb, n2x, c), lambda k, b: (b, k, 0, 0)),
        out_shape=jax.ShapeDtypeStruct((1, n, n2x, c), F32),
        compiler_params=_params(("parallel", "parallel"), 40 << 20),
        name="hyena_filter_spectrum",
    )(m_mid, tw, a)


def hyena_mid(a, hspec, m_mid, m_mid_t, tw):
    bsz, n, n2x, c = a.shape
    kb = min(HYENA_K1_PER_STEP, n)
    return pl.pallas_call(
        functools.partial(_hy_mid_body, n=n, kb=kb),
        grid=(n // kb, bsz),
        in_specs=[pl.BlockSpec((n2x, n2x), lambda k, b: (0, 0)),
                  pl.BlockSpec((n2x, n2x), lambda k, b: (0, 0)),
                  pl.BlockSpec((kb, 2, n, 1), lambda k, b: (k, 0, 0, 0)),
                  pl.BlockSpec((1, kb, n2x, c), lambda k, b: (0, k, 0, 0)),
                  pl.BlockSpec((1, kb, n2x, c), lambda k, b: (b, k, 0, 0))],
        out_specs=pl.BlockSpec((1, kb, n2x, c), lambda k, b: (b, k, 0, 0)),
        out_shape=jax.ShapeDtypeStruct((bsz, n, n2x, c), BF16),
        compiler_params=_params(("parallel", "parallel"), 40 << 20),
        name="hyena_dft_mid",
    )(m_mid, m_mid_t, tw, hspec, a)


def _hy_combine_body(y_ref, s_ref, x0_ref, sc_ref, bi_ref, o_ref):
    o_ref[...] = ((y_ref[...] * sc_ref[...] + s_ref[...] * bi_ref[...]) * x0_ref[...]).astype(o_ref.dtype)


def hyena_combine(y, s, x0, scale, bias, tm):
    t, c = y.shape
    row = lambda: pl.BlockSpec((tm, c), lambda i: (i, 0))
    vec = lambda: pl.BlockSpec((1, c), lambda i: (0, 0))
    return pl.pallas_call(
        _hy_combine_body,
        grid=(t // tm,),
        in_specs=[row(), row(), row(), vec(), vec()],
        out_specs=row(),
        out_shape=jax.ShapeDtypeStruct((t, c), BF16),
        compiler_params=_params(("parallel",), 2 * 4 * tm * c * 4 + (4 << 20)),
        name="hyena_combine",
    )(y, s, x0, scale.reshape(1, c), bias.reshape(1, c))


def hyena_bidir(z, col0, short_w, short_b, f_w1, f_b1, f_w2, f_b2, f_w3, f_b3, f_w4, f_freq, bias):
    bsz, seq, _ = z.shape
    c = HYENA_WIDTH
    n = int(round(math.sqrt(2 * seq)))
    assert n * n == 2 * seq and n % 16 == 0
    nh = n // 2
    tb = 8

    w_first, w_last, m_mid, tw = _dft_consts(n, n)
    w_first_x = jnp.asarray(w_first[:, :nh], BF16)
    w_first_h = jnp.asarray(w_first, BF16)
    w_last_y = jnp.asarray(w_last[:nh], BF16)
    m_mid_j = jnp.asarray(m_mid, BF16)
    m_mid_t = jnp.asarray(m_mid.T, BF16)
    tw_j = jnp.asarray(tw, F32)

    kern, ss = hyena_filter(seq, f_w1, f_b1, f_w2, f_b2, f_w3, f_b3, f_w4, f_freq, tl=min(seq, 1024))
    scale = lax.rsqrt(ss[0] + EPS) * (1.0 / (2 * seq))
    kern_t = jnp.transpose(kern.reshape(1, n, n, c), (0, 2, 1, 3)).astype(BF16)
    ah = hyena_stage(w_first_h, kern_t, tb, BF16)
    ah = jnp.transpose(ah.reshape(1, n, 2, n, c), (0, 3, 2, 1, 4)).reshape(1, n, 2 * n, c)
    hspec = hyena_spectrum(ah, m_mid_j, tw_j)

    s, x0 = hyena_gate(z, short_w, short_b, col0)
    s_t = jnp.transpose(s.reshape(bsz, nh, n, c), (0, 2, 1, 3)).astype(BF16)
    a = hyena_stage(w_first_x, s_t, tb, BF16)
    a = jnp.transpose(a.reshape(bsz, n, 2, n, c), (0, 3, 2, 1, 4)).reshape(bsz, n, 2 * n, c)
    bm = hyena_mid(a, hspec, m_mid_j, m_mid_t, tw_j)
    bm = jnp.transpose(bm.reshape(bsz, n, 2, n, c), (0, 3, 2, 1, 4)).reshape(bsz, n, 2 * n, c)
    y = hyena_stage(w_last_y, bm, tb, BF16)
    y = jnp.transpose(y, (0, 2, 1, 3)).reshape(bsz * seq, c)
    return hyena_combine(y, s.reshape(bsz * seq, c), x0.reshape(bsz * seq, c), scale, bias,
                         tm=min(bsz * seq, 2048))


RWKV_PAIRS = RWKV_WIDTH // LANES
RWKV_PREP_CHUNKS = 4


def _pair_rows(x, first):
    return jnp.concatenate([jnp.where(first, x, 0.0), jnp.where(first, 0.0, x)], axis=0)


def _fold_rows(x2, ch):
    return x2[0:ch] + x2[ch:2 * ch]


def _rwkv_prep_body(z_ref, halo_ref, mu_ref, w0_ref, wup_ref, a0_ref, aup_ref, kk_ref, ka_ref, rk_ref, sel_ref,
                    wt_ref, rt_ref, kb_ref, ab_ref, v_ref, ut_ref, yk_ref, ara_ref, ge_ref, bonus_ref,
                    *, reverse, nsteps, cps):
    ch = RWKV_CHUNK
    cw = RWKV_WIDTH
    rows = cps * ch
    c_idx = pl.program_id(1)

    zs = z_ref[0]
    row = lax.broadcasted_iota(jnp.int32, (rows, 1), 0)
    if reverse:
        nbr = jnp.where(c_idx == nsteps - 1, 0.0, halo_ref[0, 0:1, :])
        shifted = jnp.where(row == rows - 1, jnp.broadcast_to(nbr, zs.shape), pltpu.roll(zs, rows - 1, 0))
    else:
        nbr = jnp.where(c_idx == 0, 0.0, halo_ref[0, SUBLANES - 1:SUBLANES, :])
        shifted = jnp.where(row == 0, jnp.broadcast_to(nbr, zs.shape), pltpu.roll(zs, 1, 0))
    zd = zs + (shifted - zs) * mu_ref[...]

    r = zd[:, 0:cw]
    k = zd[:, cw:2 * cw]
    v = zd[:, 2 * cw:3 * cw]
    lora = zd[:, 3 * cw:3 * cw + DECAY_RANK + ICLR_RANK]
    x = w0_ref[...] + _bdot(jnp.tanh(lora), wup_ref[...])
    log_w = jnp.minimum(x, 0.0) - jnp.log(1.0 + jnp.exp(-jnp.abs(x))) - 0.5
    lnw = -jnp.exp(log_w)
    a = jax.nn.sigmoid(a0_ref[...] + _bdot(lora, aup_ref[...]))
    sel = sel_ref[...]
    kkr = k * kk_ref[...]
    kkn = jnp.sqrt(_bdot(kkr * kkr, sel))
    kk = kkr / jnp.maximum(kkn, 1e-12)
    k2 = k * (1.0 + (a - 1.0) * ka_ref[...])
    ah = kk * a
    bonus_ref[0] = (_bdot(r * k2 * rk_ref[...], sel) * v).astype(bonus_ref.dtype)

    ri = lax.broadcasted_iota(jnp.int32, (rows, rows), 0)
    ci = lax.broadcasted_iota(jnp.int32, (rows, rows), 1)
    csh = ch.bit_length() - 1
    tri = ((ci >= ri) if reverse else (ci <= ri)) & ((ri >> csh) == (ci >> csh))
    cs = _dot_exact_lhs(tri.astype(BF16), lnw)
    total = jnp.concatenate(
        [jnp.broadcast_to(jnp.sum(lnw[g * ch:(g + 1) * ch], axis=0, keepdims=True), (ch, cw)) for g in range(cps)],
        axis=0)
    e_inv = jnp.exp(-cs)
    e_end = jnp.exp(total - cs)
    rt = r * jnp.exp(cs)
    kkt = kk * jnp.exp(cs - lnw)
    kh = k2 * e_inv
    ahh = ah * e_inv
    rt_ref[0] = rt.astype(rt_ref.dtype)
    kb_ref[0] = (k2 * e_end).astype(kb_ref.dtype)
    ab_ref[0] = (ah * e_end).astype(ab_ref.dtype)
    v_ref[0] = v.astype(v_ref.dtype)
    for g in range(cps):
        ge_ref[0, g] = jnp.exp(total[g * ch:g * ch + 1])

    r2 = lax.broadcasted_iota(jnp.int32, (2 * ch, 2 * ch), 0)
    c2 = lax.broadcasted_iota(jnp.int32, (2 * ch, 2 * ch), 1)
    same_head = (r2 < ch) == (c2 < ch)
    si = r2 & (ch - 1)
    sj = c2 & (ch - 1)
    before = (sj > si) if reverse else (sj < si)
    strict = same_head & before
    incl = same_head & jnp.logical_or(before, si == sj)
    eye = (r2 == c2).astype(F32)
    first = lax.broadcasted_iota(jnp.int32, (1, LANES), 1) < RWKV_HEAD_DIM

    units = [(g, p) for g in range(cps) for p in range(RWKV_PAIRS)]
    un = range(len(units))
    tile = lambda arr, g, p: arr[g * ch:(g + 1) * ch, p * LANES:(p + 1) * LANES]
    l2 = [_pair_rows(tile(kkt, g, p), first) for g, p in units]
    l4 = [jnp.concatenate([l2[n], _pair_rows(tile(rt, g, p), first)], axis=0) for n, (g, p) in enumerate(units)]
    v2 = [_pair_rows(tile(v, g, p), first) for g, p in units]
    sk = [_bdot_nt(l4[n], jnp.concatenate([tile(kh, g, p)] * 2, axis=0)) for n, (g, p) in enumerate(units)]
    sa = [_bdot_nt(l4[n], jnp.concatenate([tile(ahh, g, p)] * 2, axis=0)) for n, (g, p) in enumerate(units)]
    nm = [jnp.where(strict, sa[n][0:2 * ch], 0.0) for n in un]
    for n, (g, p) in enumerate(units):
        ara_ref[0, g, p] = jnp.where(incl, sa[n][2 * ch:4 * ch], 0.0).astype(ara_ref.dtype)
    akr = [jnp.concatenate([jnp.where(strict, sk[n][0:2 * ch], 0.0),
                            jnp.where(incl, sk[n][2 * ch:4 * ch], 0.0)], axis=0) for n in un]
    av = [_bdot(akr[n], v2[n]) for n in un]

    lvl = (si >> 1) == (sj >> 1)
    t = [eye - jnp.where(lvl, nm[n], 0.0) for n in un]
    s = 2
    while s < ch:
        sh = s.bit_length() - 1
        lvl = ((si >> (sh + 1)) == (sj >> (sh + 1))) & ((si >> sh) != (sj >> sh))
        x = [_bdot(jnp.where(lvl, nm[n], 0.0), t[n]) for n in un]
        t = [t[n] - _bdot(t[n], x[n]) for n in un]
        s *= 2

    wu = [_bdot(t[n], jnp.concatenate([l2[n], av[n][0:2 * ch]], axis=1)) for n in un]
    gather = lambda f: jnp.concatenate(
        [jnp.concatenate([f(g * RWKV_PAIRS + p) for p in range(RWKV_PAIRS)], axis=1) for g in range(cps)], axis=0)
    wt_ref[0] = gather(lambda n: _fold_rows(wu[n][:, 0:LANES], ch)).astype(wt_ref.dtype)
    ut_ref[0] = gather(lambda n: _fold_rows(wu[n][:, LANES:2 * LANES], ch)).astype(ut_ref.dtype)
    yk_ref[0] = gather(lambda n: _fold_rows(av[n][2 * ch:4 * ch], ch)).astype(yk_ref.dtype)


def rwkv7_prep(z, mu, w0, w_up, a0, a_up, k_k, k_a, r_k, reverse):
    bsz, seq, _ = z.shape
    ch = RWKV_CHUNK
    cw = RWKV_WIDTH
    nc = seq // ch
    cps = min(RWKV_PREP_CHUNKS, nc)
    nsteps = nc // cps
    cpb = cps * ch // SUBLANES
    nhb = seq // SUBLANES
    sel_np = (np.arange(cw)[:, None] // RWKV_HEAD_DIM == np.arange(cw)[None, :] // RWKV_HEAD_DIM)
    sel = jnp.asarray(sel_np, BF16)
    zero_pad = jnp.zeros((DECAY_RANK, cw), F32)
    wup = jnp.concatenate([w_up, zero_pad], axis=0).astype(BF16)
    aup = jnp.concatenate([zero_pad, a_up], axis=0).astype(BF16)
    vec = lambda a: a.reshape(1, -1)
    params = [vec(mu), vec(w0), wup, vec(a0), aup, vec(k_k), vec(k_a), vec(r_k), sel]
    full = lambda a: pl.BlockSpec(a.shape, lambda b, c: (0,) * a.ndim)
    if reverse:
        halo = pl.BlockSpec((1, SUBLANES, RWKV_SHIFT_COLS), lambda b, c: (b, jnp.minimum((c + 1) * cpb, nhb - 1), 0))
    else:
        halo = pl.BlockSpec((1, SUBLANES, RWKV_SHIFT_COLS), lambda b, c: (b, jnp.maximum(c * cpb - 1, 0), 0))
    tok = pl.BlockSpec((1, cps * ch, cw), lambda b, c: (b, c, 0))
    tok_bf = jax.ShapeDtypeStruct((bsz, seq, cw), BF16)
    return pl.pallas_call(
        functools.partial(_rwkv_prep_body, reverse=reverse, nsteps=nsteps, cps=cps),
        grid=(bsz, nsteps),
        in_specs=[pl.BlockSpec((1, cps * ch, RWKV_SHIFT_COLS), lambda b, c: (b, c, 0)), halo]
                 + [full(a) for a in params],
        out_specs=[tok] * 7
                  + [pl.BlockSpec((1, cps, RWKV_PAIRS, 2 * ch, 2 * ch), lambda b, c: (b, c, 0, 0, 0)),
                     pl.BlockSpec((1, cps, 1, cw), lambda b, c: (b, c, 0, 0)),
                     tok],
        out_shape=[tok_bf, tok_bf, tok_bf, tok_bf, tok_bf, tok_bf, tok_bf,
                   jax.ShapeDtypeStruct((bsz, nc, RWKV_PAIRS, 2 * ch, 2 * ch), BF16),
                   jax.ShapeDtypeStruct((bsz, nc, 1, cw), F32),
                   tok_bf],
        compiler_params=_params(("parallel", "parallel"), 40 << 20),
        name="rwkv7_prep_bwd" if reverse else "rwkv7_prep_fwd",
    )(z, z, *params)


def _rwkv_scan_body(*refs, bsz, final):
    if final:
        (wt_ref, rt_ref, kb_ref, ab_ref, v_ref, ut_ref, yk_ref, ara_ref, ge_ref,
         yin_ref, b1_ref, b2_ref, gd_ref, gup_ref, lnw_ref, lnb_ref, sel_ref, o_ref, st_ref) = refs
    else:
        (wt_ref, rt_ref, kb_ref, ab_ref, v_ref, ut_ref, yk_ref, ara_ref, ge_ref, o_ref, st_ref) = refs
    ch = RWKV_CHUNK

    @pl.when(pl.program_id(0) == 0)
    def _():
        st_ref[...] = jnp.zeros_like(st_ref)

    first = lax.broadcasted_iota(jnp.int32, (1, LANES), 1) < RWKV_HEAD_DIM
    blk = ((lax.broadcasted_iota(jnp.int32, (LANES, LANES), 0) < RWKV_HEAD_DIM)
           == (lax.broadcasted_iota(jnp.int32, (LANES, LANES), 1) < RWKV_HEAD_DIM))
    idx = [(b, p) for b in range(bsz) for p in range(RWKV_PAIRS)]
    sl = [slice(p * LANES, (p + 1) * LANES) for p in range(RWKV_PAIRS)]
    st = [st_ref[b * RWKV_PAIRS + p] for b, p in idx]
    lm0 = [_bdot_nt(jnp.concatenate([wt_ref[b, :, sl[p]], rt_ref[b, :, sl[p]]], axis=0), st[n])
           for n, (b, p) in enumerate(idx)]
    u = [lm0[n][0:ch] + ut_ref[b, :, sl[p]] for n, (b, p) in enumerate(idx)]
    yr = [_bdot(ara_ref[b, 0, p], _pair_rows(u[n], first)) for n, (b, p) in enumerate(idx)]
    upd = [_bdot_tn(jnp.concatenate([v_ref[b, :, sl[p]].astype(F32), -u[n]], axis=0),
                    jnp.concatenate([kb_ref[b, :, sl[p]], ab_ref[b, :, sl[p]]], axis=0))
           for n, (b, p) in enumerate(idx)]
    for n, (b, p) in enumerate(idx):
        st_ref[b * RWKV_PAIRS + p] = st[n] * ge_ref[b, 0, :, sl[p]] + jnp.where(blk, upd[n], 0.0)
    for b in range(bsz):
        y = jnp.concatenate(
            [lm0[b * RWKV_PAIRS + p][ch:2 * ch] + yk_ref[b, :, sl[p]] - _fold_rows(yr[b * RWKV_PAIRS + p], ch)
             for p in range(RWKV_PAIRS)], axis=1)
        if not final:
            o_ref[b] = y.astype(o_ref.dtype)
        else:
            sel = sel_ref[...]
            yt = y + yin_ref[b]
            mean = _bdot(yt, sel) * (1.0 / RWKV_HEAD_DIM)
            dv = yt - mean
            var = _bdot(dv * dv, sel) * (1.0 / RWKV_HEAD_DIM)
            yn = dv * lax.rsqrt(var + RWKV_GN_EPS) * lnw_ref[...] + lnb_ref[...]
            g = _bdot(jax.nn.sigmoid(gd_ref[b]), gup_ref[...])
            o_ref[b] = ((yn + b1_ref[b] + b2_ref[b]) * g).astype(o_ref.dtype)


def rwkv7_scan(prep, reverse, final_args=None):
    wt, rt, kb, ab, v, ut, yk, ara, ge, _ = prep
    bsz, seq, cw = wt.shape
    ch = RWKV_CHUNK
    nc = seq // ch
    cidx = (lambda c: nc - 1 - c) if reverse else (lambda c: c)
    tok = pl.BlockSpec((bsz, ch, cw), lambda c: (0, cidx(c), 0))
    in_specs = [tok] * 7 + [pl.BlockSpec((bsz, 1, RWKV_PAIRS, 2 * ch, 2 * ch), lambda c: (0, cidx(c), 0, 0, 0)),
                            pl.BlockSpec((bsz, 1, 1, cw), lambda c: (0, cidx(c), 0, 0))]
    args = [wt, rt, kb, ab, v, ut, yk, ara, ge]
    final = final_args is not None
    if final:
        z, y_in, bonus_a, bonus_b, g_up, ln_w, ln_b = final_args
        sel_np = (np.arange(cw)[:, None] // RWKV_HEAD_DIM == np.arange(cw)[None, :] // RWKV_HEAD_DIM)
        full = lambda a: pl.BlockSpec(a.shape, lambda c: (0,) * a.ndim)
        extra = [g_up.astype(BF16), ln_w.reshape(1, cw), ln_b.reshape(1, cw), jnp.asarray(sel_np, BF16)]
        in_specs += [tok, tok, tok,
                     pl.BlockSpec((bsz, ch, GATE_RANK), lambda c: (0, cidx(c), RWKV_SHIFT_COLS // GATE_RANK))]
        in_specs += [full(a) for a in extra]
        args += [y_in, bonus_a, bonus_b, z] + extra
    out_dtype = BF16
    return pl.pallas_call(
        functools.partial(_rwkv_scan_body, bsz=bsz, final=final),
        grid=(nc,),
        in_specs=in_specs,
        out_specs=tok,
        out_shape=jax.ShapeDtypeStruct((bsz, seq, cw), out_dtype),
        scratch_shapes=[pltpu.VMEM((bsz * RWKV_PAIRS, LANES, LANES), F32)],
        compiler_params=_params(("arbitrary",), 40 << 20),
        name="rwkv7_scan_bwd" if reverse else "rwkv7_scan_fwd",
    )(*args)


def rwkv7_bidir(z, mu, w0, w_up, a0, a_up, g_up, k_k, k_a, r_k, ln_w, ln_b):
    rk = r_k.reshape(-1)
    prep_f = rwkv7_prep(z, mu[0], w0[0], w_up[0], a0[0], a_up[0], k_k, k_a, rk, reverse=False)
    prep_b = rwkv7_prep(z, mu[1], w0[1], w_up[1], a0[1], a_up[1], k_k, k_a, rk, reverse=True)
    y_f = rwkv7_scan(prep_f, reverse=False)
    return rwkv7_scan(prep_b, reverse=True, final_args=(z, y_f, prep_f[9], prep_b[9], g_up, ln_w, ln_b))


def kernel(x, p, mix_norm, mix_w_in, rwkv_mu, rwkv_w0, rwkv_w_up, rwkv_a0, rwkv_a_up, rwkv_g_up, rwkv_k_k, rwkv_k_a, rwkv_r_k, rwkv_ln_w, rwkv_ln_b, hy_short_w, hy_short_b, hy_f_w1, hy_f_b1, hy_f_w2, hy_f_b2, hy_f_w3, hy_f_b3, hy_f_w4, hy_f_freq, hy_bias, mix_w_out, na_norm, na_w_qkv, na_q_g, na_k_g, na_rpb, na_w_out, ffn_norm, ffn_w_up, ffn_conv_w, ffn_conv_b, ffn_w_down, ple_norm, ple_w_gate, ple_w_proj):
    bsz, seq, d = x.shape
    depth = p.shape[0]
    t = bsz * seq
    tm = min(1024, seq)
    h = x.reshape(t, d)
    for i in range(depth):
        j = i // 2
        if i % 2 == 0:
            w_in = mix_w_in[j].astype(BF16)
            z = norm_matmul(h, mix_norm[j], w_in, tm=tm, tn=w_in.shape[1] // 2).reshape(bsz, seq, -1)
            y_a = rwkv7_bidir(z, rwkv_mu[j], rwkv_w0[j], rwkv_w_up[j], rwkv_a0[j], rwkv_a_up[j], rwkv_g_up[j],
                              rwkv_k_k[j], rwkv_k_a[j], rwkv_r_k[j], rwkv_ln_w[j], rwkv_ln_b[j])
            y_b = hyena_bidir(z, RWKV_COLS, hy_short_w[j], hy_short_b[j], hy_f_w1[j], hy_f_b1[j], hy_f_w2[j],
                              hy_f_b2[j], hy_f_w3[j], hy_f_b3[j], hy_f_w4[j], hy_f_freq[j], hy_bias[j])
            w_out = mix_w_out[j].astype(BF16)
            ys, wos = [y_a.reshape(t, -1), y_b], [w_out[:RWKV_WIDTH], w_out[RWKV_WIDTH:]]
        else:
            z = norm_matmul(h, na_norm[j], na_w_qkv[j].astype(BF16), tm=tm, tn=1024, out_dtype=BF16)
            o = neighbourhood_attention(z.reshape(bsz, seq, 3 * d), na_q_g[j], na_k_g[j], na_rpb[j], rb=32)
            ys, wos = [o.reshape(t, d)], [na_w_out[j].astype(BF16)]
        h = conv_ffn_ple(h, ys, wos, ffn_norm[i], ffn_w_up[i].astype(BF16), ffn_conv_w[i], ffn_conv_b[i],
                         ffn_w_down[i].astype(BF16), p[i].reshape(t, -1), ple_norm[i],
                         ple_w_gate[i].astype(BF16), ple_w_proj[i].astype(BF16), seq_len=seq, tm=tm, tf=256)
    return h.reshape(bsz, seq, d)
```

```python
import functools
import math

import numpy as np
import jax
import jax.numpy as jnp
from jax import lax
from jax.experimental import pallas as pl
from jax.experimental.pallas import tpu as pltpu

F32 = jnp.float32
BF16 = jnp.bfloat16
HIGHEST = lax.Precision.HIGHEST

EPS = 1e-6
GRID_W = 64
RWKV_HEADS = 8
RWKV_HEAD_DIM = 64
RWKV_WIDTH = RWKV_HEADS * RWKV_HEAD_DIM
DECAY_RANK = 64
ICLR_RANK = 64
GATE_RANK = 128
RWKV_SHIFT_COLS = 3 * RWKV_WIDTH + DECAY_RANK + ICLR_RANK
RWKV_COLS = RWKV_SHIFT_COLS + GATE_RANK
RWKV_GN_EPS = 64e-5
RWKV_CHUNK = 64
HYENA_WIDTH = 512
FILTER_EMB = 17
FILTER_EMB_PAD = 32
DECAY_MIN = math.log(1e-2) / 1.5
DECAY_MAX = math.log(1e-2) / 0.3
NA_HEADS = 16
NA_HEAD_DIM = 64
NA_WIN_ROWS = 8
NA_WIN_COLS = 16
NEG_BIG = -1e30

LANES = 128
SUBLANES = 8
V7X_VMEM_BYTES = 64 * 1024 * 1024
VMEM_CAP = V7X_VMEM_BYTES - 8 * 1024 * 1024


def _params(semantics, vmem_bytes):
    return pltpu.CompilerParams(dimension_semantics=semantics,
                                vmem_limit_bytes=int(min(max(vmem_bytes, 16 * 1024 * 1024), VMEM_CAP)))


def _rms_rows(x, g):
    ms = jnp.mean(x * x, axis=-1, keepdims=True)
    return x * lax.rsqrt(ms + EPS) * g


def _bdot(a, b):
    return jnp.dot(a.astype(BF16), b.astype(BF16), preferred_element_type=F32)


def _bdot_nt(a, b):
    return lax.dot_general(a.astype(BF16), b.astype(BF16), (((1,), (1,)), ((), ())),
                           preferred_element_type=F32)


def _bdot_tn(a, b):
    return lax.dot_general(a.astype(BF16), b.astype(BF16), (((0,), (0,)), ((), ())),
                           preferred_element_type=F32)


def _dot_exact_lhs(a_bf16, b):
    d = functools.partial(jnp.dot, preferred_element_type=F32)
    b1 = b.astype(BF16)
    r1 = b - b1.astype(F32)
    b2 = r1.astype(BF16)
    b3 = (r1 - b2.astype(F32)).astype(BF16)
    return d(a_bf16, b1) + (d(a_bf16, b2) + d(a_bf16, b3))


def _norm_matmul_body(x_ref, g_ref, w_ref, o_ref, xn_ref):
    @pl.when(pl.program_id(1) == 0)
    def _():
        xn_ref[...] = _rms_rows(x_ref[...], g_ref[...]).astype(BF16)

    o_ref[...] = jnp.dot(xn_ref[...], w_ref[...], preferred_element_type=F32).astype(o_ref.dtype)


def norm_matmul(x, g, w, tm, tn, out_dtype=F32):
    t, d = x.shape
    n = w.shape[1]
    osz = jnp.dtype(out_dtype).itemsize
    vmem = 2 * (tm * d * 4 + d * tn * 2 + tm * tn * osz) + tm * d * 2 + (4 << 20)
    return pl.pallas_call(
        _norm_matmul_body,
        grid=(t // tm, n // tn),
        in_specs=[pl.BlockSpec((tm, d), lambda i, j: (i, 0)),
                  pl.BlockSpec((1, d), lambda i, j: (0, 0)),
                  pl.BlockSpec((d, tn), lambda i, j: (0, j))],
        out_specs=pl.BlockSpec((tm, tn), lambda i, j: (i, j)),
        out_shape=jax.ShapeDtypeStruct((t, n), out_dtype),
        scratch_shapes=[pltpu.VMEM((tm, d), BF16)],
        compiler_params=_params(("parallel", "arbitrary"), vmem),
        name="norm_matmul",
    )(x, g.reshape(1, d), w)


FFN_HALO = 16
FFN_ROW_BLOCK = 128


def _ffn_ple_body(*refs, tm, tiles_per_seq, nd, n_in):
    xm_ref, xp_ref, xx_ref, g_ref = refs[0:4]
    ym, yp, yx, wo = (refs[4 + k * n_in:4 + (k + 1) * n_in] for k in range(4))
    (wa0_ref, wb0_ref, wa1_ref, wb1_ref,
     cwa0_ref, cwb0_ref, cba0_ref, cbb0_ref, wd0_ref,
     cwa1_ref, cwb1_ref, cba1_ref, cbb1_ref, wd1_ref,
     pg_ref, wg_ref, p_ref, wp_ref,
     o_ref, xn_ref, za0_ref, zb0_ref, za1_ref, zb1_ref, act_ref) = refs[4 + 4 * n_in:]
    i = pl.program_id(0)
    m = pl.program_id(1)
    hl = FFN_HALO

    def prologue():
        def mixed(x_ref, ys):
            acc = x_ref[...]
            for y_ref, w_ref in zip(ys, wo):
                acc = acc + jnp.dot(y_ref[...], w_ref[...], preferred_element_type=F32)
            return acc

        g = g_ref[...]
        first = (i % tiles_per_seq) == 0
        last = (i % tiles_per_seq) == tiles_per_seq - 1
        xm = mixed(xm_ref, ym)
        xn_ref[hl:hl + tm, :] = _rms_rows(xm, g).astype(BF16)
        xn_ref[0:hl, :] = jnp.where(first, 0.0, _rms_rows(mixed(xp_ref, yp), g)).astype(BF16)
        xn_ref[hl + tm:2 * hl + tm, :] = jnp.where(last, 0.0, _rms_rows(mixed(xx_ref, yx), g)).astype(BF16)
        o_ref[...] = xm

    def conv(z_ref, r0, cw, cb):
        return (z_ref[r0 - 1:r0 - 1 + FFN_ROW_BLOCK, :] * cw[0:1, :] + z_ref[r0:r0 + FFN_ROW_BLOCK, :] * cw[1:2, :]
                + z_ref[r0 + 1:r0 + 1 + FFN_ROW_BLOCK, :] * cw[2:3, :] + cb)

    def contribution(za_ref, zb_ref, cwa_ref, cwb_ref, cba_ref, cbb_ref, wd_ref):
        cwa, cwb, cba, cbb = cwa_ref[...], cwb_ref[...], cba_ref[...], cbb_ref[...]
        for blk in range(tm // FFN_ROW_BLOCK):
            r0 = hl + blk * FFN_ROW_BLOCK
            act = jax.nn.gelu(conv(za_ref, r0, cwa, cba)) * conv(zb_ref, r0, cwb, cbb)
            act_ref[blk * FFN_ROW_BLOCK:(blk + 1) * FFN_ROW_BLOCK, :] = act.astype(BF16)
        return jnp.dot(act_ref[...], wd_ref[...], preferred_element_type=F32)

    @pl.when(m == 0)
    def _():
        prologue()
        za1_ref[...] = jnp.zeros_like(za1_ref)
        zb1_ref[...] = jnp.zeros_like(zb1_ref)

    xn = xn_ref[...]
    za0_ref[...] = jnp.dot(xn, wa0_ref[...], preferred_element_type=F32)
    zb0_ref[...] = jnp.dot(xn, wb0_ref[...], preferred_element_type=F32)
    ca = contribution(za1_ref, zb1_ref, cwa0_ref, cwb0_ref, cba0_ref, cbb0_ref, wd0_ref)
    o_ref[...] += jnp.where(m > 0, ca, 0.0)
    za1_ref[...] = jnp.dot(xn, wa1_ref[...], preferred_element_type=F32)
    zb1_ref[...] = jnp.dot(xn, wb1_ref[...], preferred_element_type=F32)
    o_ref[...] += contribution(za0_ref, zb0_ref, cwa1_ref, cwb1_ref, cba1_ref, cbb1_ref, wd1_ref)

    @pl.when(m == nd - 1)
    def _():
        h2 = o_ref[...]
        gate = jax.nn.sigmoid(jnp.dot(_rms_rows(h2, pg_ref[...]).astype(BF16), wg_ref[...],
                                      preferred_element_type=F32))
        proj = jnp.dot(p_ref[...].astype(BF16), wp_ref[...], preferred_element_type=F32)
        o_ref[...] = h2 + gate * proj


def conv_ffn_ple(h, ys, wos, norm_g, w_up, conv_w, conv_b, w_down, p, ple_g, w_gate, w_proj, seq_len, tm, tf):
    t, d = h.shape
    n_in = len(ys)
    f = w_down.shape[0]
    pd = p.shape[1]
    nf = f // tf
    assert nf % 2 == 1
    nd = (nf + 1) // 2
    hl = FFN_HALO
    tiles_per_seq = seq_len // tm
    nhb = t // hl
    last = nf - 1
    up0 = lambda m: jnp.minimum(2 * m, last)
    up1 = lambda m: jnp.minimum(2 * m + 1, last)
    pr0 = lambda m: jnp.maximum(2 * m - 1, 0)
    pr1 = lambda m: 2 * m
    wspec = lambda fn, off: pl.BlockSpec((d, tf), lambda i, m: (0, off + fn(m)))
    cspec = lambda fn, off: pl.BlockSpec((3, tf), lambda i, m: (0, off + fn(m)))
    bspec = lambda fn, off: pl.BlockSpec((1, tf), lambda i, m: (0, off + fn(m)))
    dspec = lambda fn: pl.BlockSpec((tf, d), lambda i, m: (fn(m), 0))
    cb = conv_b.reshape(1, 2 * f)
    zbuf = pltpu.VMEM((tm + 2 * hl, tf), F32)
    prev_blk = lambda i, m: (jnp.maximum(i * (tm // hl) - 1, 0), 0)
    next_blk = lambda i, m: (jnp.minimum((i + 1) * (tm // hl), nhb - 1), 0)
    ksum = sum(y.shape[1] for y in ys)
    vmem = (2 * (2 * tm * d * 4 + 2 * hl * d * 4 + 4 * d * tf * 2 + 2 * tf * d * 2 + tm * pd * 4 + d * d * 2 + pd * d * 2
                 + (tm + 2 * hl) * ksum * 2 + ksum * d * 2)
            + (tm + 2 * hl) * d * 2 + 4 * (tm + 2 * hl) * tf * 4 + 8 * tm * tf * 4 + 2 * tm * d * 4 + (4 << 20))
    return pl.pallas_call(
        functools.partial(_ffn_ple_body, tm=tm, tiles_per_seq=tiles_per_seq, nd=nd, n_in=n_in),
        grid=(t // tm, nd),
        in_specs=[
            pl.BlockSpec((tm, d), lambda i, m: (i, 0)),
            pl.BlockSpec((hl, d), prev_blk),
            pl.BlockSpec((hl, d), next_blk),
            pl.BlockSpec((1, d), lambda i, m: (0, 0)),
            *[pl.BlockSpec((tm, y.shape[1]), lambda i, m: (i, 0)) for y in ys],
            *[pl.BlockSpec((hl, y.shape[1]), prev_blk) for y in ys],
            *[pl.BlockSpec((hl, y.shape[1]), next_blk) for y in ys],
            *[pl.BlockSpec(w.shape, lambda i, m: (0, 0)) for w in wos],
            wspec(up0, 0), wspec(up0, nf), wspec(up1, 0), wspec(up1, nf),
            cspec(pr0, 0), cspec(pr0, nf), bspec(pr0, 0), bspec(pr0, nf), dspec(pr0),
            cspec(pr1, 0), cspec(pr1, nf), bspec(pr1, 0), bspec(pr1, nf), dspec(pr1),
            pl.BlockSpec((1, d), lambda i, m: (0, 0)),
            pl.BlockSpec((d, d), lambda i, m: (0, 0)),
            pl.BlockSpec((tm, pd), lambda i, m: (i, 0)),
            pl.BlockSpec((pd, d), lambda i, m: (0, 0)),
        ],
        out_specs=pl.BlockSpec((tm, d), lambda i, m: (i, 0)),
        out_shape=jax.ShapeDtypeStruct((t, d), F32),
        scratch_shapes=[pltpu.VMEM((tm + 2 * hl, d), BF16), zbuf, zbuf, zbuf, zbuf, pltpu.VMEM((tm, tf), BF16)],
        compiler_params=_params(("parallel", "arbitrary"), vmem),
        name="conv_ffn_ple",
    )(h, h, h, norm_g.reshape(1, d), *ys, *ys, *ys, *wos,
      w_up, w_up, w_up, w_up,
      conv_w, conv_w, cb, cb, w_down,
      conv_w, conv_w, cb, cb, w_down,
      ple_g.reshape(1, d), w_gate, p, w_proj)


NA_KNORM_ROWS = 512


def _pair_head_rms(x, g, first):
    xx = x * x
    ss_a = jnp.sum(jnp.where(first, xx, 0.0), axis=-1, keepdims=True)
    ss_b = jnp.sum(jnp.where(first, 0.0, xx), axis=-1, keepdims=True)
    inv = lax.rsqrt(jnp.where(first, ss_a, ss_b) * (1.0 / NA_HEAD_DIM) + EPS)
    return x * inv * g


def _natten_body(q_ref, k_ref, v_ref, qg_ref, kg_ref, b_ref, o_ref, kn_ref, *, rows, rb):
    ib = pl.program_id(2)
    first = lax.broadcasted_iota(jnp.int32, (1, LANES), 1) < NA_HEAD_DIM
    kwin = NA_WIN_ROWS * GRID_W
    seq = rows * GRID_W

    @pl.when(ib == 0)
    def _():
        def norm_rows(c, carry):
            sl = pl.ds(pl.multiple_of(c * NA_KNORM_ROWS, NA_KNORM_ROWS), NA_KNORM_ROWS)
            kn_ref[sl, :] = _pair_head_rms(k_ref[0, sl, :].astype(F32), kg_ref[...], first).astype(BF16)
            return carry
        lax.fori_loop(0, seq // NA_KNORM_ROWS, norm_rows, 0)

    qn = _pair_head_rms(q_ref[0].astype(F32), qg_ref[...], first)

    combos = [(r, a) for r in range(rb) for a in range(2)]
    kstart, var = [], []
    for r in range(rb):
        i = ib * rb + r
        rs = jnp.clip(i - NA_WIN_ROWS // 2, 0, rows - NA_WIN_ROWS)
        var.append(i - rs)
        kstart.append(pl.multiple_of(rs * GRID_W, GRID_W))
    s = []
    for r, a in combos:
        q = qn[r * GRID_W:(r + 1) * GRID_W, :]
        keep = first if a == 0 else jnp.logical_not(first)
        qa = jnp.where(keep, q, 0.0).astype(BF16)
        k = kn_ref[pl.ds(kstart[r], kwin), :]
        s.append(lax.dot_general(qa, k, (((1,), (1,)), ((), ())), preferred_element_type=F32) + b_ref[a, var[r]])
    p, l = [], []
    for n in range(len(combos)):
        e = jnp.exp(s[n] - jnp.max(s[n], axis=-1, keepdims=True))
        l.append(jnp.sum(e, axis=-1, keepdims=True))
        p.append(e.astype(BF16))
    o = [jnp.dot(p[n], v_ref[0, pl.ds(kstart[r], kwin), :], preferred_element_type=F32) / l[n]
         for n, (r, a) in enumerate(combos)]
    for r in range(rb):
        o_ref[0, r * GRID_W:(r + 1) * GRID_W, :] = jnp.where(first, o[2 * r], o[2 * r + 1]).astype(o_ref.dtype)


def _natten_bias(rpb):
    kh, kw, w = NA_WIN_ROWS, NA_WIN_COLS, GRID_W
    var = np.arange(kh)
    jr = np.arange(kh)
    dr = jr[None, :] + (kh - 1) - var[:, None]
    sel_r = (dr[:, :, None] == np.arange(2 * kh - 1)[None, None, :]).astype(np.float32)
    col = np.arange(w)
    cstart = np.clip(col - kw // 2, 0, w - kw)
    dc = col[None, :] - col[:, None] + (kw - 1)
    valid = (col[None, :] >= cstart[:, None]) & (col[None, :] < cstart[:, None] + kw)
    sel_c = ((dc[:, :, None] == np.arange(2 * kw - 1)[None, None, :]) & valid[:, :, None]).astype(np.float32)
    b = jnp.einsum('hrd,vjr,cgd->hvcjg', rpb, jnp.asarray(sel_r), jnp.asarray(sel_c), precision=HIGHEST)
    mask = np.where(valid, 0.0, NEG_BIG).astype(np.float32)[None, None, :, None, :]
    b = b + jnp.asarray(mask)
    return b.reshape(rpb.shape[0], kh, w, kh * w)


def neighbourhood_attention(z, q_g, k_g, rpb, rb):
    bsz, seq, d3 = z.shape
    d = d3 // 3
    rows = seq // GRID_W
    bias = _natten_bias(rpb)
    hg = d // LANES
    kwin = NA_WIN_ROWS * GRID_W
    qg = jnp.tile(q_g.reshape(1, NA_HEAD_DIM), (1, 2)) * (NA_HEAD_DIM ** -0.5)
    kg = jnp.tile(k_g.reshape(1, NA_HEAD_DIM), (1, 2))
    vmem = (2 * (2 * seq * LANES * 2 + 2 * rb * GRID_W * LANES * 2 + 2 * NA_WIN_ROWS * GRID_W * kwin * 4)
            + seq * LANES * 2 + 6 * rb * GRID_W * kwin * 4 + (4 << 20))
    return pl.pallas_call(
        functools.partial(_natten_body, rows=rows, rb=rb),
        grid=(bsz, hg, rows // rb),
        in_specs=[pl.BlockSpec((1, rb * GRID_W, LANES), lambda b, h, i: (b, i, h)),
                  pl.BlockSpec((1, seq, LANES), lambda b, h, i: (b, 0, hg + h)),
                  pl.BlockSpec((1, seq, LANES), lambda b, h, i: (b, 0, 2 * hg + h)),
                  pl.BlockSpec((1, LANES), lambda b, h, i: (0, 0)),
                  pl.BlockSpec((1, LANES), lambda b, h, i: (0, 0)),
                  pl.BlockSpec((2, NA_WIN_ROWS, GRID_W, kwin), lambda b, h, i: (h, 0, 0, 0))],
        out_specs=pl.BlockSpec((1, rb * GRID_W, LANES), lambda b, h, i: (b, i, h)),
        out_shape=jax.ShapeDtypeStruct((bsz, seq, d), BF16),
        scratch_shapes=[pltpu.VMEM((seq, LANES), BF16)],
        compiler_params=_params(("parallel", "parallel", "arbitrary"), vmem),
        name="natten",
    )(z, z, z, qg, kg, bias)


def _hy_filter_body(f_ref, w1_ref, b1_ref, w2_ref, b2_ref, w3_ref, b3_ref, w4_ref, fr_ref, dl_ref,
                    h_ref, ss_ref, *, c):
    d = functools.partial(jnp.dot, preferred_element_type=F32, precision=HIGHEST)
    f = f_ref[...]
    fr = fr_ref[...]
    x = jnp.sin(fr * (d(f, w1_ref[...]) + b1_ref[...]))
    x = jnp.sin(fr * (d(x, w2_ref[...]) + b2_ref[...]))
    x = jnp.sin(fr * (d(x, w3_ref[...]) + b3_ref[...]))
    h = d(x, w4_ref[...])
    hf = h[:, 0:c] * jnp.exp(-f[:, 0:1] * dl_ref[...])
    hb = h[:, c:2 * c] * jnp.exp(-f[:, FILTER_EMB_PAD:FILTER_EMB_PAD + 1] * dl_ref[...])
    h_ref[0] = hf
    h_ref[1] = hb * f[:, FILTER_EMB_PAD + FILTER_EMB:FILTER_EMB_PAD + FILTER_EMB + 1]

    @pl.when(pl.program_id(0) == 0)
    def _():
        ss_ref[...] = jnp.zeros_like(ss_ref)

    ss_ref[...] += jnp.sum(hf * hf, axis=0, keepdims=True) + jnp.sum(hb * hb, axis=0, keepdims=True)


def hyena_filter(seq, w1, b1, w2, b2, w3, b3, w4, freq, tl):
    c = w4.shape[1] // 2
    hid = w1.shape[1]
    t = np.linspace(0.0, 1.0, seq, dtype=np.float32)[:, None]
    bands = (FILTER_EMB - 1) // 2
    ang = (np.float32(2.0 * math.pi / seq) * np.arange(seq, dtype=np.float32)[:, None]
           * np.linspace(1e-4, bands - 1, bands, dtype=np.float32)[None])
    feats = np.concatenate([t, np.cos(ang), -np.sin(ang), np.ones((seq, 1), np.float32),
                            np.zeros((seq, FILTER_EMB_PAD - FILTER_EMB - 1), np.float32)], axis=-1)
    feats_b = np.concatenate([feats[0:1], feats[:0:-1]], axis=0)
    feats_b[0, FILTER_EMB] = 0.0
    feats2 = np.concatenate([feats, feats_b], axis=1)
    pair = lambda a: jnp.concatenate([a, a], axis=-1)
    bdiag = lambda a, b: jnp.concatenate(
        [jnp.concatenate([a, jnp.zeros((a.shape[0], b.shape[1]), F32)], axis=1),
         jnp.concatenate([jnp.zeros((b.shape[0], a.shape[1]), F32), b], axis=1)], axis=0)
    w1p = jnp.concatenate([w1, jnp.zeros((FILTER_EMB_PAD - FILTER_EMB, hid), F32)], axis=0)
    deltas = np.abs(np.linspace(DECAY_MIN, DECAY_MAX, c, dtype=np.float32))[None]
    full = lambda a: pl.BlockSpec(a.shape, lambda i: (0,) * a.ndim)
    args = [jnp.asarray(feats2), bdiag(w1p, w1p), pair(b1.reshape(1, hid)), bdiag(w2, w2), pair(b2.reshape(1, hid)),
            bdiag(w3, w3), pair(b3.reshape(1, hid)), bdiag(w4[:, :c], w4[:, c:]), pair(freq.reshape(1, hid)),
            jnp.asarray(deltas)]
    in_specs = [pl.BlockSpec((tl, 2 * FILTER_EMB_PAD), lambda i: (i, 0))] + [full(a) for a in args[1:]]
    kern, ss = pl.pallas_call(
        functools.partial(_hy_filter_body, c=c),
        grid=(seq // tl,),
        in_specs=in_specs,
        out_specs=[pl.BlockSpec((2, tl, c), lambda i: (0, i, 0)), pl.BlockSpec((1, c), lambda i: (0, 0))],
        out_shape=[jax.ShapeDtypeStruct((2, seq, c), F32), jax.ShapeDtypeStruct((1, c), F32)],
        compiler_params=_params(("arbitrary",), 40 << 20),
        name="hyena_filter",
    )(*args)
    return kern.reshape(2 * seq, c), ss


def _hy_gate_body(z0_ref, z1_ref, zv_ref, w0_ref, w1_ref, wv_ref, b0_ref, b1_ref, bv_ref, s_ref, x0_ref,
                  *, seq):
    row = lax.broadcasted_iota(jnp.int32, (seq, LANES), 0)
    top = row == 0
    bot = row == seq - 1

    def conv(z_ref, w_ref, b_ref):
        z = z_ref[0]
        w = w_ref[...]
        zp = jnp.where(top, 0.0, pltpu.roll(z, 1, 0))
        zn = jnp.where(bot, 0.0, pltpu.roll(z, seq - 1, 0))
        return zp * w[0:1, :] + z * w[1:2, :] + zn * w[2:3, :] + b_ref[...]

    x0_ref[0] = conv(z0_ref, w0_ref, b0_ref).astype(x0_ref.dtype)
    s_ref[0] = (conv(zv_ref, wv_ref, bv_ref) * conv(z1_ref, w1_ref, b1_ref)).astype(s_ref.dtype)


def hyena_gate(z, short_w, short_b, col0):
    bsz, seq, _ = z.shape
    c = HYENA_WIDTH
    nb = c // LANES
    o0 = col0 // LANES
    sb = short_b.reshape(1, 3 * c)
    zspec = lambda off: pl.BlockSpec((1, seq, LANES), lambda b, j: (b, 0, off + j))
    wspec = lambda off: pl.BlockSpec((3, LANES), lambda b, j: (0, off + j))
    bspec = lambda off: pl.BlockSpec((1, LANES), lambda b, j: (0, off + j))
    out = jax.ShapeDtypeStruct((bsz, seq, c), BF16)
    return pl.pallas_call(
        functools.partial(_hy_gate_body, seq=seq),
        grid=(bsz, nb),
        in_specs=[zspec(o0), zspec(o0 + nb), zspec(o0 + 2 * nb),
                  wspec(0), wspec(nb), wspec(2 * nb), bspec(0), bspec(nb), bspec(2 * nb)],
        out_specs=[pl.BlockSpec((1, seq, LANES), lambda b, j: (b, 0, j))] * 2,
        out_shape=[out, out],
        compiler_params=_params(("parallel", "parallel"), 2 * 5 * seq * LANES * 4 + 8 * seq * LANES * 4 + (4 << 20)),
        name="hyena_gate",
    )(z, z, z, short_w, short_w, short_w, sb, sb, sb)


def _dft_consts(n, n_in):
    k = np.arange(n)[:, None].astype(np.float64)
    t = np.arange(n_in)[None, :].astype(np.float64)
    th = 2.0 * np.pi * k * t / n
    w_first = np.concatenate([np.cos(th), -np.sin(th)], axis=0)
    w_last = np.concatenate([np.cos(th).T, -np.sin(th).T], axis=1)
    tt = np.arange(n)[None, :].astype(np.float64)
    th2 = 2.0 * np.pi * k * tt / n
    cm, sm = np.cos(th2), np.sin(th2)
    m_mid = np.block([[cm, sm], [-sm, cm]])
    ph = 2.0 * np.pi * k * tt / (n * n)
    tw = np.stack([np.cos(ph), np.sin(ph)], axis=1)[..., None]
    return w_first, w_last, m_mid, tw


def _hy_stage1_body(w_ref, x_ref, o_ref, *, tb):
    w = w_ref[...]
    for j in range(tb):
        o_ref[0, j] = jnp.dot(w, x_ref[0, j], preferred_element_type=F32).astype(o_ref.dtype)


def hyena_stage(w, x, tb, out_dtype):
    bsz, n2, kk, c = x.shape
    m = w.shape[0]
    osz = jnp.dtype(out_dtype).itemsize
    vmem = 2 * (tb * kk * c * 2 + tb * m * c * osz + m * kk * 2) + 2 * m * c * 4 + (4 << 20)
    return pl.pallas_call(
        functools.partial(_hy_stage1_body, tb=tb),
        grid=(bsz, n2 // tb),
        in_specs=[pl.BlockSpec((m, kk), lambda b, j: (0, 0)),
                  pl.BlockSpec((1, tb, kk, c), lambda b, j: (b, j, 0, 0))],
        out_specs=pl.BlockSpec((1, tb, m, c), lambda b, j: (b, j, 0, 0)),
        out_shape=jax.ShapeDtypeStruct((bsz, n2, m, c), out_dtype),
        compiler_params=_params(("parallel", "parallel"), vmem),
        name="hyena_dft_outer",
    )(w, x)


def _twiddle(xr, xi, tc, ts, sign):
    return xr * tc + sign * (xi * ts), xi * tc - sign * (xr * ts)


HYENA_K1_PER_STEP = 8


def _hy_twiddled_input(a_ref, tw_ref, j, n):
    a = a_ref[0, j].astype(F32)
    xr, xi = _twiddle(a[0:n], a[n:2 * n], tw_ref[j, 0], tw_ref[j, 1], 1.0)
    return jnp.concatenate([xr, xi], axis=0).astype(BF16)


def _hy_spec_body(m_ref, tw_ref, a_ref, o_ref, *, n, kb):
    x = [_hy_twiddled_input(a_ref, tw_ref, j, n) for j in range(kb)]
    for j in range(kb):
        o_ref[0, j] = jnp.dot(m_ref[...], x[j], preferred_element_type=F32)


def _hy_mid_body(m_ref, mt_ref, tw_ref, hs_ref, a_ref, o_ref, *, n, kb):
    x = [_hy_twiddled_input(a_ref, tw_ref, j, n) for j in range(kb)]
    s = [jnp.dot(m_ref[...], x[j], preferred_element_type=F32) for j in range(kb)]
    p = []
    for j in range(kb):
        hs = hs_ref[0, j]
        sr, si = s[j][0:n], s[j][n:2 * n]
        hr, hi = hs[0:n], hs[n:2 * n]
        p.append(jnp.concatenate([sr * hr - si * hi, sr * hi + si * hr], axis=0).astype(BF16))
    y = [jnp.dot(mt_ref[...], p[j], preferred_element_type=F32) for j in range(kb)]
    for j in range(kb):
        yr, yi = _twiddle(y[j][0:n], y[j][n:2 * n], tw_ref[j, 0], tw_ref[j, 1], -1.0)
        o_ref[0, j] = jnp.concatenate([yr, yi], axis=0).astype(o_ref.dtype)


def hyena_spectrum(a, m_mid, tw):
    _, n, n2x, c = a.shape
    kb = min(HYENA_K1_PER_STEP, n)
    return pl.pallas_call(
        functools.partial(_hy_spec_body, n=n, kb=kb),
        grid=(n // kb, 1),
        in_specs=[pl.BlockSpec((n2x, n2x), lambda k, b: (0, 0)),
                  pl.BlockSpec((kb, 2, n, 1), lambda k, b: (k, 0, 0, 0)),
                  pl.BlockSpec((1, kb, n2x, c), lambda k, b: (b, k, 0, 0))],
        out_specs=pl.BlockSpec((1, kb, n2x, c), lambda k, b: (b, k, 0, 0)),
        out_shape=jax.ShapeDtypeStruct((1, n, n2x, c), F32),
        compiler_params=_params(("parallel", "parallel"), 40 << 20),
        name="hyena_filter_spectrum",
    )(m_mid, tw, a)


def hyena_mid(a, hspec, m_mid, m_mid_t, tw):
    bsz, n, n2x, c = a.shape
    kb = min(HYENA_K1_PER_STEP, n)
    return pl.pallas_call(
        functools.partial(_hy_mid_body, n=n, kb=kb),
        grid=(n // kb, bsz),
        in_specs=[pl.BlockSpec((n2x, n2x), lambda k, b: (0, 0)),
                  pl.BlockSpec((n2x, n2x), lambda k, b: (0, 0)),
                  pl.BlockSpec((kb, 2, n, 1), lambda k, b: (k, 0, 0, 0)),
                  pl.BlockSpec((1, kb, n2x, c), lambda k, b: (0, k, 0, 0)),
                  pl.BlockSpec((1, kb, n2x, c), lambda k, b: (b, k, 0, 0))],
        out_specs=pl.BlockSpec((1, kb, n2x, c), lambda k, b: (b, k, 0, 0)),
        out_shape=jax.ShapeDtypeStruct((bsz, n, n2x, c), BF16),
        compiler_params=_params(("parallel", "parallel"), 40 << 20),
        name="hyena_dft_mid",
    )(m_mid, m_mid_t, tw, hspec, a)


def _hy_combine_body(y_ref, s_ref, x0_ref, sc_ref, bi_ref, o_ref):
    o_ref[...] = ((y_ref[...] * sc_ref[...] + s_ref[...] * bi_ref[...]) * x0_ref[...]).astype(o_ref.dtype)


def hyena_combine(y, s, x0, scale, bias, tm):
    t, c = y.shape
    row = lambda: pl.BlockSpec((tm, c), lambda i: (i, 0))
    vec = lambda: pl.BlockSpec((1, c), lambda i: (0, 0))
    return pl.pallas_call(
        _hy_combine_body,
        grid=(t // tm,),
        in_specs=[row(), row(), row(), vec(), vec()],
        out_specs=row(),
        out_shape=jax.ShapeDtypeStruct((t, c), BF16),
        compiler_params=_params(("parallel",), 2 * 4 * tm * c * 4 + (4 << 20)),
        name="hyena_combine",
    )(y, s, x0, scale.reshape(1, c), bias.reshape(1, c))


def hyena_bidir(z, col0, short_w, short_b, f_w1, f_b1, f_w2, f_b2, f_w3, f_b3, f_w4, f_freq, bias):
    bsz, seq, _ = z.shape
    c = HYENA_WIDTH
    n = int(round(math.sqrt(2 * seq)))
    assert n * n == 2 * seq and n % 16 == 0
    nh = n // 2
    tb = 8

    w_first, w_last, m_mid, tw = _dft_consts(n, n)
    w_first_x = jnp.asarray(w_first[:, :nh], BF16)
    w_first_h = jnp.asarray(w_first, BF16)
    w_last_y = jnp.asarray(w_last[:nh], BF16)
    m_mid_j = jnp.asarray(m_mid, BF16)
    m_mid_t = jnp.asarray(m_mid.T, BF16)
    tw_j = jnp.asarray(tw, F32)

    kern, ss = hyena_filter(seq, f_w1, f_b1, f_w2, f_b2, f_w3, f_b3, f_w4, f_freq, tl=min(seq, 1024))
    scale = lax.rsqrt(ss[0] + EPS) * (1.0 / (2 * seq))
    kern_t = jnp.transpose(kern.reshape(1, n, n, c), (0, 2, 1, 3)).astype(BF16)
    ah = hyena_stage(w_first_h, kern_t, tb, BF16)
    ah = jnp.transpose(ah.reshape(1, n, 2, n, c), (0, 3, 2, 1, 4)).reshape(1, n, 2 * n, c)
    hspec = hyena_spectrum(ah, m_mid_j, tw_j)

    s, x0 = hyena_gate(z, short_w, short_b, col0)
    s_t = jnp.transpose(s.reshape(bsz, nh, n, c), (0, 2, 1, 3)).astype(BF16)
    a = hyena_stage(w_first_x, s_t, tb, BF16)
    a = jnp.transpose(a.reshape(bsz, n, 2, n, c), (0, 3, 2, 1, 4)).reshape(bsz, n, 2 * n, c)
    bm = hyena_mid(a, hspec, m_mid_j, m_mid_t, tw_j)
    bm = jnp.transpose(bm.reshape(bsz, n, 2, n, c), (0, 3, 2, 1, 4)).reshape(bsz, n, 2 * n, c)
    y = hyena_stage(w_last_y, bm, tb, BF16)
    y = jnp.transpose(y, (0, 2, 1, 3)).reshape(bsz * seq, c)
    return hyena_combine(y, s.reshape(bsz * seq, c), x0.reshape(bsz * seq, c), scale, bias,
                         tm=min(bsz * seq, 2048))


RWKV_PAIRS = RWKV_WIDTH // LANES
RWKV_PREP_CHUNKS = 4


def _pair_rows(x, first):
    return jnp.concatenate([jnp.where(first, x, 0.0), jnp.where(first, 0.0, x)], axis=0)


def _fold_rows(x2, ch):
    return x2[0:ch] + x2[ch:2 * ch]


def _rwkv_prep_body(z_ref, halo_ref, mu_ref, w0_ref, wup_ref, a0_ref, aup_ref, kk_ref, ka_ref, rk_ref, sel_ref,
                    wt_ref, rt_ref, kb_ref, ab_ref, v_ref, ut_ref, yk_ref, ara_ref, ge_ref, bonus_ref,
                    *, reverse, nsteps, cps):
    ch = RWKV_CHUNK
    cw = RWKV_WIDTH
    rows = cps * ch
    c_idx = pl.program_id(1)

    zs = z_ref[0]
    row = lax.broadcasted_iota(jnp.int32, (rows, 1), 0)
    if reverse:
        nbr = jnp.where(c_idx == nsteps - 1, 0.0, halo_ref[0, 0:1, :])
        shifted = jnp.where(row == rows - 1, jnp.broadcast_to(nbr, zs.shape), pltpu.roll(zs, rows - 1, 0))
    else:
        nbr = jnp.where(c_idx == 0, 0.0, halo_ref[0, SUBLANES - 1:SUBLANES, :])
        shifted = jnp.where(row == 0, jnp.broadcast_to(nbr, zs.shape), pltpu.roll(zs, 1, 0))
    zd = zs + (shifted - zs) * mu_ref[...]

    r = zd[:, 0:cw]
    k = zd[:, cw:2 * cw]
    v = zd[:, 2 * cw:3 * cw]
    lora = zd[:, 3 * cw:3 * cw + DECAY_RANK + ICLR_RANK]
    x = w0_ref[...] + _bdot(jnp.tanh(lora), wup_ref[...])
    log_w = jnp.minimum(x, 0.0) - jnp.log(1.0 + jnp.exp(-jnp.abs(x))) - 0.5
    lnw = -jnp.exp(log_w)
    a = jax.nn.sigmoid(a0_ref[...] + _bdot(lora, aup_ref[...]))
    sel = sel_ref[...]
    kkr = k * kk_ref[...]
    kkn = jnp.sqrt(_bdot(kkr * kkr, sel))
    kk = kkr / jnp.maximum(kkn, 1e-12)
    k2 = k * (1.0 + (a - 1.0) * ka_ref[...])
    ah = kk * a
    bonus_ref[0] = (_bdot(r * k2 * rk_ref[...], sel) * v).astype(bonus_ref.dtype)

    ri = lax.broadcasted_iota(jnp.int32, (rows, rows), 0)
    ci = lax.broadcasted_iota(jnp.int32, (rows, rows), 1)
    csh = ch.bit_length() - 1
    tri = ((ci >= ri) if reverse else (ci <= ri)) & ((ri >> csh) == (ci >> csh))
    cs = _dot_exact_lhs(tri.astype(BF16), lnw)
    total = jnp.concatenate(
        [jnp.broadcast_to(jnp.sum(lnw[g * ch:(g + 1) * ch], axis=0, keepdims=True), (ch, cw)) for g in range(cps)],
        axis=0)
    e_inv = jnp.exp(-cs)
    e_end = jnp.exp(total - cs)
    rt = r * jnp.exp(cs)
    kkt = kk * jnp.exp(cs - lnw)
    kh = k2 * e_inv
    ahh = ah * e_inv
    rt_ref[0] = rt.astype(rt_ref.dtype)
    kb_ref[0] = (k2 * e_end).astype(kb_ref.dtype)
    ab_ref[0] = (ah * e_end).astype(ab_ref.dtype)
    v_ref[0] = v.astype(v_ref.dtype)
    for g in range(cps):
        ge_ref[0, g] = jnp.exp(total[g * ch:g * ch + 1])

    r2 = lax.broadcasted_iota(jnp.int32, (2 * ch, 2 * ch), 0)
    c2 = lax.broadcasted_iota(jnp.int32, (2 * ch, 2 * ch), 1)
    same_head = (r2 < ch) == (c2 < ch)
    si = r2 & (ch - 1)
    sj = c2 & (ch - 1)
    before = (sj > si) if reverse else (sj < si)
    strict = same_head & before
    incl = same_head & jnp.logical_or(before, si == sj)
    eye = (r2 == c2).astype(F32)
    first = lax.broadcasted_iota(jnp.int32, (1, LANES), 1) < RWKV_HEAD_DIM

    units = [(g, p) for g in range(cps) for p in range(RWKV_PAIRS)]
    un = range(len(units))
    tile = lambda arr, g, p: arr[g * ch:(g + 1) * ch, p * LANES:(p + 1) * LANES]
    l2 = [_pair_rows(tile(kkt, g, p), first) for g, p in units]
    l4 = [jnp.concatenate([l2[n], _pair_rows(tile(rt, g, p), first)], axis=0) for n, (g, p) in enumerate(units)]
    v2 = [_pair_rows(tile(v, g, p), first) for g, p in units]
    sk = [_bdot_nt(l4[n], jnp.concatenate([tile(kh, g, p)] * 2, axis=0)) for n, (g, p) in enumerate(units)]
    sa = [_bdot_nt(l4[n], jnp.concatenate([tile(ahh, g, p)] * 2, axis=0)) for n, (g, p) in enumerate(units)]
    nm = [jnp.where(strict, sa[n][0:2 * ch], 0.0) for n in un]
    for n, (g, p) in enumerate(units):
        ara_ref[0, g, p] = jnp.where(incl, sa[n][2 * ch:4 * ch], 0.0).astype(ara_ref.dtype)
    akr = [jnp.concatenate([jnp.where(strict, sk[n][0:2 * ch], 0.0),
                            jnp.where(incl, sk[n][2 * ch:4 * ch], 0.0)], axis=0) for n in un]
    av = [_bdot(akr[n], v2[n]) for n in un]

    lvl = (si >> 1) == (sj >> 1)
    t = [eye - jnp.where(lvl, nm[n], 0.0) for n in un]
    s = 2
    while s < ch:
        sh = s.bit_length() - 1
        lvl = ((si >> (sh + 1)) == (sj >> (sh + 1))) & ((si >> sh) != (sj >> sh))
        x = [_bdot(jnp.where(lvl, nm[n], 0.0), t[n]) for n in un]
        t = [t[n] - _bdot(t[n], x[n]) for n in un]
        s *= 2

    wu = [_bdot(t[n], jnp.concatenate([l2[n], av[n][0:2 * ch]], axis=1)) for n in un]
    gather = lambda f: jnp.concatenate(
        [jnp.concatenate([f(g * RWKV_PAIRS + p) for p in range(RWKV_PAIRS)], axis=1) for g in range(cps)], axis=0)
    wt_ref[0] = gather(lambda n: _fold_rows(wu[n][:, 0:LANES], ch)).astype(wt_ref.dtype)
    ut_ref[0] = gather(lambda n: _fold_rows(wu[n][:, LANES:2 * LANES], ch)).astype(ut_ref.dtype)
    yk_ref[0] = gather(lambda n: _fold_rows(av[n][2 * ch:4 * ch], ch)).astype(yk_ref.dtype)


def rwkv7_prep(z, mu, w0, w_up, a0, a_up, k_k, k_a, r_k, reverse):
    bsz, seq, _ = z.shape
    ch = RWKV_CHUNK
    cw = RWKV_WIDTH
    nc = seq // ch
    cps = min(RWKV_PREP_CHUNKS, nc)
    nsteps = nc // cps
    cpb = cps * ch // SUBLANES
    nhb = seq // SUBLANES
    sel_np = (np.arange(cw)[:, None] // RWKV_HEAD_DIM == np.arange(cw)[None, :] // RWKV_HEAD_DIM)
    sel = jnp.asarray(sel_np, BF16)
    zero_pad = jnp.zeros((DECAY_RANK, cw), F32)
    wup = jnp.concatenate([w_up, zero_pad], axis=0).astype(BF16)
    aup = jnp.concatenate([zero_pad, a_up], axis=0).astype(BF16)
    vec = lambda a: a.reshape(1, -1)
    params = [vec(mu), vec(w0), wup, vec(a0), aup, vec(k_k), vec(k_a), vec(r_k), sel]
    full = lambda a: pl.BlockSpec(a.shape, lambda b, c: (0,) * a.ndim)
    if reverse:
        halo = pl.BlockSpec((1, SUBLANES, RWKV_SHIFT_COLS), lambda b, c: (b, jnp.minimum((c + 1) * cpb, nhb - 1), 0))
    else:
        halo = pl.BlockSpec((1, SUBLANES, RWKV_SHIFT_COLS), lambda b, c: (b, jnp.maximum(c * cpb - 1, 0), 0))
    tok = pl.BlockSpec((1, cps * ch, cw), lambda b, c: (b, c, 0))
    tok_bf = jax.ShapeDtypeStruct((bsz, seq, cw), BF16)
    return pl.pallas_call(
        functools.partial(_rwkv_prep_body, reverse=reverse, nsteps=nsteps, cps=cps),
        grid=(bsz, nsteps),
        in_specs=[pl.BlockSpec((1, cps * ch, RWKV_SHIFT_COLS), lambda b, c: (b, c, 0)), halo]
                 + [full(a) for a in params],
        out_specs=[tok] * 7
                  + [pl.BlockSpec((1, cps, RWKV_PAIRS, 2 * ch, 2 * ch), lambda b, c: (b, c, 0, 0, 0)),
                     pl.BlockSpec((1, cps, 1, cw), lambda b, c: (b, c, 0, 0)),
                     tok],
        out_shape=[tok_bf, tok_bf, tok_bf, tok_bf, tok_bf, tok_bf, tok_bf,
                   jax.ShapeDtypeStruct((bsz, nc, RWKV_PAIRS, 2 * ch, 2 * ch), BF16),
                   jax.ShapeDtypeStruct((bsz, nc, 1, cw), F32),
                   tok_bf],
        compiler_params=_params(("parallel", "parallel"), 40 << 20),
        name="rwkv7_prep_bwd" if reverse else "rwkv7_prep_fwd",
    )(z, z, *params)


def _rwkv_scan_body(*refs, bsz, final):
    if final:
        (wt_ref, rt_ref, kb_ref, ab_ref, v_ref, ut_ref, yk_ref, ara_ref, ge_ref,
         yin_ref, b1_ref, b2_ref, gd_ref, gup_ref, lnw_ref, lnb_ref, sel_ref, o_ref, st_ref) = refs
    else:
        (wt_ref, rt_ref, kb_ref, ab_ref, v_ref, ut_ref, yk_ref, ara_ref, ge_ref, o_ref, st_ref) = refs
    ch = RWKV_CHUNK

    @pl.when(pl.program_id(0) == 0)
    def _():
        st_ref[...] = jnp.zeros_like(st_ref)

    first = lax.broadcasted_iota(jnp.int32, (1, LANES), 1) < RWKV_HEAD_DIM
    blk = ((lax.broadcasted_iota(jnp.int32, (LANES, LANES), 0) < RWKV_HEAD_DIM)
           == (lax.broadcasted_iota(jnp.int32, (LANES, LANES), 1) < RWKV_HEAD_DIM))
    idx = [(b, p) for b in range(bsz) for p in range(RWKV_PAIRS)]
    sl = [slice(p * LANES, (p + 1) * LANES) for p in range(RWKV_PAIRS)]
    st = [st_ref[b * RWKV_PAIRS + p] for b, p in idx]
    lm0 = [_bdot_nt(jnp.concatenate([wt_ref[b, :, sl[p]], rt_ref[b, :, sl[p]]], axis=0), st[n])
           for n, (b, p) in enumerate(idx)]
    u = [lm0[n][0:ch] + ut_ref[b, :, sl[p]] for n, (b, p) in enumerate(idx)]
    yr = [_bdot(ara_ref[b, 0, p], _pair_rows(u[n], first)) for n, (b, p) in enumerate(idx)]
    upd = [_bdot_tn(jnp.concatenate([v_ref[b, :, sl[p]].astype(F32), -u[n]], axis=0),
                    jnp.concatenate([kb_ref[b, :, sl[p]], ab_ref[b, :, sl[p]]], axis=0))
           for n, (b, p) in enumerate(idx)]
    for n, (b, p) in enumerate(idx):
        st_ref[b * RWKV_PAIRS + p] = st[n] * ge_ref[b, 0, :, sl[p]] + jnp.where(blk, upd[n], 0.0)
    for b in range(bsz):
        y = jnp.concatenate(
            [lm0[b * RWKV_PAIRS + p][ch:2 * ch] + yk_ref[b, :, sl[p]] - _fold_rows(yr[b * RWKV_PAIRS + p], ch)
             for p in range(RWKV_PAIRS)], axis=1)
        if not final:
            o_ref[b] = y.astype(o_ref.dtype)
        else:
            sel = sel_ref[...]
            yt = y + yin_ref[b]
            mean = _bdot(yt, sel) * (1.0 / RWKV_HEAD_DIM)
            dv = yt - mean
            var = _bdot(dv * dv, sel) * (1.0 / RWKV_HEAD_DIM)
            yn = dv * lax.rsqrt(var + RWKV_GN_EPS) * lnw_ref[...] + lnb_ref[...]
            g = _bdot(jax.nn.sigmoid(gd_ref[b]), gup_ref[...])
            o_ref[b] = ((yn + b1_ref[b] + b2_ref[b]) * g).astype(o_ref.dtype)


def rwkv7_scan(prep, reverse, final_args=None):
    wt, rt, kb, ab, v, ut, yk, ara, ge, _ = prep
    bsz, seq, cw = wt.shape
    ch = RWKV_CHUNK
    nc = seq // ch
    cidx = (lambda c: nc - 1 - c) if reverse else (lambda c: c)
    tok = pl.BlockSpec((bsz, ch, cw), lambda c: (0, cidx(c), 0))
    in_specs = [tok] * 7 + [pl.BlockSpec((bsz, 1, RWKV_PAIRS, 2 * ch, 2 * ch), lambda c: (0, cidx(c), 0, 0, 0)),
                            pl.BlockSpec((bsz, 1, 1, cw), lambda c: (0, cidx(c), 0, 0))]
    args = [wt, rt, kb, ab, v, ut, yk, ara, ge]
    final = final_args is not None
    if final:
        z, y_in, bonus_a, bonus_b, g_up, ln_w, ln_b = final_args
        sel_np = (np.arange(cw)[:, None] // RWKV_HEAD_DIM == np.arange(cw)[None, :] // RWKV_HEAD_DIM)
        full = lambda a: pl.BlockSpec(a.shape, lambda c: (0,) * a.ndim)
        extra = [g_up.astype(BF16), ln_w.reshape(1, cw), ln_b.reshape(1, cw), jnp.asarray(sel_np, BF16)]
        in_specs += [tok, tok, tok,
                     pl.BlockSpec((bsz, ch, GATE_RANK), lambda c: (0, cidx(c), RWKV_SHIFT_COLS // GATE_RANK))]
        in_specs += [full(a) for a in extra]
        args += [y_in, bonus_a, bonus_b, z] + extra
    out_dtype = BF16
    return pl.pallas_call(
        functools.partial(_rwkv_scan_body, bsz=bsz, final=final),
        grid=(nc,),
        in_specs=in_specs,
        out_specs=tok,
        out_shape=jax.ShapeDtypeStruct((bsz, seq, cw), out_dtype),
        scratch_shapes=[pltpu.VMEM((bsz * RWKV_PAIRS, LANES, LANES), F32)],
        compiler_params=_params(("arbitrary",), 40 << 20),
        name="rwkv7_scan_bwd" if reverse else "rwkv7_scan_fwd",
    )(*args)


def rwkv7_bidir(z, mu, w0, w_up, a0, a_up, g_up, k_k, k_a, r_k, ln_w, ln_b):
    rk = r_k.reshape(-1)
    prep_f = rwkv7_prep(z, mu[0], w0[0], w_up[0], a0[0], a_up[0], k_k, k_a, rk, reverse=False)
    prep_b = rwkv7_prep(z, mu[1], w0[1], w_up[1], a0[1], a_up[1], k_k, k_a, rk, reverse=True)
    y_f = rwkv7_scan(prep_f, reverse=False)
    return rwkv7_scan(prep_b, reverse=True, final_args=(z, y_f, prep_f[9], prep_b[9], g_up, ln_w, ln_b))


def kernel(x, p, mix_norm, mix_w_in, rwkv_mu, rwkv_w0, rwkv_w_up, rwkv_a0, rwkv_a_up, rwkv_g_up, rwkv_k_k, rwkv_k_a, rwkv_r_k, rwkv_ln_w, rwkv_ln_b, hy_short_w, hy_short_b, hy_f_w1, hy_f_b1, hy_f_w2, hy_f_b2, hy_f_w3, hy_f_b3, hy_f_w4, hy_f_freq, hy_bias, mix_w_out, na_norm, na_w_qkv, na_q_g, na_k_g, na_rpb, na_w_out, ffn_norm, ffn_w_up, ffn_conv_w, ffn_conv_b, ffn_w_down, ple_norm, ple_w_gate, ple_w_proj):
    bsz, seq, d = x.shape
    depth = p.shape[0]
    t = bsz * seq
    tm = min(1024, seq)
    h = x.reshape(t, d)
    for i in range(depth):
        j = i // 2
        if i % 2 == 0:
            w_in = mix_w_in[j].astype(BF16)
            z = norm_matmul(h, mix_norm[j], w_in, tm=tm, tn=w_in.shape[1] // 2).reshape(bsz, seq, -1)
            y_a = rwkv7_bidir(z, rwkv_mu[j], rwkv_w0[j], rwkv_w_up[j], rwkv_a0[j], rwkv_a_up[j], rwkv_g_up[j],
                              rwkv_k_k[j], rwkv_k_a[j], rwkv_r_k[j], rwkv_ln_w[j], rwkv_ln_b[j])
            y_b = hyena_bidir(z, RWKV_COLS, hy_short_w[j], hy_short_b[j], hy_f_w1[j], hy_f_b1[j], hy_f_w2[j],
                              hy_f_b2[j], hy_f_w3[j], hy_f_b3[j], hy_f_w4[j], hy_f_freq[j], hy_bias[j])
            w_out = mix_w_out[j].astype(BF16)
            ys, wos = [y_a.reshape(t, -1), y_b], [w_out[:RWKV_WIDTH], w_out[RWKV_WIDTH:]]
        else:
            z = norm_matmul(h, na_norm[j], na_w_qkv[j].astype(BF16), tm=tm, tn=1024, out_dtype=BF16)
            o = neighbourhood_attention(z.reshape(bsz, seq, 3 * d), na_q_g[j], na_k_g[j], na_rpb[j], rb=32)
            ys, wos = [o.reshape(t, d)], [na_w_out[j].astype(BF16)]
        h = conv_ffn_ple(h, ys, wos, ffn_norm[i], ffn_w_up[i].astype(BF16), ffn_conv_w[i], ffn_conv_b[i],
                         ffn_w_down[i].astype(BF16), p[i].reshape(t, -1), ple_norm[i],
                         ple_w_gate[i].astype(BF16), ple_w_proj[i].astype(BF16), seq_len=seq, tm=tm, tf=256)
    return h.reshape(bsz, seq, d)
```
